```python
import jax, jax.numpy as jnp
from jax import lax
import numpy as np

D_MODEL = 4096
BATCH = 1
SEQ = 8192
DEPTH = 1

D_MIX = D_MODEL
CONV_CH = D_MIX // 4
ATTN_WIDTH = D_MIX - CONV_CH
HEAD_DIM = 128
N_HEADS = ATTN_WIDTH // HEAD_DIM
CONV_WIDTH = 3
IN_WIDTH = 3 * CONV_CH + 3 * ATTN_WIDTH
PATTERNS = ((128, 1), (512, 4), (2048, 16))
BAND = 128
ALIBI_MAX_EXP = 8.0
N_GROUPS = 8
EXPERTS_PER_GROUP = 8
N_EXPERTS = N_GROUPS * EXPERTS_PER_GROUP
TOP_K = 2
D_EXPERT = D_MODEL // 8
MOE_BLOCK = 128
EPS = 1e-6

kernel_name = "hybrid_conv_dilated_attn_hiermoe_adaln"


def rms_norm(x, g):
    xf = x.astype(jnp.float32)
    y = xf * lax.rsqrt(jnp.mean(xf * xf, axis=-1, keepdims=True) + EPS)
    return (y * g.astype(jnp.float32)).astype(x.dtype)


def causal_dwconv(u, w):
    ch = u.shape[-1]
    return lax.conv_general_dilated(
        u, w[:, None, :], window_strides=(1,), padding=[(CONV_WIDTH - 1, 0)],
        dimension_numbers=('NWC', 'WIO', 'NWC'), feature_group_count=ch)


def dilated_branch(q, k, v, slopes, window, dil):
    B, S, H, E = q.shape
    chunk = dil * BAND
    s_pad = -(-S // chunk) * chunk
    nb = s_pad // chunk

    def blocks(a):
        a = jnp.pad(a, ((0, 0), (0, s_pad - S), (0, 0), (0, 0)))
        return a.reshape(B, nb, BAND, dil, H, E)

    def with_prev(a):
        prev = jnp.pad(a, ((0, 0), (1, 0), (0, 0), (0, 0), (0, 0), (0, 0)))[:, :-1]
        return jnp.concatenate([prev, a], axis=2)

    qb = blocks(q)
    kb = with_prev(blocks(k))
    vb = with_prev(blocks(v))
    scores = jnp.einsum('bnirhe,bnurhe->bnrhiu', qb, kb)
    i = jnp.arange(BAND)[:, None]
    u = jnp.arange(2 * BAND)[None, :]
    steps = BAND + i - u
    n = jnp.arange(nb)[:, None, None]
    valid = (steps >= 0) & (steps <= window // dil) & ((n > 0) | (u >= BAND))
    alibi = -slopes[:, None, None] * (steps * dil).astype(jnp.float32)
    scores = jnp.where(valid[None, :, None, None], scores + alibi, -jnp.inf)
    m = scores.max(-1)
    p = jnp.exp(scores - m[..., None])
    s = p.sum(-1)
    o = jnp.einsum('bnrhiu,bnurhe->bnirhe', p, vb).reshape(B, s_pad, H, E)[:, :S]

    def to_seq(a):
        return a.transpose(0, 1, 4, 2, 3).reshape(B, s_pad, H)[:, :S]

    return to_seq(m), to_seq(s), o


def dilated_mixture_attention(q, k, v):
    H = q.shape[2]
    slopes = jnp.exp2(-ALIBI_MAX_EXP * jnp.arange(1, H + 1, dtype=jnp.float32) / H)
    parts = [dilated_branch(q, k, v, slopes, w, d) for (w, d) in PATTERNS]
    m_all = jnp.stack([p[0] for p in parts])
    s_all = jnp.stack([p[1] for p in parts])
    o_all = jnp.stack([p[2] for p in parts])
    wts = jnp.exp(m_all - m_all.max(0))
    den = (wts * s_all).sum(0)
    return (wts[..., None] * o_all).sum(0) / den[..., None]


def token_mixer(h, w_in, conv_w, q_norm_g, k_norm_g, g_branch, w_out):
    B, S, _ = h.shape
    z = h @ w_in
    c1, c2, c3 = CONV_CH, 2 * CONV_CH, 3 * CONV_CH
    b_gate, c_gate, v_conv, q, k, v = jnp.split(
        z, [c1, c2, c3, c3 + ATTN_WIDTH, c3 + 2 * ATTN_WIDTH], axis=-1)
    y_conv = b_gate * causal_dwconv(c_gate * v_conv, conv_w)
    q = rms_norm(q.reshape(B, S, N_HEADS, HEAD_DIM), q_norm_g).astype(jnp.float32) * (HEAD_DIM ** -0.5)
    k = rms_norm(k.reshape(B, S, N_HEADS, HEAD_DIM), k_norm_g).astype(jnp.float32)
    v = v.reshape(B, S, N_HEADS, HEAD_DIM).astype(jnp.float32)
    y_attn = dilated_mixture_attention(q, k, v).astype(h.dtype).reshape(B, S, ATTN_WIDTH)
    y = jnp.concatenate([rms_norm(y_conv, g_branch[:CONV_CH]),
                         rms_norm(y_attn, g_branch[CONV_CH:])], axis=-1)
    return y @ w_out


def hier_moe(h, w_group, w_expert, w1, w3, w2):
    B, S, D = h.shape
    T = B * S
    xt = h.reshape(T, D)
    g_logits = (xt @ w_group).astype(jnp.float32)
    g_prob = jax.nn.softmax(g_logits, axis=-1)
    g_sel = jnp.argmax(g_logits, axis=-1)
    g_w = jnp.take_along_axis(g_prob, g_sel[:, None], axis=-1)[:, 0]
    e_logits = (xt @ w_expert).astype(jnp.float32).reshape(T, N_GROUPS, EXPERTS_PER_GROUP)
    e_in = jnp.take_along_axis(e_logits, g_sel[:, None, None], axis=1)[:, 0]
    e_prob = jax.nn.softmax(e_in, axis=-1)
    top_w, top_i = lax.top_k(e_prob, TOP_K)
    top_w = top_w / top_w.sum(-1, keepdims=True) * g_w[:, None]
    expert_id = g_sel[:, None] * EXPERTS_PER_GROUP + top_i
    A = T * TOP_K
    flat_e = expert_id.reshape(A)
    flat_t = jnp.repeat(jnp.arange(T, dtype=jnp.int32), TOP_K)
    flat_w = top_w.reshape(A)
    order = jnp.argsort(flat_e)
    se, st, sw = flat_e[order], flat_t[order], flat_w[order]
    counts = jnp.bincount(flat_e, length=N_EXPERTS)
    starts = jnp.cumsum(counts) - counts
    pcounts = (counts + MOE_BLOCK - 1) // MOE_BLOCK * MOE_BLOCK
    pends = jnp.cumsum(pcounts)
    pstarts = pends - pcounts
    dest = pstarts[se] + jnp.arange(A) - starts[se]
    R = A + N_EXPERTS * MOE_BLOCK
    n_blk = R // MOE_BLOCK
    row_tok = jnp.full((R,), T, jnp.int32).at[dest].set(st)
    row_w = jnp.zeros((R,), jnp.float32).at[dest].set(sw)
    blk_e = jnp.minimum(jnp.searchsorted(pends, jnp.arange(n_blk) * MOE_BLOCK, side='right'),
                        N_EXPERTS - 1)
    x_pad = jnp.concatenate([xt, jnp.zeros((1, D), xt.dtype)], axis=0)
    xs = x_pad[row_tok].reshape(n_blk, MOE_BLOCK, D)

    def expert_block(args):
        xb, e = args
        return (jax.nn.silu(xb @ w1[e]) * (xb @ w3[e])) @ w2[e]

    ys = lax.map(expert_block, (xs, blk_e)).reshape(R, D)
    out = jnp.zeros((T + 1, D), jnp.float32).at[row_tok].add(
        ys.astype(jnp.float32) * row_w[:, None])[:T]
    return out.astype(h.dtype).reshape(B, S, D)


def hybrid_layer(x, c, w_ada, b_ada, g_mix, g_ffn, w_in, conv_w, q_norm_g, k_norm_g,
                 g_branch, w_out, w_group, w_expert, w1, w3, w2):
    mod = (jax.nn.silu(c) @ w_ada + b_ada)[:, None, :]
    sh1, sc1, gt1, sh2, sc2, gt2 = jnp.split(mod, 6, axis=-1)
    h = rms_norm(x, g_mix) * (1.0 + sc1) + sh1
    x = x + gt1 * token_mixer(h, w_in, conv_w, q_norm_g, k_norm_g, g_branch, w_out)
    h = rms_norm(x, g_ffn) * (1.0 + sc2) + sh2
    x = x + gt2 * hier_moe(h, w_group, w_expert, w1, w3, w2)
    return x


def setup_inputs(seed: int = 0) -> dict:
    key = jax.random.key(seed)
    ks = jax.random.split(key, 17)
    f32 = jnp.float32

    def nrm(k, shape, scale):
        return jax.random.normal(k, shape, f32) * scale

    L = DEPTH
    return {
        "x": nrm(ks[0], (BATCH, SEQ, D_MODEL), 1.0),
        "c": nrm(ks[1], (BATCH, D_MODEL), 1.0),
        "w_ada": nrm(ks[2], (L, D_MODEL, 6 * D_MODEL), 0.5 * D_MODEL ** -0.5),
        "b_ada": nrm(ks[3], (L, 6 * D_MODEL), 0.01),
        "g_mix": 1.0 + nrm(ks[4], (L, D_MODEL), 0.02),
        "g_ffn": 1.0 + nrm(ks[5], (L, D_MODEL), 0.02),
        "w_in": nrm(ks[6], (L, D_MODEL, IN_WIDTH), D_MODEL ** -0.5),
        "conv_w": nrm(ks[7], (L, CONV_WIDTH, CONV_CH), CONV_WIDTH ** -0.5),
        "q_norm_g": 1.0 + nrm(ks[8], (L, HEAD_DIM), 0.02),
        "k_norm_g": 1.0 + nrm(ks[9], (L, HEAD_DIM), 0.02),
        "g_branch": 1.0 + nrm(ks[10], (L, D_MIX), 0.02),
        "w_out": nrm(ks[11], (L, D_MIX, D_MODEL), D_MIX ** -0.5),
        "w_group": nrm(ks[12], (L, D_MODEL, N_GROUPS), D_MODEL ** -0.5),
        "w_expert": nrm(ks[13], (L, D_MODEL, N_EXPERTS), D_MODEL ** -0.5),
        "w1": nrm(ks[14], (L, N_EXPERTS, D_MODEL, D_EXPERT), D_MODEL ** -0.5),
        "w3": nrm(ks[15], (L, N_EXPERTS, D_MODEL, D_EXPERT), D_MODEL ** -0.5),
        "w2": nrm(ks[16], (L, N_EXPERTS, D_EXPERT, D_MODEL), D_EXPERT ** -0.5),
    }


def reference(x, c, w_ada, b_ada, g_mix, g_ffn, w_in, conv_w, q_norm_g, k_norm_g,
              g_branch, w_out, w_group, w_expert, w1, w3, w2):
    for l in range(DEPTH):
        x = hybrid_layer(x, c, w_ada[l], b_ada[l], g_mix[l], g_ffn[l], w_in[l], conv_w[l],
                         q_norm_g[l], k_norm_g[l], g_branch[l], w_out[l], w_group[l],
                         w_expert[l], w1[l], w3[l], w2[l])
    return x
```

```python
import functools

import jax
import jax.numpy as jnp
from jax import lax
from jax.experimental import pallas as pl
from jax.experimental.pallas import tpu as pltpu

EPS = 1e-6
HEAD_DIM = 128
BAND = 128
PATTERNS = ((128, 1), (512, 4), (2048, 16))
ALIBI_MAX_EXP = 8.0
N_GROUPS = 8
EXPERTS_PER_GROUP = 8
N_EXPERTS = N_GROUPS * EXPERTS_PER_GROUP
LANES = 128
VMEM_LIMIT = 56 * 1024 * 1024
F32 = jnp.float32
BF16 = jnp.bfloat16
NEG_INF = float("-inf")


def _cparams(sem):
    return pltpu.CompilerParams(dimension_semantics=sem, vmem_limit_bytes=VMEM_LIMIT)


def _ada_kernel(c_ref, w_ref, b_ref, o_ref, s_ref, *, rows):
    @pl.when(pl.program_id(0) == 0)
    def _():
        c = c_ref[...]
        s_ref[...] = c * jax.nn.sigmoid(c)

    d, tn = w_ref.shape

    def body(i, acc):
        r0 = pl.multiple_of(i * rows, rows)
        p = w_ref[pl.ds(r0, rows), :] * s_ref[pl.ds(r0, rows), :]
        return acc + jnp.sum(p.reshape(rows // 8, 8, tn), axis=0)

    acc = lax.fori_loop(0, d // rows, body, jnp.zeros((8, tn), F32))
    o_ref[...] = jnp.sum(acc, axis=0, keepdims=True) + b_ref[...]


def _ada(c, w_ada, b_ada):
    d, n = w_ada.shape
    tn = min(512, n)
    rows = min(512, d)
    return pl.pallas_call(
        functools.partial(_ada_kernel, rows=rows),
        out_shape=jax.ShapeDtypeStruct((1, n), F32),
        grid=(n // tn,),
        in_specs=[pl.BlockSpec((d, 1), lambda j: (0, 0)),
                  pl.BlockSpec((d, tn), lambda j: (0, j)),
                  pl.BlockSpec((1, tn), lambda j: (0, j))],
        out_specs=pl.BlockSpec((1, tn), lambda j: (0, j)),
        scratch_shapes=[pltpu.VMEM((d, 1), F32)],
        compiler_params=_cparams(("arbitrary",)),
        name="ada",
    )(c.reshape(d, 1), w_ada, b_ada.reshape(1, n))


def _rms_mod(x, g, sc, sh):
    ms = jnp.mean(x * x, axis=-1, keepdims=True)
    return (x * lax.rsqrt(ms + EPS) * g) * (1.0 + sc) + sh


def _inproj_kernel(x_ref, g_ref, sc_ref, sh_ref, w_ref, qg_ref, kg_ref, z_ref, h_ref, *, q0, k0, v0):
    j = pl.program_id(1)

    @pl.when(j == 0)
    def _():
        h_ref[...] = _rms_mod(x_ref[...], g_ref[...], sc_ref[...], sh_ref[...]).astype(BF16)

    acc = jnp.dot(h_ref[...], w_ref[...], preferred_element_type=F32)
    is_q = (j >= q0) & (j < k0)
    is_k = (j >= k0) & (j < v0)

    @pl.when(is_q | is_k)
    def _():
        g = jnp.where(is_q, qg_ref[...] * (HEAD_DIM ** -0.5), kg_ref[...])
        for hh in range(acc.shape[1] // HEAD_DIM):
            a = acc[:, hh * HEAD_DIM:(hh + 1) * HEAD_DIM]
            ms = jnp.mean(a * a, axis=-1, keepdims=True)
            z_ref[:, hh * HEAD_DIM:(hh + 1) * HEAD_DIM] = a * lax.rsqrt(ms + EPS) * g

    @pl.when(jnp.logical_not(is_q | is_k))
    def _():
        z_ref[...] = acc


def _inproj(x2d, g_mix, sc1, sh1, w_in_bf, q_g, k_g, conv_ch, attn_w):
    s, d = x2d.shape
    n = w_in_bf.shape[1]
    tm = min(512, s)
    tn = min(1024, conv_ch)
    assert conv_ch % tn == 0 and attn_w % tn == 0 and tn % HEAD_DIM == 0
    q0 = 3 * conv_ch // tn
    k0 = q0 + attn_w // tn
    v0 = k0 + attn_w // tn
    vec = lambda: pl.BlockSpec((1, d), lambda i, j: (0, 0))
    hvec = lambda: pl.BlockSpec((1, HEAD_DIM), lambda i, j: (0, 0))
    return pl.pallas_call(
        functools.partial(_inproj_kernel, q0=q0, k0=k0, v0=v0),
        out_shape=jax.ShapeDtypeStruct((s, n), F32),
        grid=(s // tm, n // tn),
        in_specs=[pl.BlockSpec((tm, d), lambda i, j: (i, 0)), vec(), vec(), vec(),
                  pl.BlockSpec((d, tn), lambda i, j: (0, j)), hvec(), hvec()],
        out_specs=pl.BlockSpec((tm, tn), lambda i, j: (i, j)),
        scratch_shapes=[pltpu.VMEM((tm, d), BF16)],
        compiler_params=_cparams(("arbitrary", "arbitrary")),
        name="inproj",
    )(x2d, g_mix, sc1, sh1, w_in_bf, q_g, k_g)


def _conv_kernel(b_ref, c_ref, v_ref, cw_ref, g_ref, o_ref, ue_ref):
    i = pl.program_id(0)
    tm = b_ref.shape[0]
    u = c_ref[...] * v_ref[...]

    @pl.when(i == 0)
    def _():
        ue_ref[0:8, :] = jnp.zeros((8, ue_ref.shape[1]), F32)

    @pl.when(i > 0)
    def _():
        ue_ref[0:8, :] = ue_ref[tm:tm + 8, :]

    ue_ref[8:8 + tm, :] = u
    u1 = ue_ref[7:7 + tm, :]
    u2 = ue_ref[6:6 + tm, :]
    w = cw_ref[...]
    y = b_ref[...] * (w[0:1, :] * u2 + w[1:2, :] * u1 + w[2:3, :] * u)
    ms = jnp.mean(y * y, axis=-1, keepdims=True)
    o_ref[...] = (y * lax.rsqrt(ms + EPS) * g_ref[...]).astype(BF16)


def _conv(z, conv_w, g_conv, conv_ch):
    s = z.shape[0]
    tm = min(512, s)
    blk = lambda jj: pl.BlockSpec((tm, conv_ch), lambda i: (i, jj))
    return pl.pallas_call(
        _conv_kernel,
        out_shape=jax.ShapeDtypeStruct((s, conv_ch), BF16),
        grid=(s // tm,),
        in_specs=[blk(0), blk(1), blk(2),
                  pl.BlockSpec(conv_w.shape, lambda i: (0, 0)),
                  pl.BlockSpec((1, conv_ch), lambda i: (0, 0))],
        out_specs=pl.BlockSpec((tm, conv_ch), lambda i: (i, 0)),
        scratch_shapes=[pltpu.VMEM((tm + 8, conv_ch), F32)],
        compiler_params=_cparams(("arbitrary",)),
        name="conv",
    )(z, z, z, conv_w, g_conv)


def _attn_kernel(slopes_ref, q_ref, kp_ref, kc_ref, vp_ref, vc_ref, o_ref,
                 kk_ref, vv_ref, m_ref, l_ref, acc_ref):
    h = pl.program_id(0)
    cidx = pl.program_id(1)
    sc = q_ref.shape[0]
    kk_ref[0:sc, :] = kp_ref[...]
    kk_ref[sc:2 * sc, :] = kc_ref[...]
    vv_ref[0:sc, :] = vp_ref[...]
    vv_ref[sc:2 * sc, :] = vc_ref[...]
    slope = slopes_ref[h]
    first_pen = jnp.where(cidx > 0, 0.0, NEG_INF).astype(F32)

    qi = lax.broadcasted_iota(jnp.int32, (BAND, BAND), 0)
    ku = lax.broadcasted_iota(jnp.int32, (BAND, BAND), 1)
    steps_cur = qi - ku
    steps_prev = BAND + qi - ku
    dn_t = (((1,), (1,)), ((), ()))

    for pi, (window, dil) in enumerate(PATTERNS):
        assert window // dil == BAND and sc % (BAND * dil) == 0
        bias_cur = jnp.where(steps_cur >= 0, -slope * (steps_cur * dil).astype(F32), NEG_INF)
        bias_prev = jnp.where(steps_prev <= BAND, -slope * (steps_prev * dil).astype(F32), NEG_INF)
        span = BAND * dil
        for n_loc in range(sc // span):
            for r in range(dil):
                qs = n_loc * span + r
                rows = pl.ds(qs, BAND, stride=dil) if dil > 1 else pl.ds(qs, BAND)
                cur = pl.ds(sc + qs, BAND, stride=dil) if dil > 1 else pl.ds(sc + qs, BAND)
                prv = pl.ds(sc + qs - span, BAND, stride=dil) if dil > 1 else pl.ds(sc + qs - span, BAND)
                q = q_ref[rows, :].astype(BF16)
                s_cur = lax.dot_general(q, kk_ref[cur, :].astype(BF16), dn_t,
                                        preferred_element_type=F32) + bias_cur
                s_prev = lax.dot_general(q, kk_ref[prv, :].astype(BF16), dn_t,
                                         preferred_element_type=F32) + bias_prev
                if n_loc == 0:
                    s_prev = s_prev + first_pen
                m = jnp.max(jnp.maximum(s_cur, s_prev), axis=-1, keepdims=True)
                p_cur = jnp.exp(s_cur - m)
                p_prev = jnp.exp(s_prev - m)
                l = jnp.sum(p_cur + p_prev, axis=-1, keepdims=True)
                o = (jnp.dot(p_cur.astype(BF16), vv_ref[cur, :].astype(BF16), preferred_element_type=F32)
                     + jnp.dot(p_prev.astype(BF16), vv_ref[prv, :].astype(BF16), preferred_element_type=F32))
                m_b = jnp.broadcast_to(m, (BAND, HEAD_DIM))
                l_b = jnp.broadcast_to(l, (BAND, HEAD_DIM))
                if pi == 0:
                    m_ref[rows, :] = m_b
                    l_ref[rows, :] = l_b
                    acc_ref[rows, :] = o
                else:
                    m_old = m_ref[rows, :]
                    m_new = jnp.maximum(m_old, m_b)
                    a_old = jnp.exp(m_old - m_new)
                    a_new = jnp.exp(m_b - m_new)
                    l_new = a_old * l_ref[rows, :] + a_new * l_b
                    o_new = a_old * acc_ref[rows, :] + a_new * o
                    if pi == len(PATTERNS) - 1:
                        acc_ref[rows, :] = o_new / l_new
                    else:
                        m_ref[rows, :] = m_new
                        l_ref[rows, :] = l_new
                        acc_ref[rows, :] = o_new
    o_ref[...] = acc_ref[...].astype(BF16)


def _attention(z, slopes, conv_ch, attn_w):
    s = z.shape[0]
    n_heads = attn_w // HEAD_DIM
    sc = BAND * PATTERNS[-1][1]
    assert s % sc == 0
    qb = 3 * conv_ch // HEAD_DIM
    kb = qb + n_heads
    vb = kb + n_heads
    cur = lambda base: pl.BlockSpec((sc, HEAD_DIM), lambda h, c, sl: (c, base + h))
    prev = lambda base: pl.BlockSpec((sc, HEAD_DIM), lambda h, c, sl: (jnp.maximum(c - 1, 0), base + h))
    return pl.pallas_call(
        _attn_kernel,
        out_shape=jax.ShapeDtypeStruct((s, attn_w), BF16),
        grid_spec=pltpu.PrefetchScalarGridSpec(
            num_scalar_prefetch=1,
            grid=(n_heads, s // sc),
            in_specs=[cur(qb), prev(kb), cur(kb), prev(vb), cur(vb)],
            out_specs=pl.BlockSpec((sc, HEAD_DIM), lambda h, c, sl: (c, h)),
            scratch_shapes=[pltpu.VMEM((2 * sc, HEAD_DIM), F32), pltpu.VMEM((2 * sc, HEAD_DIM), F32),
                            pltpu.VMEM((sc, HEAD_DIM), F32), pltpu.VMEM((sc, HEAD_DIM), F32),
                            pltpu.VMEM((sc, HEAD_DIM), F32)]),
        compiler_params=_cparams(("arbitrary", "arbitrary")),
        name="attn",
    )(slopes, z, z, z, z, z)


def _outproj_kernel(yc_ref, ya_ref, ga_ref, w_ref, x_ref, gt_ref, o_ref, y_ref):
    @pl.when(pl.program_id(1) == 0)
    def _():
        cc = yc_ref.shape[1]
        ya = ya_ref[...].astype(F32)
        ms = jnp.mean(ya * ya, axis=-1, keepdims=True)
        y_ref[:, 0:cc] = yc_ref[...]
        y_ref[:, cc:] = (ya * lax.rsqrt(ms + EPS) * ga_ref[...]).astype(BF16)

    acc = jnp.dot(y_ref[...], w_ref[...], preferred_element_type=F32)
    o_ref[...] = x_ref[...] + gt_ref[...] * acc


def _outproj(y_conv, y_attn, g_attn, w_out_bf, x2d, gt1):
    s, d = x2d.shape
    cc, aw = y_conv.shape[1], y_attn.shape[1]
    tm = min(512, s)
    tn = min(1024, d)
    return pl.pallas_call(
        _outproj_kernel,
        out_shape=jax.ShapeDtypeStruct((s, d), F32),
        grid=(s // tm, d // tn),
        in_specs=[pl.BlockSpec((tm, cc), lambda i, j: (i, 0)),
                  pl.BlockSpec((tm, aw), lambda i, j: (i, 0)),
                  pl.BlockSpec((1, aw), lambda i, j: (0, 0)),
                  pl.BlockSpec((cc + aw, tn), lambda i, j: (0, j)),
                  pl.BlockSpec((tm, tn), lambda i, j: (i, j)),
                  pl.BlockSpec((1, tn), lambda i, j: (0, j))],
        out_specs=pl.BlockSpec((tm, tn), lambda i, j: (i, j)),
        scratch_shapes=[pltpu.VMEM((tm, cc + aw), BF16)],
        compiler_params=_cparams(("arbitrary", "arbitrary")),
        name="outproj",
    )(y_conv, y_attn, g_attn, w_out_bf, x2d, gt1)


def _router_kernel(x_ref, g_ref, sc_ref, sh_ref, wr_ref, h_ref, eid_ref, wt_ref):
    h = _rms_mod(x_ref[...], g_ref[...], sc_ref[...], sh_ref[...])
    h_ref[...] = h
    logits = jnp.dot(h, wr_ref[...], preferred_element_type=F32, precision=lax.Precision.HIGHEST)
    tm = logits.shape[0]
    lane = lax.broadcasted_iota(jnp.int32, (tm, LANES), 1).astype(F32)
    big = float(LANES)
    is_g = lane < N_GROUPS
    gl = jnp.where(is_g, logits, NEG_INF)
    gmax = jnp.max(gl, axis=-1, keepdims=True)
    gsel = jnp.min(jnp.where(gl == gmax, lane, big), axis=-1, keepdims=True)
    g_w = 1.0 / jnp.sum(jnp.where(is_g, jnp.exp(gl - gmax), 0.0), axis=-1, keepdims=True)
    lo = N_GROUPS + gsel * EXPERTS_PER_GROUP
    in_grp = (lane >= lo) & (lane < lo + EXPERTS_PER_GROUP)
    el = jnp.where(in_grp, logits, NEG_INF)
    e1 = jnp.max(el, axis=-1, keepdims=True)
    i1 = jnp.min(jnp.where(el == e1, lane, big), axis=-1, keepdims=True)
    el2 = jnp.where(lane == i1, NEG_INF, el)
    e2 = jnp.max(el2, axis=-1, keepdims=True)
    i2 = jnp.min(jnp.where(el2 == e2, lane, big), axis=-1, keepdims=True)
    t = jnp.exp(e2 - e1)
    w1 = g_w / (1.0 + t)
    w2 = g_w * t / (1.0 + t)
    eid = jnp.where(lane == 0.0, i1 - N_GROUPS, jnp.where(lane == 1.0, i2 - N_GROUPS, 0.0))
    eid_ref[...] = eid.astype(jnp.int32)
    wt_ref[...] = jnp.where(lane == 0.0, w1, jnp.where(lane == 1.0, w2, 0.0))


def _router(x2, g_ffn, sc2, sh2, w_r):
    s, d = x2.shape
    tm = min(512, s)
    vec = lambda: pl.BlockSpec((1, d), lambda i: (0, 0))
    lanes_out = lambda: pl.BlockSpec((tm, LANES), lambda i: (i, 0))
    return pl.pallas_call(
        _router_kernel,
        out_shape=(jax.ShapeDtypeStruct((s, d), F32),
                   jax.ShapeDtypeStruct((s, LANES), jnp.int32),
                   jax.ShapeDtypeStruct((s, LANES), F32)),
        grid=(s // tm,),
        in_specs=[pl.BlockSpec((tm, d), lambda i: (i, 0)), vec(), vec(), vec(),
                  pl.BlockSpec((d, LANES), lambda i: (0, 0))],
        out_specs=(pl.BlockSpec((tm, d), lambda i: (i, 0)), lanes_out(), lanes_out()),
        compiler_params=_cparams(("arbitrary",)),
        name="router",
    )(x2, g_ffn, sc2, sh2, w_r)


def _rank_kernel(eid_ref, rank_ref, cnt_ref, run_ref):
    @pl.when(pl.program_id(0) == 0)
    def _():
        run_ref[...] = jnp.zeros_like(run_ref)

    tb = eid_ref.shape[0]
    lane = lax.broadcasted_iota(jnp.int32, (tb, LANES), 1)
    row = lax.broadcasted_iota(jnp.int32, (tb, tb), 0)
    col = lax.broadcasted_iota(jnp.int32, (tb, tb), 1)
    earlier = (col < row).astype(BF16)
    e = eid_ref[...]
    run = run_ref[...]
    ranks = []
    for k in range(2):
        onehot = e[:, k:k + 1] == lane
        oh = onehot.astype(F32)
        before = jnp.dot(earlier, oh.astype(BF16), preferred_element_type=F32) + run
        ranks.append(jnp.sum(jnp.where(onehot, before, 0.0), axis=-1, keepdims=True))
        run = run + jnp.sum(oh, axis=0, keepdims=True)
    rank_ref[...] = jnp.where(lane == 0, ranks[0], jnp.where(lane == 1, ranks[1], 0.0)).astype(jnp.int32)
    run_ref[...] = run
    cnt_ref[...] = run


def _rank(eid):
    s = eid.shape[0]
    tb = min(512, s)
    return pl.pallas_call(
        _rank_kernel,
        out_shape=(jax.ShapeDtypeStruct((s, LANES), jnp.int32), jax.ShapeDtypeStruct((1, LANES), F32)),
        grid=(s // tb,),
        in_specs=[pl.BlockSpec((tb, LANES), lambda i: (i, 0))],
        out_specs=(pl.BlockSpec((tb, LANES), lambda i: (i, 0)), pl.BlockSpec((1, LANES), lambda i: (0, 0))),
        scratch_shapes=[pltpu.VMEM((1, LANES), F32)],
        compiler_params=_cparams(("arbitrary",)),
        name="rank",
    )(eid)


def _plan_kernel(eid_ref, rank_ref, cnt_ref, dest_ref, src_ref, blke_ref, blkc_ref, nused_ref, pst_ref,
                 *, blk, n_blk):
    def per_expert(e, nb):
        c = cnt_ref[e]
        pst_ref[e] = nb * blk
        nbe = (c + (blk - 1)) // blk

        def per_block(j, carry):
            blke_ref[nb + j] = e
            blkc_ref[nb + j] = jnp.minimum(blk, c - j * blk)
            return carry

        lax.fori_loop(0, nbe, per_block, 0)
        return nb + nbe

    n_used = lax.fori_loop(0, N_EXPERTS, per_expert, jnp.int32(0))
    nused_ref[0] = n_used
    last_e = blke_ref[n_used - 1]

    def unused(i, carry):
        blke_ref[i] = last_e
        blkc_ref[i] = 0
        return carry

    lax.fori_loop(n_used, n_blk, unused, 0)

    def place(a, carry):
        dst = pst_ref[eid_ref[a]] + rank_ref[a]
        dest_ref[a] = dst
        src_ref[dst] = a
        return carry

    lax.fori_loop(0, eid_ref.shape[0], place, 0)


def _plan(eid_flat, rank_flat, counts, blk):
    a = eid_flat.shape[0]
    n_blk = a // blk + N_EXPERTS
    smem = lambda: pl.BlockSpec(memory_space=pltpu.SMEM)
    i32 = jnp.int32
    return pl.pallas_call(
        functools.partial(_plan_kernel, blk=blk, n_blk=n_blk),
        out_shape=(jax.ShapeDtypeStruct((a,), i32), jax.ShapeDtypeStruct((n_blk * blk,), i32),
                   jax.ShapeDtypeStruct((n_blk,), i32), jax.ShapeDtypeStruct((n_blk,), i32),
                   jax.ShapeDtypeStruct((1,), i32)),
        in_specs=[smem(), smem(), smem()],
        out_specs=(smem(), smem(), smem(), smem(), smem()),
        scratch_shapes=[pltpu.SMEM((N_EXPERTS,), i32)],
        name="plan",
    )(eid_flat, rank_flat, counts)


def _expert_kernel(blke_ref, blkc_ref, nused_ref, src_ref, h_hbm, w1_ref, w3_ref, w2_ref, o_ref, xbuf, sem,
                   *, blk):
    i = pl.program_id(0)
    c = pl.program_id(1)
    n_tok = h_hbm.shape[0]
    n_used = nused_ref[0]

    def row_copy(tok, slot, r):
        return pltpu.make_async_copy(h_hbm.at[pl.ds(tok, 1), :], xbuf.at[slot, pl.ds(r, 1), :], sem.at[slot])

    def issue(b, slot):
        def body(r, carry):
            row_copy(lax.rem(src_ref[b * blk + r], n_tok), slot, r).start()
            return carry
        lax.fori_loop(0, blkc_ref[b], body, 0)

    def wait(b, slot):
        def body(r, carry):
            row_copy(0, slot, 0).wait()
            return carry
        lax.fori_loop(0, blkc_ref[b], body, 0)

    slot = lax.rem(i, 2)

    @pl.when((i == 0) & (c == 0))
    def _():
        xbuf[...] = jnp.zeros_like(xbuf)
        issue(0, 0)

    @pl.when((c == 0) & (i < n_used))
    def _():
        wait(i, slot)

        @pl.when(i + 1 < n_used)
        def _():
            issue(i + 1, 1 - slot)

    @pl.when(i < n_used)
    def _():
        x = xbuf[slot].astype(BF16)
        a = jnp.dot(x, w1_ref[...].astype(BF16), preferred_element_type=F32)
        b = jnp.dot(x, w3_ref[...].astype(BF16), preferred_element_type=F32)
        hid = ((a * jax.nn.sigmoid(a)) * b).astype(BF16)
        y = jnp.dot(hid, w2_ref[...].astype(BF16), preferred_element_type=F32)

        @pl.when(c == 0)
        def _():
            o_ref[...] = y

        @pl.when(c > 0)
        def _():
            o_ref[...] += y


def _experts(h2, w1, w3, w2, blk_e, blk_cnt, n_used, row_src, blk):
    s, d = h2.shape
    de = w1.shape[-1]
    ck = min(256, de)
    nc = de // ck
    n_blk = blk_e.shape[0]

    def chunk(i, c, nu):
        return jnp.where(i < nu[0], c, nc - 1)

    return pl.pallas_call(
        functools.partial(_expert_kernel, blk=blk),
        out_shape=jax.ShapeDtypeStruct((n_blk * blk, d), F32),
        grid_spec=pltpu.PrefetchScalarGridSpec(
            num_scalar_prefetch=4,
            grid=(n_blk, nc),
            in_specs=[pl.BlockSpec(memory_space=pl.ANY),
                      pl.BlockSpec((None, d, ck), lambda i, c, be, bc, nu, rs: (be[i], 0, chunk(i, c, nu))),
                      pl.BlockSpec((None, d, ck), lambda i, c, be, bc, nu, rs: (be[i], 0, chunk(i, c, nu))),
                      pl.BlockSpec((None, ck, d), lambda i, c, be, bc, nu, rs: (be[i], chunk(i, c, nu), 0))],
            out_specs=pl.BlockSpec((blk, d), lambda i, c, be, bc, nu, rs: (jnp.minimum(i, nu[0] - 1), 0)),
            scratch_shapes=[pltpu.VMEM((2, blk, d), F32), pltpu.SemaphoreType.DMA((2,))]),
        compiler_params=_cparams(("arbitrary", "arbitrary")),
        name="experts",
    )(blk_e, blk_cnt, n_used, row_src, h2, w1, w3, w2)


def _combine_kernel(dest_ref, ys_hbm, x_ref, wt_ref, gt_ref, o_ref, ybuf, sem):
    i = pl.program_id(0)
    nb = pl.num_programs(0)
    tm = x_ref.shape[0]
    n_tok = tm * nb

    def row_copy(src_row, slot, k, r):
        return pltpu.make_async_copy(ys_hbm.at[pl.ds(src_row, 1), :], ybuf.at[slot, k, pl.ds(r, 1), :],
                                     sem.at[slot])

    def issue(b, slot):
        def body(r, carry):
            for k in range(2):
                row_copy(dest_ref[k * n_tok + b * tm + r], slot, k, r).start()
            return carry
        lax.fori_loop(0, tm, body, 0)

    def wait(slot):
        def body(r, carry):
            for k in range(2):
                row_copy(0, slot, k, 0).wait()
            return carry
        lax.fori_loop(0, tm, body, 0)

    slot = lax.rem(i, 2)

    @pl.when(i == 0)
    def _():
        issue(0, 0)

    @pl.when(i + 1 < nb)
    def _():
        issue(i + 1, 1 - slot)

    wait(slot)
    w = wt_ref[...]
    y = ybuf[slot, 0] * w[:, 0:1] + ybuf[slot, 1] * w[:, 1:2]
    o_ref[...] = x_ref[...] + gt_ref[...] * y


def _combine(dest, ys, x2, wts, gt2):
    s, d = x2.shape
    tm = min(256, s)
    return pl.pallas_call(
        _combine_kernel,
        out_shape=jax.ShapeDtypeStruct((s, d), F32),
        grid_spec=pltpu.PrefetchScalarGridSpec(
            num_scalar_prefetch=1,
            grid=(s // tm,),
            in_specs=[pl.BlockSpec(memory_space=pl.ANY),
                      pl.BlockSpec((tm, d), lambda i, ds_: (i, 0)),
                      pl.BlockSpec((tm, LANES), lambda i, ds_: (i, 0)),
                      pl.BlockSpec((1, d), lambda i, ds_: (0, 0))],
            out_specs=pl.BlockSpec((tm, d), lambda i, ds_: (i, 0)),
            scratch_shapes=[pltpu.VMEM((2, 2, tm, d), F32), pltpu.SemaphoreType.DMA((2,))]),
        compiler_params=_cparams(("arbitrary",)),
        name="combine",
    )(dest, ys, x2, wts, gt2)


MOE_BLOCK = 256


def _layer(x2d, c, w_ada, b_ada, g_mix, g_ffn, w_in, conv_w, q_norm_g, k_norm_g, g_branch, w_out,
           w_group, w_expert, w1, w3, w2):
    s, d = x2d.shape
    conv_ch = conv_w.shape[1]
    attn_w = d - conv_ch
    n_heads = attn_w // HEAD_DIM
    row = lambda v: v.reshape(1, -1)

    mod = _ada(c, w_ada, b_ada)
    sh1, sc1, gt1, sh2, sc2, gt2 = [mod[:, i * d:(i + 1) * d] for i in range(6)]

    z = _inproj(x2d, row(g_mix), sc1, sh1, w_in.astype(BF16), row(q_norm_g), row(k_norm_g), conv_ch, attn_w)
    y_conv = _conv(z, conv_w, row(g_branch[:conv_ch]), conv_ch)
    slopes = jnp.exp2(-ALIBI_MAX_EXP * jnp.arange(1, n_heads + 1, dtype=F32) / n_heads)
    y_attn = _attention(z, slopes, conv_ch, attn_w)
    x2 = _outproj(y_conv, y_attn, row(g_branch[conv_ch:]), w_out.astype(BF16), x2d, gt1)

    w_r = jnp.concatenate([w_group, w_expert,
                           jnp.zeros((d, LANES - N_GROUPS - N_EXPERTS), F32)], axis=1)
    h2, eid, wts = _router(x2, row(g_ffn), sc2, sh2, w_r)
    rank, counts = _rank(eid)
    eid_flat = eid[:, :2].T.reshape(-1)
    rank_flat = rank[:, :2].T.reshape(-1)
    cnt = counts[0, :N_EXPERTS].astype(jnp.int32)
    dest, row_src, blk_e, blk_cnt, n_used = _plan(eid_flat, rank_flat, cnt, MOE_BLOCK)
    ys = _experts(h2, w1, w3, w2, blk_e, blk_cnt, n_used, row_src, MOE_BLOCK)
    return _combine(dest, ys, x2, wts, gt2)


def kernel(x, c, w_ada, b_ada, g_mix, g_ffn, w_in, conv_w, q_norm_g, k_norm_g, g_branch, w_out,
           w_group, w_expert, w1, w3, w2):
    b, s, d = x.shape
    assert b == 1 and c.shape[0] == 1
    for l in range(w_ada.shape[0]):
        x = _layer(x.reshape(s, d), c[0], w_ada[l], b_ada[l], g_mix[l], g_ffn[l], w_in[l], conv_w[l],
                   q_norm_g[l], k_norm_g[l], g_branch[l], w_out[l], w_group[l], w_expert[l],
                   w1[l], w3[l], w2[l]).reshape(b, s, d)
    return x
```

```python
import functools

import jax
import jax.numpy as jnp
from jax import lax
from jax.experimental import pallas as pl
from jax.experimental.pallas import tpu as pltpu

EPS = 1e-6
HEAD_DIM = 128
BAND = 128
PATTERNS = ((128, 1), (512, 4), (2048, 16))
RES = 16
ALIBI_MAX_EXP = 8.0
N_GROUPS = 8
EXPERTS_PER_GROUP = 8
N_EXPERTS = N_GROUPS * EXPERTS_PER_GROUP
MOE_BLOCK = 256
LANES = 128
SUBLANES = 8
VMEM_LIMIT = 56 * 1024 * 1024
F32 = jnp.float32
BF16 = jnp.bfloat16
NEG_INF = float("-inf")


def _cparams(sem):
    return pltpu.CompilerParams(dimension_semantics=sem, vmem_limit_bytes=VMEM_LIMIT)


def _ada_kernel(c_ref, w_ref, b_ref, o_ref, s_ref, *, rows):
    @pl.when(pl.program_id(0) == 0)
    def _():
        c = c_ref[...]
        s_ref[...] = c * jax.nn.sigmoid(c)

    d, tn = w_ref.shape

    def body(i, acc):
        r0 = pl.multiple_of(i * rows, rows)
        p = w_ref[pl.ds(r0, rows), :] * s_ref[pl.ds(r0, rows), :]
        return acc + jnp.sum(p.reshape(rows // SUBLANES, SUBLANES, tn), axis=0)

    acc = lax.fori_loop(0, d // rows, body, jnp.zeros((SUBLANES, tn), F32))
    o_ref[...] = jnp.sum(acc, axis=0, keepdims=True) + b_ref[...]


def _ada(c, w_ada, b_ada):
    d, n = w_ada.shape
    tn = min(512, n)
    rows = min(512, d)
    return pl.pallas_call(
        functools.partial(_ada_kernel, rows=rows),
        out_shape=jax.ShapeDtypeStruct((1, n), F32),
        grid=(n // tn,),
        in_specs=[pl.BlockSpec((d, 1), lambda j: (0, 0)),
                  pl.BlockSpec((d, tn), lambda j: (0, j)),
                  pl.BlockSpec((1, tn), lambda j: (0, j))],
        out_specs=pl.BlockSpec((1, tn), lambda j: (0, j)),
        scratch_shapes=[pltpu.VMEM((d, 1), F32)],
        compiler_params=_cparams(("arbitrary",)),
        name="ada",
    )(c.reshape(d, 1), w_ada, b_ada.reshape(1, n))


def _rms_mod(x, g, sc, sh):
    ms = jnp.mean(x * x, axis=-1, keepdims=True)
    return (x * lax.rsqrt(ms + EPS) * g) * (1.0 + sc) + sh


def _inproj_kernel(x_ref, g_ref, sc_ref, sh_ref, w_ref, qg_ref, kg_ref, z_ref, h_ref, *, q0, k0, v0):
    j = pl.program_id(2)

    @pl.when(j == 0)
    def _():
        h_ref[...] = _rms_mod(x_ref[...], g_ref[...], sc_ref[...], sh_ref[...]).astype(BF16)

    acc = jnp.dot(h_ref[...], w_ref[...], preferred_element_type=F32)
    is_q = (j >= q0) & (j < k0)
    is_k = (j >= k0) & (j < v0)

    @pl.when(is_q | is_k)
    def _():
        g = jnp.where(is_q, qg_ref[...] * (HEAD_DIM ** -0.5), kg_ref[...])
        for hh in range(acc.shape[1] // HEAD_DIM):
            a = acc[:, hh * HEAD_DIM:(hh + 1) * HEAD_DIM]
            ms = jnp.mean(a * a, axis=-1, keepdims=True)
            z_ref[:, hh * HEAD_DIM:(hh + 1) * HEAD_DIM] = a * lax.rsqrt(ms + EPS) * g

    @pl.when(jnp.logical_not(is_q | is_k))
    def _():
        z_ref[...] = acc


def _inproj(x_view, g_mix, sc1, sh1, w_in_bf, q_g, k_g, conv_ch, attn_w):
    na = x_view.shape[0]
    d, n = w_in_bf.shape
    tm = min(512, na)
    nab = na // tm
    tn = min(1024, conv_ch)
    assert conv_ch % tn == 0 and attn_w % tn == 0 and tn % HEAD_DIM == 0
    q0 = 3 * conv_ch // tn
    k0 = q0 + attn_w // tn
    v0 = k0 + attn_w // tn
    vec = lambda: pl.BlockSpec((1, d), lambda b, ia, j: (0, 0))
    hvec = lambda: pl.BlockSpec((1, HEAD_DIM), lambda b, ia, j: (0, 0))
    return pl.pallas_call(
        functools.partial(_inproj_kernel, q0=q0, k0=k0, v0=v0),
        out_shape=jax.ShapeDtypeStruct((RES * na, n), F32),
        grid=(RES, nab, n // tn),
        in_specs=[pl.BlockSpec((tm, d), lambda b, ia, j: (ia, b)), vec(), vec(), vec(),
                  pl.BlockSpec((d, tn), lambda b, ia, j: (0, j)), hvec(), hvec()],
        out_specs=pl.BlockSpec((tm, tn), lambda b, ia, j: (b * nab + ia, j)),
        scratch_shapes=[pltpu.VMEM((tm, d), BF16)],
        compiler_params=_cparams(("arbitrary", "arbitrary", "arbitrary")),
        name="inproj",
    )(x_view, g_mix, sc1, sh1, w_in_bf, q_g, k_g)


def _conv_kernel(b_ref, c_ref, v_ref, cw_ref, g_ref, o_ref, ue_ref):
    i = pl.program_id(0)
    ta = b_ref.shape[1]
    w = cw_ref[...]
    assert w.shape[0] == 3
    for k in range(2):
        @pl.when(i == 0)
        def _():
            ue_ref[k, 0:SUBLANES, :] = jnp.zeros((SUBLANES, ue_ref.shape[2]), F32)

        @pl.when(i > 0)
        def _():
            ue_ref[k, 0:SUBLANES, :] = ue_ref[k, ta:ta + SUBLANES, :]

        r = RES - 2 + k
        ue_ref[k, SUBLANES:SUBLANES + ta, :] = c_ref[r] * v_ref[r]

    def u_at(bb):
        if bb >= 0:
            return c_ref[bb] * v_ref[bb]
        return ue_ref[2 + bb, SUBLANES - 1:SUBLANES - 1 + ta, :]

    for b in range(RES):
        y = b_ref[b] * (w[0:1, :] * u_at(b - 2) + w[1:2, :] * u_at(b - 1) + w[2:3, :] * u_at(b))
        ms = jnp.mean(y * y, axis=-1, keepdims=True)
        o_ref[b] = (y * lax.rsqrt(ms + EPS) * g_ref[...]).astype(BF16)


def _conv(z3, conv_w, g_conv, conv_ch):
    na = z3.shape[1]
    ta = min(64, na)
    blk = lambda jj: pl.BlockSpec((RES, ta, conv_ch), lambda i: (0, i, jj))
    return pl.pallas_call(
        _conv_kernel,
        out_shape=jax.ShapeDtypeStruct((RES, na, conv_ch), BF16),
        grid=(na // ta,),
        in_specs=[blk(0), blk(1), blk(2),
                  pl.BlockSpec(conv_w.shape, lambda i: (0, 0)),
                  pl.BlockSpec((1, conv_ch), lambda i: (0, 0))],
        out_specs=pl.BlockSpec((RES, ta, conv_ch), lambda i: (0, i, 0)),
        scratch_shapes=[pltpu.VMEM((2, ta + SUBLANES, conv_ch), F32)],
        compiler_params=_cparams(("arbitrary",)),
        name="conv",
    )(z3, z3, z3, conv_w, g_conv)


def _attn_kernel(slopes_ref, q_ref, kp_ref, kc_ref, vp_ref, vc_ref, o_ref, bias_ref, m_ref, l_ref, acc_ref):
    h = pl.program_id(0)
    cidx = pl.program_id(1)
    slope = slopes_ref[h]
    first_pen = jnp.where(cidx > 0, 0.0, NEG_INF).astype(F32)
    dn_t = (((1,), (1,)), ((), ()))

    j = lax.broadcasted_iota(jnp.int32, (BAND, 2 * BAND), 0)
    u = lax.broadcasted_iota(jnp.int32, (BAND, 2 * BAND), 1)
    is_prev = u < BAND
    uu = jnp.where(is_prev, u, u - BAND)
    for pi, (window, dil) in enumerate(PATTERNS):
        assert window // dil == BAND and RES % dil == 0
        ns, sr = RES // dil, BAND * dil // RES
        assert sr & (sr - 1) == 0
        sh = sr.bit_length() - 1
        mq = (j & (sr - 1)) * ns + (j >> sh)
        mk = (uu & (sr - 1)) * ns + (uu >> sh)
        steps = mq - mk + jnp.where(is_prev, BAND, 0)
        valid = (is_prev & (steps <= BAND)) | (jnp.logical_not(is_prev) & (steps >= 0))
        bias = jnp.where(valid, -slope * (steps * dil).astype(F32), NEG_INF)
        bias_ref[pi, 0] = bias
        bias_ref[pi, 1] = bias + jnp.where(is_prev, first_pen, 0.0)

    for pi, (window, dil) in enumerate(PATTERNS):
        ns, sr = RES // dil, BAND * dil // RES
        n_chunks = BAND // sr
        for n in range(n_chunks):
            for r in range(dil):
                def gather(ref, nn):
                    return jnp.concatenate([ref[dil * c + r, nn * sr:(nn + 1) * sr, :] for c in range(ns)], axis=0)

                def scatter(ref, val):
                    for c in range(ns):
                        ref[dil * c + r, n * sr:(n + 1) * sr, :] = val[c * sr:(c + 1) * sr, :]

                q = gather(q_ref, n).astype(BF16)
                if n > 0:
                    k_prev, v_prev = gather(kc_ref, n - 1), gather(vc_ref, n - 1)
                else:
                    k_prev, v_prev = gather(kp_ref, n_chunks - 1), gather(vp_ref, n_chunks - 1)
                kcat = jnp.concatenate([k_prev, gather(kc_ref, n)], axis=0).astype(BF16)
                vcat = jnp.concatenate([v_prev, gather(vc_ref, n)], axis=0).astype(BF16)
                s = lax.dot_general(q, kcat, dn_t, preferred_element_type=F32) + bias_ref[pi, 1 if n == 0 else 0]
                m = jnp.max(s, axis=-1, keepdims=True)
                p = jnp.exp(s - m)
                l = jnp.sum(p, axis=-1, keepdims=True)
                o = jnp.dot(p.astype(BF16), vcat, preferred_element_type=F32)
                m_b = jnp.broadcast_to(m, (BAND, HEAD_DIM))
                l_b = jnp.broadcast_to(l, (BAND, HEAD_DIM))
                if pi == 0:
                    scatter(m_ref, m_b)
                    scatter(l_ref, l_b)
                    scatter(acc_ref, o)
                else:
                    m_old = gather(m_ref, n)
                    m_new = jnp.maximum(m_old, m_b)
                    a_old = jnp.exp(m_old - m_new)
                    a_new = jnp.exp(m_b - m_new)
                    l_new = a_old * gather(l_ref, n) + a_new * l_b
                    o_new = a_old * gather(acc_ref, n) + a_new * o
                    if pi == len(PATTERNS) - 1:
                        scatter(acc_ref, o_new / l_new)
                    else:
                        scatter(m_ref, m_new)
                        scatter(l_ref, l_new)
                        scatter(acc_ref, o_new)
    o_ref[...] = acc_ref[...].astype(BF16)


def _attention(z3, slopes, conv_ch, attn_w):
    na = z3.shape[1]
    n_heads = attn_w // HEAD_DIM
    assert na % BAND == 0
    qb = 3 * conv_ch // HEAD_DIM
    kb = qb + n_heads
    vb = kb + n_heads
    blk = (RES, BAND, HEAD_DIM)
    cur = lambda base: pl.BlockSpec(blk, lambda h, c, sl: (0, c, base + h))
    prev = lambda base: pl.BlockSpec(blk, lambda h, c, sl: (0, jnp.maximum(c - 1, 0), base + h))
    return pl.pallas_call(
        _attn_kernel,
        out_shape=jax.ShapeDtypeStruct((RES, na, attn_w), BF16),
        grid_spec=pltpu.PrefetchScalarGridSpec(
            num_scalar_prefetch=1,
            grid=(n_heads, na // BAND),
            in_specs=[cur(qb), prev(kb), cur(kb), prev(vb), cur(vb)],
            out_specs=pl.BlockSpec(blk, lambda h, c, sl: (0, c, h)),
            scratch_shapes=[pltpu.VMEM((len(PATTERNS), 2, BAND, 2 * BAND), F32),
                            pltpu.VMEM(blk, F32), pltpu.VMEM(blk, F32), pltpu.VMEM(blk, F32)]),
        compiler_params=_cparams(("arbitrary", "arbitrary")),
        name="attn",
    )(slopes, z3, z3, z3, z3, z3)


def _outproj_kernel(yc_ref, ya_ref, ga_ref, w_ref, x_ref, gt_ref, o_ref, y_ref):
    @pl.when(pl.program_id(2) == 0)
    def _():
        cc = yc_ref.shape[1]
        ya = ya_ref[...].astype(F32)
        ms = jnp.mean(ya * ya, axis=-1, keepdims=True)
        y_ref[:, 0:cc] = yc_ref[...]
        y_ref[:, cc:] = (ya * lax.rsqrt(ms + EPS) * ga_ref[...]).astype(BF16)

    acc = jnp.dot(y_ref[...], w_ref[...], preferred_element_type=F32)
    o_ref[...] = x_ref[...] + gt_ref[...] * acc


def _outproj(y_conv, y_attn, g_attn, w_out_bf, x_view, gt1):
    na = x_view.shape[0]
    d = w_out_bf.shape[1]
    cc, aw = y_conv.shape[1], y_attn.shape[1]
    tm = min(512, na)
    nab = na // tm
    tn = min(1024, d)
    ntn = d // tn
    row = lambda b, ia, j: b * nab + ia
    return pl.pallas_call(
        _outproj_kernel,
        out_shape=jax.ShapeDtypeStruct((RES * na, d), F32),
        grid=(RES, nab, ntn),
        in_specs=[pl.BlockSpec((tm, cc), lambda b, ia, j: (row(b, ia, j), 0)),
                  pl.BlockSpec((tm, aw), lambda b, ia, j: (row(b, ia, j), 0)),
                  pl.BlockSpec((1, aw), lambda b, ia, j: (0, 0)),
                  pl.BlockSpec((cc + aw, tn), lambda b, ia, j: (0, j)),
                  pl.BlockSpec((tm, tn), lambda b, ia, j: (ia, b * ntn + j)),
                  pl.BlockSpec((1, tn), lambda b, ia, j: (0, j))],
        out_specs=pl.BlockSpec((tm, tn), lambda b, ia, j: (row(b, ia, j), j)),
        scratch_shapes=[pltpu.VMEM((tm, cc + aw), BF16)],
        compiler_params=_cparams(("arbitrary", "arbitrary", "arbitrary")),
        name="outproj",
    )(y_conv, y_attn, g_attn, w_out_bf, x_view, gt1)


def _router_kernel(x_ref, g_ref, sc_ref, sh_ref, wr_ref, h_ref, eid_ref, wt_ref):
    h = _rms_mod(x_ref[...], g_ref[...], sc_ref[...], sh_ref[...])
    h_ref[...] = h
    logits = jnp.dot(h, wr_ref[...], preferred_element_type=F32, precision=lax.Precision.HIGHEST)
    tm = logits.shape[0]
    lane = lax.broadcasted_iota(jnp.int32, (tm, LANES), 1).astype(F32)
    big = float(LANES)
    is_g = lane < N_GROUPS
    gl = jnp.where(is_g, logits, NEG_INF)
    gmax = jnp.max(gl, axis=-1, keepdims=True)
    gsel = jnp.min(jnp.where(gl == gmax, lane, big), axis=-1, keepdims=True)
    g_w = 1.0 / jnp.sum(jnp.where(is_g, jnp.exp(gl - gmax), 0.0), axis=-1, keepdims=True)
    lo = N_GROUPS + gsel * EXPERTS_PER_GROUP
    in_grp = (lane >= lo) & (lane < lo + EXPERTS_PER_GROUP)
    el = jnp.where(in_grp, logits, NEG_INF)
    e1 = jnp.max(el, axis=-1, keepdims=True)
    i1 = jnp.min(jnp.where(el == e1, lane, big), axis=-1, keepdims=True)
    el2 = jnp.where(lane == i1, NEG_INF, el)
    e2 = jnp.max(el2, axis=-1, keepdims=True)
    i2 = jnp.min(jnp.where(el2 == e2, lane, big), axis=-1, keepdims=True)
    t = jnp.exp(e2 - e1)
    w1 = g_w / (1.0 + t)
    w2 = g_w * t / (1.0 + t)
    eid = jnp.where(lane == 0.0, i1 - N_GROUPS, jnp.where(lane == 1.0, i2 - N_GROUPS, 0.0))
    eid_ref[...] = eid.astype(jnp.int32)
    wt_ref[...] = jnp.where(lane == 0.0, w1, jnp.where(lane == 1.0, w2, 0.0))


def _router(x2, g_ffn, sc2, sh2, w_r):
    s, d = x2.shape
    tm = min(512, s)
    vec = lambda: pl.BlockSpec((1, d), lambda i: (0, 0))
    lanes_out = lambda: pl.BlockSpec((tm, LANES), lambda i: (i, 0))
    return pl.pallas_call(
        _router_kernel,
        out_shape=(jax.ShapeDtypeStruct((s, d), F32),
                   jax.ShapeDtypeStruct((s, LANES), jnp.int32),
                   jax.ShapeDtypeStruct((s, LANES), F32)),
        grid=(s // tm,),
        in_specs=[pl.BlockSpec((tm, d), lambda i: (i, 0)), vec(), vec(), vec(),
                  pl.BlockSpec((d, LANES), lambda i: (0, 0))],
        out_specs=(pl.BlockSpec((tm, d), lambda i: (i, 0)), lanes_out(), lanes_out()),
        compiler_params=_cparams(("arbitrary",)),
        name="router",
    )(x2, g_ffn, sc2, sh2, w_r)


def _rank_kernel(eid_ref, rank_ref, cnt_ref, run_ref):
    @pl.when(pl.program_id(0) == 0)
    def _():
        run_ref[...] = jnp.zeros_like(run_ref)

    tb = eid_ref.shape[0]
    lane = lax.broadcasted_iota(jnp.int32, (tb, LANES), 1)
    row = lax.broadcasted_iota(jnp.int32, (tb, tb), 0)
    col = lax.broadcasted_iota(jnp.int32, (tb, tb), 1)
    earlier = (col < row).astype(BF16)
    e = eid_ref[...]
    run = run_ref[...]
    ranks = []
    for k in range(2):
        onehot = e[:, k:k + 1] == lane
        oh = onehot.astype(F32)
        before = jnp.dot(earlier, oh.astype(BF16), preferred_element_type=F32) + run
        ranks.append(jnp.sum(jnp.where(onehot, before, 0.0), axis=-1, keepdims=True))
        run = run + jnp.sum(oh, axis=0, keepdims=True)
    rank_ref[...] = jnp.where(lane == 0, ranks[0], jnp.where(lane == 1, ranks[1], 0.0)).astype(jnp.int32)
    run_ref[...] = run
    cnt_ref[...] = run


def _rank(eid):
    s = eid.shape[0]
    tb = min(512, s)
    return pl.pallas_call(
        _rank_kernel,
        out_shape=(jax.ShapeDtypeStruct((s, LANES), jnp.int32), jax.ShapeDtypeStruct((1, LANES), F32)),
        grid=(s // tb,),
        in_specs=[pl.BlockSpec((tb, LANES), lambda i: (i, 0))],
        out_specs=(pl.BlockSpec((tb, LANES), lambda i: (i, 0)), pl.BlockSpec((1, LANES), lambda i: (0, 0))),
        scratch_shapes=[pltpu.VMEM((1, LANES), F32)],
        compiler_params=_cparams(("arbitrary",)),
        name="rank",
    )(eid)


def _plan_kernel(eid_ref, rank_ref, cnt_ref, dest_ref, src_ref, blke_ref, blkc_ref, nused_ref, pst_ref,
                 *, blk, n_blk):
    def per_expert(e, nb):
        c = cnt_ref[e]
        pst_ref[e] = nb * blk
        nbe = (c + (blk - 1)) // blk

        def per_block(jb, carry):
            blke_ref[nb + jb] = e
            blkc_ref[nb + jb] = jnp.minimum(blk, c - jb * blk)
            return carry

        lax.fori_loop(0, nbe, per_block, 0)
        return nb + nbe

    n_used = lax.fori_loop(0, N_EXPERTS, per_expert, jnp.int32(0))
    nused_ref[0] = n_used
    last_e = blke_ref[n_used - 1]

    def unused(i, carry):
        blke_ref[i] = last_e
        blkc_ref[i] = 0
        return carry

    lax.fori_loop(n_used, n_blk, unused, 0)

    def place(a, carry):
        dst = pst_ref[eid_ref[a]] + rank_ref[a]
        dest_ref[a] = dst
        src_ref[dst] = a
        return carry

    lax.fori_loop(0, eid_ref.shape[0], place, 0, unroll=8)


def _plan(eid_flat, rank_flat, counts, blk):
    a = eid_flat.shape[0]
    n_blk = a // blk + N_EXPERTS
    smem = lambda: pl.BlockSpec(memory_space=pltpu.SMEM)
    i32 = jnp.int32
    return pl.pallas_call(
        functools.partial(_plan_kernel, blk=blk, n_blk=n_blk),
        out_shape=(jax.ShapeDtypeStruct((a,), i32), jax.ShapeDtypeStruct((n_blk * blk,), i32),
                   jax.ShapeDtypeStruct((n_blk,), i32), jax.ShapeDtypeStruct((n_blk,), i32),
                   jax.ShapeDtypeStruct((1,), i32)),
        in_specs=[smem(), smem(), smem()],
        out_specs=(smem(), smem(), smem(), smem(), smem()),
        scratch_shapes=[pltpu.SMEM((N_EXPERTS,), i32)],
        name="plan",
    )(eid_flat, rank_flat, counts)


def _expert_kernel(blke_ref, blkc_ref, nused_ref, src_ref, h_hbm, w1_ref, w3_ref, w2_ref, o_ref, xbuf, sem,
                   *, blk):
    i = pl.program_id(0)
    c = pl.program_id(1)
    n_tok = h_hbm.shape[0]
    n_used = nused_ref[0]

    def row_copy(tok, slot, r):
        return pltpu.make_async_copy(h_hbm.at[pl.ds(tok, 1), :], xbuf.at[slot, pl.ds(r, 1), :], sem.at[slot])

    def issue(b, slot):
        def body(r, carry):
            row_copy(lax.rem(src_ref[b * blk + r], n_tok), slot, r).start()
            return carry
        lax.fori_loop(0, blkc_ref[b], body, 0)

    def wait(b, slot):
        def body(r, carry):
            row_copy(0, slot, 0).wait()
            return carry
        lax.fori_loop(0, blkc_ref[b], body, 0)

    slot = lax.rem(i, 2)

    @pl.when((i == 0) & (c == 0))
    def _():
        xbuf[...] = jnp.zeros_like(xbuf)
        issue(0, 0)

    @pl.when((c == 0) & (i < n_used))
    def _():
        wait(i, slot)

        @pl.when(i + 1 < n_used)
        def _():
            issue(i + 1, 1 - slot)

    @pl.when(i < n_used)
    def _():
        x = xbuf[slot].astype(BF16)
        a = jnp.dot(x, w1_ref[...].astype(BF16), preferred_element_type=F32)
        b = jnp.dot(x, w3_ref[...].astype(BF16), preferred_element_type=F32)
        hid = ((a * jax.nn.sigmoid(a)) * b).astype(BF16)
        y = jnp.dot(hid, w2_ref[...].astype(BF16), preferred_element_type=F32)

        @pl.when(c == 0)
        def _():
            o_ref[...] = y

        @pl.when(c > 0)
        def _():
            o_ref[...] += y


def _experts(h2, w1, w3, w2, blk_e, blk_cnt, n_used, row_src, blk):
    s, d = h2.shape
    de = w1.shape[-1]
    ck = min(256, de)
    nc = de // ck
    n_blk = blk_e.shape[0]

    def chunk(i, c, nu):
        return jnp.where(i < nu[0], c, nc - 1)

    return pl.pallas_call(
        functools.partial(_expert_kernel, blk=blk),
        out_shape=jax.ShapeDtypeStruct((n_blk * blk, d), F32),
        grid_spec=pltpu.PrefetchScalarGridSpec(
            num_scalar_prefetch=4,
            grid=(n_blk, nc),
            in_specs=[pl.BlockSpec(memory_space=pl.ANY),
                      pl.BlockSpec((None, d, ck), lambda i, c, be, bc, nu, rs: (be[i], 0, chunk(i, c, nu))),
                      pl.BlockSpec((None, d, ck), lambda i, c, be, bc, nu, rs: (be[i], 0, chunk(i, c, nu))),
                      pl.BlockSpec((None, ck, d), lambda i, c, be, bc, nu, rs: (be[i], chunk(i, c, nu), 0))],
            out_specs=pl.BlockSpec((blk, d), lambda i, c, be, bc, nu, rs: (jnp.minimum(i, nu[0] - 1), 0)),
            scratch_shapes=[pltpu.VMEM((2, blk, d), F32), pltpu.SemaphoreType.DMA((2,))]),
        compiler_params=_cparams(("arbitrary", "arbitrary")),
        name="experts",
    )(blk_e, blk_cnt, n_used, row_src, h2, w1, w3, w2)


def _combine_kernel(dest_ref, ys_hbm, x_ref, wt_ref, gt_ref, o_ref, ybuf, sem):
    i = pl.program_id(0)
    nb = pl.num_programs(0)
    tm = x_ref.shape[0]
    n_tok = tm * nb

    def row_copy(src_row, slot, k, r):
        return pltpu.make_async_copy(ys_hbm.at[pl.ds(src_row, 1), :], ybuf.at[slot, k, pl.ds(r, 1), :],
                                     sem.at[slot])

    def issue(b, slot):
        def body(r, carry):
            for k in range(2):
                row_copy(dest_ref[k * n_tok + b * tm + r], slot, k, r).start()
            return carry
        lax.fori_loop(0, tm, body, 0)

    def wait(slot):
        def body(r, carry):
            for k in range(2):
                row_copy(0, slot, k, 0).wait()
            return carry
        lax.fori_loop(0, tm, body, 0)

    slot = lax.rem(i, 2)

    @pl.when(i == 0)
    def _():
        issue(0, 0)

    @pl.when(i + 1 < nb)
    def _():
        issue(i + 1, 1 - slot)

    wait(slot)
    w = wt_ref[...]
    y = ybuf[slot, 0] * w[:, 0:1] + ybuf[slot, 1] * w[:, 1:2]
    o_ref[...] = x_ref[...] + gt_ref[...] * y


def _combine(dest, ys, x2, wts, gt2):
    s, d = x2.shape
    na = s // RES
    tm = min(256, na)
    nab = na // tm
    return pl.pallas_call(
        _combine_kernel,
        out_shape=jax.ShapeDtypeStruct((na, RES * d), F32),
        grid_spec=pltpu.PrefetchScalarGridSpec(
            num_scalar_prefetch=1,
            grid=(s // tm,),
            in_specs=[pl.BlockSpec(memory_space=pl.ANY),
                      pl.BlockSpec((tm, d), lambda i, ds_: (i, 0)),
                      pl.BlockSpec((tm, LANES), lambda i, ds_: (i, 0)),
                      pl.BlockSpec((1, d), lambda i, ds_: (0, 0))],
            out_specs=pl.BlockSpec((tm, d), lambda i, ds_: (i % nab, i // nab)),
            scratch_shapes=[pltpu.VMEM((2, 2, tm, d), F32), pltpu.SemaphoreType.DMA((2,))]),
        compiler_params=_cparams(("arbitrary",)),
        name="combine",
    )(dest, ys, x2, wts, gt2)


def _layer(x2d, c, w_ada, b_ada, g_mix, g_ffn, w_in, conv_w, q_norm_g, k_norm_g, g_branch, w_out,
           w_group, w_expert, w1, w3, w2):
    s, d = x2d.shape
    conv_ch = conv_w.shape[1]
    attn_w = d - conv_ch
    n_heads = attn_w // HEAD_DIM
    na = s // RES
    assert s % (RES * BAND) == 0
    row = lambda v: v.reshape(1, -1)
    x_view = x2d.reshape(na, RES * d)

    mod = _ada(c, w_ada, b_ada)
    sh1, sc1, gt1, sh2, sc2, gt2 = [mod[:, i * d:(i + 1) * d] for i in range(6)]

    z = _inproj(x_view, row(g_mix), sc1, sh1, w_in.astype(BF16), row(q_norm_g), row(k_norm_g), conv_ch, attn_w)
    z3 = z.reshape(RES, na, z.shape[1])
    y_conv = _conv(z3, conv_w, row(g_branch[:conv_ch]), conv_ch)
    slopes = jnp.exp2(-ALIBI_MAX_EXP * jnp.arange(1, n_heads + 1, dtype=F32) / n_heads)
    y_attn = _attention(z3, slopes, conv_ch, attn_w)
    x2 = _outproj(y_conv.reshape(s, conv_ch), y_attn.reshape(s, attn_w), row(g_branch[conv_ch:]),
                  w_out.astype(BF16), x_view, gt1)

    w_r = jnp.concatenate([w_group, w_expert,
                           jnp.zeros((d, LANES - N_GROUPS - N_EXPERTS), F32)], axis=1)
    h2, eid, wts = _router(x2, row(g_ffn), sc2, sh2, w_r)
    rank, counts = _rank(eid)
    eid_flat = eid[:, :2].T.reshape(-1)
    rank_flat = rank[:, :2].T.reshape(-1)
    cnt = counts[0, :N_EXPERTS].astype(jnp.int32)
    dest, row_src, blk_e, blk_cnt, n_used = _plan(eid_flat, rank_flat, cnt, MOE_BLOCK)
    ys = _experts(h2, w1, w3, w2, blk_e, blk_cnt, n_used, row_src, MOE_BLOCK)
    return _combine(dest, ys, x2, wts, gt2).reshape(s, d)


def kernel(x, c, w_ada, b_ada, g_mix, g_ffn, w_in, conv_w, q_norm_g, k_norm_g, g_branch, w_out,
           w_group, w_expert, w1, w3, w2):
    b, s, d = x.shape
    assert b == 1 and c.shape[0] == 1
    for l in range(w_ada.shape[0]):
        x = _layer(x.reshape(s, d), c[0], w_ada[l], b_ada[l], g_mix[l], g_ffn[l], w_in[l], conv_w[l],
                   q_norm_g[l], k_norm_g[l], g_branch[l], w_out[l], w_group[l], w_expert[l],
                   w1[l], w3[l], w2[l]).reshape(b, s, d)
    return x
```

```python
import functools

import jax
import jax.numpy as jnp
from jax import lax
from jax.experimental import pallas as pl
from jax.experimental.pallas import tpu as pltpu

EPS = 1e-6
HEAD_DIM = 128
BAND = 128
PATTERNS = ((128, 1), (512, 4), (2048, 16))
RES = 16
ALIBI_MAX_EXP = 8.0
N_GROUPS = 8
EXPERTS_PER_GROUP = 8
N_EXPERTS = N_GROUPS * EXPERTS_PER_GROUP
MOE_BLOCK = 256
GATHER_GROUP = 8
Y_TILE = 512
LANES = 128
SUBLANES = 8
VMEM_LIMIT = 56 * 1024 * 1024
F32 = jnp.float32
BF16 = jnp.bfloat16
NEG_INF = float("-inf")


def _cparams(sem):
    return pltpu.CompilerParams(dimension_semantics=sem, vmem_limit_bytes=VMEM_LIMIT)


def _ada_kernel(c_ref, w_ref, b_ref, o_ref, s_ref, *, rows):
    @pl.when(pl.program_id(0) == 0)
    def _():
        c = c_ref[...]
        s_ref[...] = c * jax.nn.sigmoid(c)

    d, tn = w_ref.shape

    def body(i, acc):
        r0 = pl.multiple_of(i * rows, rows)
        p = w_ref[pl.ds(r0, rows), :] * s_ref[pl.ds(r0, rows), :]
        return acc + jnp.sum(p.reshape(rows // SUBLANES, SUBLANES, tn), axis=0)

    acc = lax.fori_loop(0, d // rows, body, jnp.zeros((SUBLANES, tn), F32))
    o_ref[...] = jnp.sum(acc, axis=0, keepdims=True) + b_ref[...]


def _ada(c, w_ada, b_ada):
    d, n = w_ada.shape
    tn = min(512, n)
    rows = min(512, d)
    return pl.pallas_call(
        functools.partial(_ada_kernel, rows=rows),
        out_shape=jax.ShapeDtypeStruct((1, n), F32),
        grid=(n // tn,),
        in_specs=[pl.BlockSpec((d, 1), lambda j: (0, 0)),
                  pl.BlockSpec((d, tn), lambda j: (0, j)),
                  pl.BlockSpec((1, tn), lambda j: (0, j))],
        out_specs=pl.BlockSpec((1, tn), lambda j: (0, j)),
        scratch_shapes=[pltpu.VMEM((d, 1), F32)],
        compiler_params=_cparams(("arbitrary",)),
        name="ada",
    )(c.reshape(d, 1), w_ada, b_ada.reshape(1, n))


def _rms_mod(x, g, sc, sh):
    ms = jnp.mean(x * x, axis=-1, keepdims=True)
    return (x * lax.rsqrt(ms + EPS) * g) * (1.0 + sc) + sh


def _row_permutation(ta, to_residue_major):
    n = RES * ta
    assert ta & (ta - 1) == 0
    sh = ta.bit_length() - 1
    r = lax.broadcasted_iota(jnp.int32, (n, n), 0)
    c = lax.broadcasted_iota(jnp.int32, (n, n), 1)
    res_major, natural = (r, c) if to_residue_major else (c, r)
    return (natural == RES * (res_major & (ta - 1)) + (res_major >> sh)).astype(BF16)


def _norm1_kernel(x_ref, g_ref, sc_ref, sh_ref, o_ref):
    ta = o_ref.shape[1]
    h = _rms_mod(x_ref[...], g_ref[...], sc_ref[...], sh_ref[...]).astype(BF16)
    hp = jnp.dot(_row_permutation(ta, True), h, preferred_element_type=F32)
    o_ref[...] = hp.astype(BF16).reshape(o_ref.shape)


def _norm1(x2d, g_mix, sc1, sh1):
    s, d = x2d.shape
    na = s // RES
    ta = min(32, na)
    vec = lambda: pl.BlockSpec((1, d), lambda i: (0, 0))
    return pl.pallas_call(
        _norm1_kernel,
        out_shape=jax.ShapeDtypeStruct((RES, na, d), BF16),
        grid=(na // ta,),
        in_specs=[pl.BlockSpec((RES * ta, d), lambda i: (i, 0)), vec(), vec(), vec()],
        out_specs=pl.BlockSpec((RES, ta, d), lambda i: (0, i, 0)),
        compiler_params=_cparams(("arbitrary",)),
        name="norm1",
    )(x2d, g_mix, sc1, sh1)


def _inproj_kernel(h_ref, w_ref, qg_ref, kg_ref, z_ref, wb_ref, *, q0, k0, v0):
    j = pl.program_id(0)

    @pl.when(pl.program_id(1) == 0)
    def _():
        wb_ref[...] = w_ref[...].astype(BF16)

    acc = jnp.dot(h_ref[...], wb_ref[...], preferred_element_type=F32)
    is_q = (j >= q0) & (j < k0)
    is_k = (j >= k0) & (j < v0)

    @pl.when(is_q | is_k)
    def _():
        g = jnp.where(is_q, qg_ref[...] * (HEAD_DIM ** -0.5), kg_ref[...])
        for hh in range(acc.shape[1] // HEAD_DIM):
            a = acc[:, hh * HEAD_DIM:(hh + 1) * HEAD_DIM]
            ms = jnp.mean(a * a, axis=-1, keepdims=True)
            z_ref[:, hh * HEAD_DIM:(hh + 1) * HEAD_DIM] = a * lax.rsqrt(ms + EPS) * g

    @pl.when(jnp.logical_not(is_q | is_k))
    def _():
        z_ref[...] = acc


def _inproj(h1, w_in, q_g, k_g, conv_ch, attn_w):
    s, d = h1.shape
    n = w_in.shape[1]
    tm = min(1024, s)
    tn = min(512, conv_ch)
    assert conv_ch % tn == 0 and attn_w % tn == 0 and tn % HEAD_DIM == 0
    q0 = 3 * conv_ch // tn
    k0 = q0 + attn_w // tn
    v0 = k0 + attn_w // tn
    hvec = lambda: pl.BlockSpec((1, HEAD_DIM), lambda j, i: (0, 0))
    return pl.pallas_call(
        functools.partial(_inproj_kernel, q0=q0, k0=k0, v0=v0),
        out_shape=jax.ShapeDtypeStruct((s, n), F32),
        grid=(n // tn, s // tm),
        in_specs=[pl.BlockSpec((tm, d), lambda j, i: (i, 0)),
                  pl.BlockSpec((d, tn), lambda j, i: (0, j)), hvec(), hvec()],
        out_specs=pl.BlockSpec((tm, tn), lambda j, i: (i, j)),
        scratch_shapes=[pltpu.VMEM((d, tn), BF16)],
        compiler_params=_cparams(("arbitrary", "arbitrary")),
        name="inproj",
    )(h1, w_in, q_g, k_g)


def _conv_kernel(b_ref, c_ref, v_ref, cw_ref, g_ref, o_ref, ue_ref):
    i = pl.program_id(0)
    ta = b_ref.shape[1]
    w = cw_ref[...]
    assert w.shape[0] == 3
    for k in range(2):
        @pl.when(i == 0)
        def _():
            ue_ref[k, 0:SUBLANES, :] = jnp.zeros((SUBLANES, ue_ref.shape[2]), F32)

        @pl.when(i > 0)
        def _():
            ue_ref[k, 0:SUBLANES, :] = ue_ref[k, ta:ta + SUBLANES, :]

        r = RES - 2 + k
        ue_ref[k, SUBLANES:SUBLANES + ta, :] = c_ref[r] * v_ref[r]

    def u_at(bb):
        if bb >= 0:
            return c_ref[bb] * v_ref[bb]
        return ue_ref[2 + bb, SUBLANES - 1:SUBLANES - 1 + ta, :]

    for b in range(RES):
        y = b_ref[b] * (w[0:1, :] * u_at(b - 2) + w[1:2, :] * u_at(b - 1) + w[2:3, :] * u_at(b))
        ms = jnp.mean(y * y, axis=-1, keepdims=True)
        o_ref[b] = (y * lax.rsqrt(ms + EPS) * g_ref[...]).astype(BF16)


def _conv(z3, conv_w, g_conv, conv_ch):
    na = z3.shape[1]
    ta = min(64, na)
    blk = lambda jj: pl.BlockSpec((RES, ta, conv_ch), lambda i: (0, i, jj))
    return pl.pallas_call(
        _conv_kernel,
        out_shape=jax.ShapeDtypeStruct((RES, na, conv_ch), BF16),
        grid=(na // ta,),
        in_specs=[blk(0), blk(1), blk(2),
                  pl.BlockSpec(conv_w.shape, lambda i: (0, 0)),
                  pl.BlockSpec((1, conv_ch), lambda i: (0, 0))],
        out_specs=pl.BlockSpec((RES, ta, conv_ch), lambda i: (0, i, 0)),
        scratch_shapes=[pltpu.VMEM((2, ta + SUBLANES, conv_ch), F32)],
        compiler_params=_cparams(("arbitrary",)),
        name="conv",
    )(z3, z3, z3, conv_w, g_conv)


def _attn_kernel(slopes_ref, q_ref, kp_ref, kc_ref, vp_ref, vc_ref, o_ref, bias_ref, m_ref, l_ref, acc_ref):
    h = pl.program_id(0)
    cidx = pl.program_id(1)
    slope = slopes_ref[h]
    first_pen = jnp.where(cidx > 0, 0.0, NEG_INF).astype(F32)
    dn_t = (((1,), (1,)), ((), ()))

    j = lax.broadcasted_iota(jnp.int32, (BAND, 2 * BAND), 0)
    u = lax.broadcasted_iota(jnp.int32, (BAND, 2 * BAND), 1)
    is_prev = u < BAND
    uu = jnp.where(is_prev, u, u - BAND)
    for pi, (window, dil) in enumerate(PATTERNS):
        assert window // dil == BAND and RES % dil == 0
        ns, sr = RES // dil, BAND * dil // RES
        assert sr & (sr - 1) == 0
        sh = sr.bit_length() - 1
        mq = (j & (sr - 1)) * ns + (j >> sh)
        mk = (uu & (sr - 1)) * ns + (uu >> sh)
        steps = mq - mk + jnp.where(is_prev, BAND, 0)
        valid = (is_prev & (steps <= BAND)) | (jnp.logical_not(is_prev) & (steps >= 0))
        bias = jnp.where(valid, -slope * (steps * dil).astype(F32), NEG_INF)
        bias_ref[pi, 0] = bias
        bias_ref[pi, 1] = bias + jnp.where(is_prev, first_pen, 0.0)

    for pi, (window, dil) in enumerate(PATTERNS):
        ns, sr = RES // dil, BAND * dil // RES
        n_chunks = BAND // sr
        for n in range(n_chunks):
            for r in range(dil):
                def gather(ref, nn):
                    return jnp.concatenate([ref[dil * c + r, nn * sr:(nn + 1) * sr, :] for c in range(ns)], axis=0)

                def scatter(ref, val):
                    for c in range(ns):
                        ref[dil * c + r, n * sr:(n + 1) * sr, :] = val[c * sr:(c + 1) * sr, :]

                q = gather(q_ref, n).astype(BF16)
                if n > 0:
                    k_prev, v_prev = gather(kc_ref, n - 1), gather(vc_ref, n - 1)
                else:
                    k_prev, v_prev = gather(kp_ref, n_chunks - 1), gather(vp_ref, n_chunks - 1)
                kcat = jnp.concatenate([k_prev, gather(kc_ref, n)], axis=0).astype(BF16)
                vcat = jnp.concatenate([v_prev, gather(vc_ref, n)], axis=0).astype(BF16)
                s = lax.dot_general(q, kcat, dn_t, preferred_element_type=F32) + bias_ref[pi, 1 if n == 0 else 0]
                m = jnp.max(s, axis=-1, keepdims=True)
                p = jnp.exp(s - m)
                l = jnp.sum(p, axis=-1, keepdims=True)
                o = jnp.dot(p.astype(BF16), vcat, preferred_element_type=F32)
                m_b = jnp.broadcast_to(m, (BAND, HEAD_DIM))
                l_b = jnp.broadcast_to(l, (BAND, HEAD_DIM))
                if pi == 0:
                    scatter(m_ref, m_b)
                    scatter(l_ref, l_b)
                    scatter(acc_ref, o)
                else:
                    m_old = gather(m_ref, n)
                    m_new = jnp.maximum(m_old, m_b)
                    a_old = jnp.exp(m_old - m_new)
                    a_new = jnp.exp(m_b - m_new)
                    l_new = a_old * gather(l_ref, n) + a_new * l_b
                    o_new = a_old * gather(acc_ref, n) + a_new * o
                    if pi == len(PATTERNS) - 1:
                        scatter(acc_ref, o_new / l_new)
                    else:
                        scatter(m_ref, m_new)
                        scatter(l_ref, l_new)
                        scatter(acc_ref, o_new)
    o_ref[...] = acc_ref[...].astype(BF16)


def _attention(z3, slopes, conv_ch, attn_w):
    na = z3.shape[1]
    n_heads = attn_w // HEAD_DIM
    assert na % BAND == 0
    qb = 3 * conv_ch // HEAD_DIM
    kb = qb + n_heads
    vb = kb + n_heads
    blk = (RES, BAND, HEAD_DIM)
    cur = lambda base: pl.BlockSpec(blk, lambda h, c, sl: (0, c, base + h))
    prev = lambda base: pl.BlockSpec(blk, lambda h, c, sl: (0, jnp.maximum(c - 1, 0), base + h))
    return pl.pallas_call(
        _attn_kernel,
        out_shape=jax.ShapeDtypeStruct((RES, na, attn_w), BF16),
        grid_spec=pltpu.PrefetchScalarGridSpec(
            num_scalar_prefetch=1,
            grid=(n_heads, na // BAND),
            in_specs=[cur(qb), prev(kb), cur(kb), prev(vb), cur(vb)],
            out_specs=pl.BlockSpec(blk, lambda h, c, sl: (0, c, h)),
            scratch_shapes=[pltpu.VMEM((len(PATTERNS), 2, BAND, 2 * BAND), F32),
                            pltpu.VMEM(blk, F32), pltpu.VMEM(blk, F32), pltpu.VMEM(blk, F32)]),
        compiler_params=_cparams(("arbitrary", "arbitrary")),
        name="attn",
    )(slopes, z3, z3, z3, z3, z3)


def _outproj_kernel(yc_ref, ya_ref, ga_ref, w_ref, x_ref, gt_ref, o_ref, y_ref):
    _, ta, cc = yc_ref.shape

    @pl.when(pl.program_id(1) == 0)
    def _():
        n = RES * ta
        ya = ya_ref[...].reshape(n, ya_ref.shape[2]).astype(F32)
        ms = jnp.mean(ya * ya, axis=-1, keepdims=True)
        yn = (ya * lax.rsqrt(ms + EPS) * ga_ref[...]).astype(BF16)
        perm = _row_permutation(ta, False)
        y_ref[:, 0:cc] = jnp.dot(perm, yc_ref[...].reshape(n, cc), preferred_element_type=F32).astype(BF16)
        y_ref[:, cc:] = jnp.dot(perm, yn, preferred_element_type=F32).astype(BF16)

    acc = jnp.dot(y_ref[...], w_ref[...], preferred_element_type=F32)
    o_ref[...] = x_ref[...] + gt_ref[...] * acc


def _outproj(y_conv3, y_attn3, g_attn, w_out_bf, x2d, gt1):
    s, d = x2d.shape
    na = s // RES
    cc, aw = y_conv3.shape[2], y_attn3.shape[2]
    ta = min(32, na)
    tn = min(1024, d)
    return pl.pallas_call(
        _outproj_kernel,
        out_shape=jax.ShapeDtypeStruct((s, d), F32),
        grid=(na // ta, d // tn),
        in_specs=[pl.BlockSpec((RES, ta, cc), lambda i, j: (0, i, 0)),
                  pl.BlockSpec((RES, ta, aw), lambda i, j: (0, i, 0)),
                  pl.BlockSpec((1, aw), lambda i, j: (0, 0)),
                  pl.BlockSpec((cc + aw, tn), lambda i, j: (0, j)),
                  pl.BlockSpec((RES * ta, tn), lambda i, j: (i, j)),
                  pl.BlockSpec((1, tn), lambda i, j: (0, j))],
        out_specs=pl.BlockSpec((RES * ta, tn), lambda i, j: (i, j)),
        scratch_shapes=[pltpu.VMEM((RES * ta, cc + aw), BF16)],
        compiler_params=_cparams(("arbitrary", "arbitrary")),
        name="outproj",
    )(y_conv3, y_attn3, g_attn, w_out_bf, x2d, gt1)


def _router_kernel(x_ref, g_ref, sc_ref, sh_ref, wr_ref, h_ref, eid_ref, wt_ref):
    h = _rms_mod(x_ref[...], g_ref[...], sc_ref[...], sh_ref[...])
    h_ref[...] = h
    logits = jnp.dot(h, wr_ref[...], preferred_element_type=F32, precision=lax.Precision.HIGHEST)
    tm = logits.shape[0]
    lane = lax.broadcasted_iota(jnp.int32, (tm, LANES), 1).astype(F32)
    big = float(LANES)
    is_g = lane < N_GROUPS
    gl = jnp.where(is_g, logits, NEG_INF)
    gmax = jnp.max(gl, axis=-1, keepdims=True)
    gsel = jnp.min(jnp.where(gl == gmax, lane, big), axis=-1, keepdims=True)
    g_w = 1.0 / jnp.sum(jnp.where(is_g, jnp.exp(gl - gmax), 0.0), axis=-1, keepdims=True)
    lo = N_GROUPS + gsel * EXPERTS_PER_GROUP
    in_grp = (lane >= lo) & (lane < lo + EXPERTS_PER_GROUP)
    el = jnp.where(in_grp, logits, NEG_INF)
    e1 = jnp.max(el, axis=-1, keepdims=True)
    i1 = jnp.min(jnp.where(el == e1, lane, big), axis=-1, keepdims=True)
    el2 = jnp.where(lane == i1, NEG_INF, el)
    e2 = jnp.max(el2, axis=-1, keepdims=True)
    i2 = jnp.min(jnp.where(el2 == e2, lane, big), axis=-1, keepdims=True)
    t = jnp.exp(e2 - e1)
    w1 = g_w / (1.0 + t)
    w2 = g_w * t / (1.0 + t)
    eid = jnp.where(lane == 0.0, i1 - N_GROUPS, jnp.where(lane == 1.0, i2 - N_GROUPS, 0.0))
    eid_ref[...] = eid.astype(jnp.int32)
    wt_ref[...] = jnp.where(lane == 0.0, w1, jnp.where(lane == 1.0, w2, 0.0))


def _router(x2, g_ffn, sc2, sh2, w_r):
    s, d = x2.shape
    tm = min(512, s)
    vec = lambda: pl.BlockSpec((1, d), lambda i: (0, 0))
    lanes_out = lambda: pl.BlockSpec((tm, LANES), lambda i: (i, 0))
    return pl.pallas_call(
        _router_kernel,
        out_shape=(jax.ShapeDtypeStruct((s, d), F32),
                   jax.ShapeDtypeStruct((s, LANES), jnp.int32),
                   jax.ShapeDtypeStruct((s, LANES), F32)),
        grid=(s // tm,),
        in_specs=[pl.BlockSpec((tm, d), lambda i: (i, 0)), vec(), vec(), vec(),
                  pl.BlockSpec((d, LANES), lambda i: (0, 0))],
        out_specs=(pl.BlockSpec((tm, d), lambda i: (i, 0)), lanes_out(), lanes_out()),
        compiler_params=_cparams(("arbitrary",)),
        name="router",
    )(x2, g_ffn, sc2, sh2, w_r)


def _rank_kernel(eid_ref, rank_ref, cnt_ref, run_ref):
    @pl.when(pl.program_id(0) == 0)
    def _():
        run_ref[...] = jnp.zeros_like(run_ref)

    tb = eid_ref.shape[0]
    lane = lax.broadcasted_iota(jnp.int32, (tb, LANES), 1)
    row = lax.broadcasted_iota(jnp.int32, (tb, tb), 0)
    col = lax.broadcasted_iota(jnp.int32, (tb, tb), 1)
    earlier = (col < row).astype(BF16)
    e = eid_ref[...]
    run = run_ref[...]
    ranks = []
    for k in range(2):
        onehot = e[:, k:k + 1] == lane
        oh = onehot.astype(F32)
        before = jnp.dot(earlier, oh.astype(BF16), preferred_element_type=F32) + run
        ranks.append(jnp.sum(jnp.where(onehot, before, 0.0), axis=-1, keepdims=True))
        run = run + jnp.sum(oh, axis=0, keepdims=True)
    rank_ref[...] = jnp.where(lane == 0, ranks[0], jnp.where(lane == 1, ranks[1], 0.0)).astype(jnp.int32)
    run_ref[...] = run
    cnt_ref[...] = run


def _rank(eid):
    s = eid.shape[0]
    tb = min(512, s)
    return pl.pallas_call(
        _rank_kernel,
        out_shape=(jax.ShapeDtypeStruct((s, LANES), jnp.int32), jax.ShapeDtypeStruct((1, LANES), F32)),
        grid=(s // tb,),
        in_specs=[pl.BlockSpec((tb, LANES), lambda i: (i, 0))],
        out_specs=(pl.BlockSpec((tb, LANES), lambda i: (i, 0)), pl.BlockSpec((1, LANES), lambda i: (0, 0))),
        scratch_shapes=[pltpu.VMEM((1, LANES), F32)],
        compiler_params=_cparams(("arbitrary",)),
        name="rank",
    )(eid)


def _plan_kernel(eid_ref, rank_ref, cnt_ref, dest_ref, src_ref, blke_ref, blkc_ref, nused_ref, pst_ref,
                 *, blk, n_blk):
    def per_expert(e, nb):
        c = cnt_ref[e]
        pst_ref[e] = nb * blk
        nbe = (c + (blk - 1)) // blk

        def per_block(jb, carry):
            blke_ref[nb + jb] = e
            blkc_ref[nb + jb] = jnp.minimum(blk, c - jb * blk)
            return carry

        lax.fori_loop(0, nbe, per_block, 0)

        def pad_slot(r, carry):
            src_ref[nb * blk + r] = 0
            return carry

        lax.fori_loop(c, (c + (GATHER_GROUP - 1)) // GATHER_GROUP * GATHER_GROUP, pad_slot, 0)
        return nb + nbe

    n_used = lax.fori_loop(0, N_EXPERTS, per_expert, jnp.int32(0))
    nused_ref[0] = n_used
    last_e = blke_ref[n_used - 1]

    def unused(i, carry):
        blke_ref[i] = last_e
        blkc_ref[i] = 0
        return carry

    lax.fori_loop(n_used, n_blk, unused, 0)

    n_tok = eid_ref.shape[0] // 2

    def place(a, carry):
        dst = pst_ref[eid_ref[a]] + rank_ref[a]
        dest_ref[a] = dst
        src_ref[dst] = jnp.where(a >= n_tok, a - n_tok, a)
        return carry

    lax.fori_loop(0, eid_ref.shape[0], place, 0, unroll=8)


def _plan(eid_flat, rank_flat, counts, blk):
    a = eid_flat.shape[0]
    n_blk = a // blk + N_EXPERTS
    smem = lambda: pl.BlockSpec(memory_space=pltpu.SMEM)
    i32 = jnp.int32
    return pl.pallas_call(
        functools.partial(_plan_kernel, blk=blk, n_blk=n_blk),
        out_shape=(jax.ShapeDtypeStruct((a,), i32), jax.ShapeDtypeStruct((n_blk * blk,), i32),
                   jax.ShapeDtypeStruct((n_blk,), i32), jax.ShapeDtypeStruct((n_blk,), i32),
                   jax.ShapeDtypeStruct((1,), i32)),
        in_specs=[smem(), smem(), smem()],
        out_specs=(smem(), smem(), smem(), smem(), smem()),
        scratch_shapes=[pltpu.SMEM((N_EXPERTS,), i32)],
        name="plan",
    )(eid_flat, rank_flat, counts)


def _gather_rows(idx_of_row, n_rows, src_hbm, dst, sem):
    def body(g, carry):
        for rr in range(GATHER_GROUP):
            r = g * GATHER_GROUP + rr
            pltpu.make_async_copy(src_hbm.at[pl.ds(idx_of_row(r), 1), :], dst.at[pl.ds(r, 1), :], sem).start()
        return carry
    lax.fori_loop(0, (n_rows + (GATHER_GROUP - 1)) // GATHER_GROUP, body, 0)


def _wait_rows(n_rows, src_hbm, dst, sem):
    def body(g, carry):
        pltpu.make_async_copy(src_hbm.at[pl.ds(0, GATHER_GROUP), :], dst.at[pl.ds(0, GATHER_GROUP), :], sem).wait()
        return carry
    lax.fori_loop(0, (n_rows + (GATHER_GROUP - 1)) // GATHER_GROUP, body, 0)


def _expert_kernel(blke_ref, blkc_ref, nused_ref, src_ref, h_hbm, w1_hbm, w3_hbm, w2_hbm, o_ref,
                   xbuf, w1b, w3b, w2b, gsem, wsem, *, blk, kc):
    i = pl.program_id(0)
    last = pl.num_programs(0) - 1
    n_used = nused_ref[0]
    e = blke_ref[i]
    e_next = blke_ref[jnp.minimum(i + 1, last)]
    first_of_expert = (i == 0) | (blke_ref[jnp.maximum(i - 1, 0)] != e)
    fetch_next = (i + 1 < n_used) & (e_next != e)
    wbufs = ((w1_hbm, w1b), (w3_hbm, w3b), (w2_hbm, w2b))

    def weight_copy(k, ee):
        return pltpu.make_async_copy(wbufs[k][0].at[ee], wbufs[k][1], wsem.at[k])

    def issue_rows(b, slot):
        _gather_rows(lambda r: src_ref[b * blk + r], blkc_ref[b], h_hbm, xbuf.at[slot], gsem.at[slot])

    def weights_ready(k):
        @pl.when(first_of_expert)
        def _():
            weight_copy(k, e).wait()

    def weights_done(k):
        @pl.when(fetch_next)
        def _():
            weight_copy(k, e_next).start()

    slot = lax.rem(i, 2)

    @pl.when(i == 0)
    def _():
        for k in range(3):
            weight_copy(k, e).start()
        xbuf[...] = jnp.zeros_like(xbuf)
        issue_rows(0, 0)

    @pl.when(i < n_used)
    def _():
        _wait_rows(blkc_ref[i], h_hbm, xbuf.at[slot], gsem.at[slot])

        @pl.when(i + 1 < n_used)
        def _():
            issue_rows(i + 1, 1 - slot)

        d = xbuf.shape[2]

        def up_proj(wb):
            acc = None
            for k0 in range(0, d, kc):
                part = jnp.dot(xbuf[slot, :, k0:k0 + kc].astype(BF16), wb[k0:k0 + kc, :].astype(BF16),
                               preferred_element_type=F32)
                acc = part if acc is None else acc + part
            return acc

        weights_ready(0)
        a = up_proj(w1b)
        weights_done(0)
        weights_ready(1)
        b = up_proj(w3b)
        weights_done(1)
        hid = ((a * jax.nn.sigmoid(a)) * b).astype(BF16)
        weights_ready(2)
        for t in range(0, d, Y_TILE):
            o_ref[:, t:t + Y_TILE] = jnp.dot(hid, w2b[:, t:t + Y_TILE].astype(BF16), preferred_element_type=F32)
        weights_done(2)


def _experts(h2, w1, w3, w2, blk_e, blk_cnt, n_used, row_src, blk):
    s, d = h2.shape
    de = w1.shape[-1]
    kc = min(1024, d)
    assert d % kc == 0 and d % Y_TILE == 0 and blk % GATHER_GROUP == 0
    n_blk = blk_e.shape[0]
    anyspace = lambda: pl.BlockSpec(memory_space=pl.ANY)
    return pl.pallas_call(
        functools.partial(_expert_kernel, blk=blk, kc=kc),
        out_shape=jax.ShapeDtypeStruct((n_blk * blk, d), F32),
        grid_spec=pltpu.PrefetchScalarGridSpec(
            num_scalar_prefetch=4,
            grid=(n_blk,),
            in_specs=[anyspace(), anyspace(), anyspace(), anyspace()],
            out_specs=pl.BlockSpec((blk, d), lambda i, be, bc, nu, rs: (jnp.minimum(i, nu[0] - 1), 0)),
            scratch_shapes=[pltpu.VMEM((2, blk, d), F32),
                            pltpu.VMEM((d, de), F32), pltpu.VMEM((d, de), F32), pltpu.VMEM((de, d), F32),
                            pltpu.SemaphoreType.DMA((2,)), pltpu.SemaphoreType.DMA((3,))]),
        compiler_params=_cparams(("arbitrary",)),
        name="experts",
    )(blk_e, blk_cnt, n_used, row_src, h2, w1, w3, w2)


def _combine_kernel(dest_ref, ys_hbm, x_ref, wt_ref, gt_ref, o_ref, ybuf, sem):
    i = pl.program_id(0)
    nb = pl.num_programs(0)
    tm = x_ref.shape[0]
    n_tok = tm * nb

    def issue(blk_i, slot):
        for k in range(2):
            _gather_rows(lambda r: dest_ref[k * n_tok + blk_i * tm + r], tm, ys_hbm, ybuf.at[slot, k],
                         sem.at[slot])

    slot = lax.rem(i, 2)

    @pl.when(i == 0)
    def _():
        issue(0, 0)

    @pl.when(i + 1 < nb)
    def _():
        issue(i + 1, 1 - slot)

    for k in range(2):
        _wait_rows(tm, ys_hbm, ybuf.at[slot, k], sem.at[slot])
    w = wt_ref[...]
    y = ybuf[slot, 0] * w[:, 0:1] + ybuf[slot, 1] * w[:, 1:2]
    o_ref[...] = x_ref[...] + gt_ref[...] * y


def _combine(dest, ys, x2, wts, gt2):
    s, d = x2.shape
    tm = min(256, s)
    return pl.pallas_call(
        _combine_kernel,
        out_shape=jax.ShapeDtypeStruct((s, d), F32),
        grid_spec=pltpu.PrefetchScalarGridSpec(
            num_scalar_prefetch=1,
            grid=(s // tm,),
            in_specs=[pl.BlockSpec(memory_space=pl.ANY),
                      pl.BlockSpec((tm, d), lambda i, ds_: (i, 0)),
                      pl.BlockSpec((tm, LANES), lambda i, ds_: (i, 0)),
                      pl.BlockSpec((1, d), lambda i, ds_: (0, 0))],
            out_specs=pl.BlockSpec((tm, d), lambda i, ds_: (i, 0)),
            scratch_shapes=[pltpu.VMEM((2, 2, tm, d), F32), pltpu.SemaphoreType.DMA((2,))]),
        compiler_params=_cparams(("arbitrary",)),
        name="combine",
    )(dest, ys, x2, wts, gt2)


def _layer(x2d, c, w_ada, b_ada, g_mix, g_ffn, w_in, conv_w, q_norm_g, k_norm_g, g_branch, w_out,
           w_group, w_expert, w1, w3, w2):
    s, d = x2d.shape
    conv_ch = conv_w.shape[1]
    attn_w = d - conv_ch
    n_heads = attn_w // HEAD_DIM
    na = s // RES
    assert s % (RES * BAND) == 0
    row = lambda v: v.reshape(1, -1)

    mod = _ada(c, w_ada, b_ada)
    sh1, sc1, gt1, sh2, sc2, gt2 = [mod[:, i * d:(i + 1) * d] for i in range(6)]

    h1 = _norm1(x2d, row(g_mix), sc1, sh1)
    z = _inproj(h1.reshape(s, d), w_in, row(q_norm_g), row(k_norm_g), conv_ch, attn_w)
    z3 = z.reshape(RES, na, z.shape[1])
    y_conv = _conv(z3, conv_w, row(g_branch[:conv_ch]), conv_ch)
    slopes = jnp.exp2(-ALIBI_MAX_EXP * jnp.arange(1, n_heads + 1, dtype=F32) / n_heads)
    y_attn = _attention(z3, slopes, conv_ch, attn_w)
    x2 = _outproj(y_conv, y_attn, row(g_branch[conv_ch:]), w_out.astype(BF16), x2d, gt1)

    w_r = jnp.concatenate([w_group, w_expert,
                           jnp.zeros((d, LANES - N_GROUPS - N_EXPERTS), F32)], axis=1)
    h2, eid, wts = _router(x2, row(g_ffn), sc2, sh2, w_r)
    rank, counts = _rank(eid)
    eid_flat = eid[:, :2].T.reshape(-1)
    rank_flat = rank[:, :2].T.reshape(-1)
    cnt = counts[0, :N_EXPERTS].astype(jnp.int32)
    dest, row_src, blk_e, blk_cnt, n_used = _plan(eid_flat, rank_flat, cnt, MOE_BLOCK)
    ys = _experts(h2, w1, w3, w2, blk_e, blk_cnt, n_used, row_src, MOE_BLOCK)
    return _combine(dest, ys, x2, wts, gt2)


def kernel(x, c, w_ada, b_ada, g_mix, g_ffn, w_in, conv_w, q_norm_g, k_norm_g, g_branch, w_out,
           w_group, w_expert, w1, w3, w2):
    b, s, d = x.shape
    assert b == 1 and c.shape[0] == 1
    for l in range(w_ada.shape[0]):
        x = _layer(x.reshape(s, d), c[0], w_ada[l], b_ada[l], g_mix[l], g_ffn[l], w_in[l], conv_w[l],
                   q_norm_g[l], k_norm_g[l], g_branch[l], w_out[l], w_group[l], w_expert[l],
                   w1[l], w3[l], w2[l]).reshape(b, s, d)
    return x
```

```python
import functools

import jax
import jax.numpy as jnp
from jax import lax
from jax.experimental import pallas as pl
from jax.experimental.pallas import tpu as pltpu

EPS = 1e-6
HEAD_DIM = 128
BAND = 128
PATTERNS = ((128, 1), (512, 4), (2048, 16))
RES = 16
ALIBI_MAX_EXP = 8.0
N_GROUPS = 8
EXPERTS_PER_GROUP = 8
N_EXPERTS = N_GROUPS * EXPERTS_PER_GROUP
MOE_BLOCK = 128
GATHER_GROUP = 8
Y_TILE = 512
LANES = 128
SUBLANES = 8
VMEM_LIMIT = 56 * 1024 * 1024
F32 = jnp.float32
BF16 = jnp.bfloat16
NEG_INF = float("-inf")


def _cparams(sem):
    return pltpu.CompilerParams(dimension_semantics=sem, vmem_limit_bytes=VMEM_LIMIT)


def _ada_kernel(c_ref, w_ref, b_ref, o_ref, s_ref, *, rows):
    @pl.when(pl.program_id(0) == 0)
    def _():
        c = c_ref[...]
        s_ref[...] = c * jax.nn.sigmoid(c)

    d, tn = w_ref.shape

    def body(i, acc):
        r0 = pl.multiple_of(i * rows, rows)
        p = w_ref[pl.ds(r0, rows), :] * s_ref[pl.ds(r0, rows), :]
        return acc + jnp.sum(p.reshape(rows // SUBLANES, SUBLANES, tn), axis=0)

    acc = lax.fori_loop(0, d // rows, body, jnp.zeros((SUBLANES, tn), F32))
    o_ref[...] = jnp.sum(acc, axis=0, keepdims=True) + b_ref[...]


def _ada(c, w_ada, b_ada):
    d, n = w_ada.shape
    tn = min(512, n)
    rows = min(512, d)
    return pl.pallas_call(
        functools.partial(_ada_kernel, rows=rows),
        out_shape=jax.ShapeDtypeStruct((1, n), F32),
        grid=(n // tn,),
        in_specs=[pl.BlockSpec((d, 1), lambda j: (0, 0)),
                  pl.BlockSpec((d, tn), lambda j: (0, j)),
                  pl.BlockSpec((1, tn), lambda j: (0, j))],
        out_specs=pl.BlockSpec((1, tn), lambda j: (0, j)),
        scratch_shapes=[pltpu.VMEM((d, 1), F32)],
        compiler_params=_cparams(("arbitrary",)),
        name="ada",
    )(c.reshape(d, 1), w_ada, b_ada.reshape(1, n))


def _rms_mod(x, g, sc, sh):
    ms = jnp.mean(x * x, axis=-1, keepdims=True)
    return (x * lax.rsqrt(ms + EPS) * g) * (1.0 + sc) + sh


def _row_permutation(ta, to_residue_major):
    n = RES * ta
    assert ta & (ta - 1) == 0
    sh = ta.bit_length() - 1
    r = lax.broadcasted_iota(jnp.int32, (n, n), 0)
    c = lax.broadcasted_iota(jnp.int32, (n, n), 1)
    res_major, natural = (r, c) if to_residue_major else (c, r)
    return (natural == RES * (res_major & (ta - 1)) + (res_major >> sh)).astype(BF16)


def _norm1_kernel(x_ref, g_ref, sc_ref, sh_ref, o_ref):
    ta = o_ref.shape[1]
    h = _rms_mod(x_ref[...], g_ref[...], sc_ref[...], sh_ref[...]).astype(BF16)
    hp = jnp.dot(_row_permutation(ta, True), h, preferred_element_type=F32)
    o_ref[...] = hp.astype(BF16).reshape(o_ref.shape)


def _norm1(x2d, g_mix, sc1, sh1):
    s, d = x2d.shape
    na = s // RES
    ta = min(32, na)
    vec = lambda: pl.BlockSpec((1, d), lambda i: (0, 0))
    return pl.pallas_call(
        _norm1_kernel,
        out_shape=jax.ShapeDtypeStruct((RES, na, d), BF16),
        grid=(na // ta,),
        in_specs=[pl.BlockSpec((RES * ta, d), lambda i: (i, 0)), vec(), vec(), vec()],
        out_specs=pl.BlockSpec((RES, ta, d), lambda i: (0, i, 0)),
        compiler_params=_cparams(("arbitrary",)),
        name="norm1",
    )(x2d, g_mix, sc1, sh1)


def _inproj_kernel(h_ref, w_ref, qg_ref, kg_ref, z_ref, wb_ref, *, q0, k0, v0):
    j = pl.program_id(0)

    @pl.when(pl.program_id(1) == 0)
    def _():
        wb_ref[...] = w_ref[...].astype(BF16)

    acc = jnp.dot(h_ref[...], wb_ref[...], preferred_element_type=F32)
    is_q = (j >= q0) & (j < k0)
    is_k = (j >= k0) & (j < v0)

    g = jnp.where(is_q, qg_ref[...] * (HEAD_DIM ** -0.5), kg_ref[...])
    for hh in range(acc.shape[1] // HEAD_DIM):
        a = acc[:, hh * HEAD_DIM:(hh + 1) * HEAD_DIM]
        ms = jnp.mean(a * a, axis=-1, keepdims=True)
        scale = jnp.where(is_q | is_k, lax.rsqrt(ms + EPS) * g, 1.0)
        z_ref[:, hh * HEAD_DIM:(hh + 1) * HEAD_DIM] = a * scale


def _inproj(h1, w_in, q_g, k_g, conv_ch, attn_w):
    s, d = h1.shape
    n = w_in.shape[1]
    tm = min(1024, s)
    tn = min(512, conv_ch)
    assert conv_ch % tn == 0 and attn_w % tn == 0 and tn % HEAD_DIM == 0
    q0 = 3 * conv_ch // tn
    k0 = q0 + attn_w // tn
    v0 = k0 + attn_w // tn
    hvec = lambda: pl.BlockSpec((1, HEAD_DIM), lambda j, i: (0, 0))
    return pl.pallas_call(
        functools.partial(_inproj_kernel, q0=q0, k0=k0, v0=v0),
        out_shape=jax.ShapeDtypeStruct((s, n), F32),
        grid=(n // tn, s // tm),
        in_specs=[pl.BlockSpec((tm, d), lambda j, i: (i, 0)),
                  pl.BlockSpec((d, tn), lambda j, i: (0, j)), hvec(), hvec()],
        out_specs=pl.BlockSpec((tm, tn), lambda j, i: (i, j)),
        scratch_shapes=[pltpu.VMEM((d, tn), BF16)],
        compiler_params=_cparams(("arbitrary", "arbitrary")),
        name="inproj",
    )(h1, w_in, q_g, k_g)


def _conv_kernel(b_ref, c_ref, v_ref, cw_ref, g_ref, o_ref, ue_ref):
    i = pl.program_id(0)
    ta = b_ref.shape[1]
    w = cw_ref[...]
    assert w.shape[0] == 3
    for k in range(2):
        @pl.when(i == 0)
        def _():
            ue_ref[k, 0:SUBLANES, :] = jnp.zeros((SUBLANES, ue_ref.shape[2]), F32)

        @pl.when(i > 0)
        def _():
            ue_ref[k, 0:SUBLANES, :] = ue_ref[k, ta:ta + SUBLANES, :]

        r = RES - 2 + k
        ue_ref[k, SUBLANES:SUBLANES + ta, :] = c_ref[r] * v_ref[r]

    def u_at(bb):
        if bb >= 0:
            return c_ref[bb] * v_ref[bb]
        return ue_ref[2 + bb, SUBLANES - 1:SUBLANES - 1 + ta, :]

    for b in range(RES):
        y = b_ref[b] * (w[0:1, :] * u_at(b - 2) + w[1:2, :] * u_at(b - 1) + w[2:3, :] * u_at(b))
        ms = jnp.mean(y * y, axis=-1, keepdims=True)
        o_ref[b] = (y * lax.rsqrt(ms + EPS) * g_ref[...]).astype(BF16)


def _conv(z3, conv_w, g_conv, conv_ch):
    na = z3.shape[1]
    ta = min(64, na)
    blk = lambda jj: pl.BlockSpec((RES, ta, conv_ch), lambda i: (0, i, jj))
    return pl.pallas_call(
        _conv_kernel,
        out_shape=jax.ShapeDtypeStruct((RES, na, conv_ch), BF16),
        grid=(na // ta,),
        in_specs=[blk(0), blk(1), blk(2),
                  pl.BlockSpec(conv_w.shape, lambda i: (0, 0)),
                  pl.BlockSpec((1, conv_ch), lambda i: (0, 0))],
        out_specs=pl.BlockSpec((RES, ta, conv_ch), lambda i: (0, i, 0)),
        scratch_shapes=[pltpu.VMEM((2, ta + SUBLANES, conv_ch), F32)],
        compiler_params=_cparams(("arbitrary",)),
        name="conv",
    )(z3, z3, z3, conv_w, g_conv)


def _attn_kernel(slopes_ref, q_ref, kp_ref, kc_ref, vp_ref, vc_ref, o_ref, bias_ref, m_ref, l_ref, acc_ref):
    h = pl.program_id(0)
    cidx = pl.program_id(1)
    slope = slopes_ref[h]
    first_pen = jnp.where(cidx > 0, 0.0, NEG_INF).astype(F32)
    dn_t = (((1,), (1,)), ((), ()))

    j = lax.broadcasted_iota(jnp.int32, (BAND, 2 * BAND), 0)
    u = lax.broadcasted_iota(jnp.int32, (BAND, 2 * BAND), 1)
    is_prev = u < BAND
    uu = jnp.where(is_prev, u, u - BAND)
    for pi, (window, dil) in enumerate(PATTERNS):
        assert window // dil == BAND and RES % dil == 0
        ns, sr = RES // dil, BAND * dil // RES
        assert sr & (sr - 1) == 0
        sh = sr.bit_length() - 1
        mq = (j & (sr - 1)) * ns + (j >> sh)
        mk = (uu & (sr - 1)) * ns + (uu >> sh)
        steps = mq - mk + jnp.where(is_prev, BAND, 0)
        valid = (is_prev & (steps <= BAND)) | (jnp.logical_not(is_prev) & (steps >= 0))
        bias = jnp.where(valid, -slope * (steps * dil).astype(F32), NEG_INF)
        bias_ref[pi, 0] = bias
        bias_ref[pi, 1] = bias + jnp.where(is_prev, first_pen, 0.0)

    for pi, (window, dil) in enumerate(PATTERNS):
        ns, sr = RES // dil, BAND * dil // RES
        n_chunks = BAND // sr
        for n in range(n_chunks):
            for r in range(dil):
                def gather(ref, nn):
                    return jnp.concatenate([ref[dil * c + r, nn * sr:(nn + 1) * sr, :] for c in range(ns)], axis=0)

                def scatter(ref, val):
                    for c in range(ns):
                        ref[dil * c + r, n * sr:(n + 1) * sr, :] = val[c * sr:(c + 1) * sr, :]

                q = gather(q_ref, n).astype(BF16)
                if n > 0:
                    k_prev, v_prev = gather(kc_ref, n - 1), gather(vc_ref, n - 1)
                else:
                    k_prev, v_prev = gather(kp_ref, n_chunks - 1), gather(vp_ref, n_chunks - 1)
                kcat = jnp.concatenate([k_prev, gather(kc_ref, n)], axis=0).astype(BF16)
                vcat = jnp.concatenate([v_prev, gather(vc_ref, n)], axis=0).astype(BF16)
                s = lax.dot_general(q, kcat, dn_t, preferred_element_type=F32) + bias_ref[pi, 1 if n == 0 else 0]
                m = jnp.max(s, axis=-1, keepdims=True)
                p = jnp.exp(s - m)
                l = jnp.sum(p, axis=-1, keepdims=True)
                o = jnp.dot(p.astype(BF16), vcat, preferred_element_type=F32)
                m_b = jnp.broadcast_to(m, (BAND, HEAD_DIM))
                l_b = jnp.broadcast_to(l, (BAND, HEAD_DIM))
                if pi == 0:
                    scatter(m_ref, m_b)
                    scatter(l_ref, l_b)
                    scatter(acc_ref, o)
                else:
                    m_old = gather(m_ref, n)
                    m_new = jnp.maximum(m_old, m_b)
                    a_old = jnp.exp(m_old - m_new)
                    a_new = jnp.exp(m_b - m_new)
                    l_new = a_old * gather(l_ref, n) + a_new * l_b
                    o_new = a_old * gather(acc_ref, n) + a_new * o
                    if pi == len(PATTERNS) - 1:
                        scatter(acc_ref, o_new / l_new)
                    else:
                        scatter(m_ref, m_new)
                        scatter(l_ref, l_new)
                        scatter(acc_ref, o_new)
    o_ref[...] = acc_ref[...].astype(BF16)


def _attention(z3, slopes, conv_ch, attn_w):
    na = z3.shape[1]
    n_heads = attn_w // HEAD_DIM
    assert na % BAND == 0
    qb = 3 * conv_ch // HEAD_DIM
    kb = qb + n_heads
    vb = kb + n_heads
    blk = (RES, BAND, HEAD_DIM)
    cur = lambda base: pl.BlockSpec(blk, lambda h, c, sl: (0, c, base + h))
    prev = lambda base: pl.BlockSpec(blk, lambda h, c, sl: (0, jnp.maximum(c - 1, 0), base + h))
    return pl.pallas_call(
        _attn_kernel,
        out_shape=jax.ShapeDtypeStruct((RES, na, attn_w), BF16),
        grid_spec=pltpu.PrefetchScalarGridSpec(
            num_scalar_prefetch=1,
            grid=(n_heads, na // BAND),
            in_specs=[cur(qb), prev(kb), cur(kb), prev(vb), cur(vb)],
            out_specs=pl.BlockSpec(blk, lambda h, c, sl: (0, c, h)),
            scratch_shapes=[pltpu.VMEM((len(PATTERNS), 2, BAND, 2 * BAND), F32),
                            pltpu.VMEM(blk, F32), pltpu.VMEM(blk, F32), pltpu.VMEM(blk, F32)]),
        compiler_params=_cparams(("arbitrary", "arbitrary")),
        name="attn",
    )(slopes, z3, z3, z3, z3, z3)


def _outproj_kernel(yc_ref, ya_ref, ga_ref, w_ref, x_ref, gt_ref, o_ref, y_ref):
    _, ta, cc = yc_ref.shape

    @pl.when(pl.program_id(1) == 0)
    def _():
        n = RES * ta
        ya = ya_ref[...].reshape(n, ya_ref.shape[2]).astype(F32)
        ms = jnp.mean(ya * ya, axis=-1, keepdims=True)
        yn = (ya * lax.rsqrt(ms + EPS) * ga_ref[...]).astype(BF16)
        perm = _row_permutation(ta, False)
        y_ref[:, 0:cc] = jnp.dot(perm, yc_ref[...].reshape(n, cc), preferred_element_type=F32).astype(BF16)
        y_ref[:, cc:] = jnp.dot(perm, yn, preferred_element_type=F32).astype(BF16)

    acc = jnp.dot(y_ref[...], w_ref[...], preferred_element_type=F32)
    o_ref[...] = x_ref[...] + gt_ref[...] * acc


def _outproj(y_conv3, y_attn3, g_attn, w_out_bf, x2d, gt1):
    s, d = x2d.shape
    na = s // RES
    cc, aw = y_conv3.shape[2], y_attn3.shape[2]
    ta = min(32, na)
    tn = min(1024, d)
    return pl.pallas_call(
        _outproj_kernel,
        out_shape=jax.ShapeDtypeStruct((s, d), F32),
        grid=(na // ta, d // tn),
        in_specs=[pl.BlockSpec((RES, ta, cc), lambda i, j: (0, i, 0)),
                  pl.BlockSpec((RES, ta, aw), lambda i, j: (0, i, 0)),
                  pl.BlockSpec((1, aw), lambda i, j: (0, 0)),
                  pl.BlockSpec((cc + aw, tn), lambda i, j: (0, j)),
                  pl.BlockSpec((RES * ta, tn), lambda i, j: (i, j)),
                  pl.BlockSpec((1, tn), lambda i, j: (0, j))],
        out_specs=pl.BlockSpec((RES * ta, tn), lambda i, j: (i, j)),
        scratch_shapes=[pltpu.VMEM((RES * ta, cc + aw), BF16)],
        compiler_params=_cparams(("arbitrary", "arbitrary")),
        name="outproj",
    )(y_conv3, y_attn3, g_attn, w_out_bf, x2d, gt1)


def _router_kernel(x_ref, g_ref, sc_ref, sh_ref, wr_ref, h_ref, eid_ref, wt_ref):
    h = _rms_mod(x_ref[...], g_ref[...], sc_ref[...], sh_ref[...])
    h_ref[...] = h
    w = wr_ref[...]
    h_hi, w_hi = h.astype(BF16), w.astype(BF16)
    h_lo = (h - h_hi.astype(F32)).astype(BF16)
    w_lo = (w - w_hi.astype(F32)).astype(BF16)
    logits = (jnp.dot(h_hi, w_hi, preferred_element_type=F32) + jnp.dot(h_hi, w_lo, preferred_element_type=F32)
              + jnp.dot(h_lo, w_hi, preferred_element_type=F32))
    tm = logits.shape[0]
    lane = lax.broadcasted_iota(jnp.int32, (tm, LANES), 1).astype(F32)
    big = float(LANES)
    is_g = lane < N_GROUPS
    gl = jnp.where(is_g, logits, NEG_INF)
    gmax = jnp.max(gl, axis=-1, keepdims=True)
    gsel = jnp.min(jnp.where(gl == gmax, lane, big), axis=-1, keepdims=True)
    g_w = 1.0 / jnp.sum(jnp.where(is_g, jnp.exp(gl - gmax), 0.0), axis=-1, keepdims=True)
    lo = N_GROUPS + gsel * EXPERTS_PER_GROUP
    in_grp = (lane >= lo) & (lane < lo + EXPERTS_PER_GROUP)
    el = jnp.where(in_grp, logits, NEG_INF)
    e1 = jnp.max(el, axis=-1, keepdims=True)
    i1 = jnp.min(jnp.where(el == e1, lane, big), axis=-1, keepdims=True)
    el2 = jnp.where(lane == i1, NEG_INF, el)
    e2 = jnp.max(el2, axis=-1, keepdims=True)
    i2 = jnp.min(jnp.where(el2 == e2, lane, big), axis=-1, keepdims=True)
    t = jnp.exp(e2 - e1)
    w1 = g_w / (1.0 + t)
    w2 = g_w * t / (1.0 + t)
    eid = jnp.where(lane == 0.0, i1 - N_GROUPS, jnp.where(lane == 1.0, i2 - N_GROUPS, 0.0))
    eid_ref[...] = eid.astype(jnp.int32)
    wt_ref[...] = jnp.where(lane == 0.0, w1, jnp.where(lane == 1.0, w2, 0.0))


def _router(x2, g_ffn, sc2, sh2, w_r):
    s, d = x2.shape
    tm = min(512, s)
    vec = lambda: pl.BlockSpec((1, d), lambda i: (0, 0))
    lanes_out = lambda: pl.BlockSpec((tm, LANES), lambda i: (i, 0))
    return pl.pallas_call(
        _router_kernel,
        out_shape=(jax.ShapeDtypeStruct((s, d), F32),
                   jax.ShapeDtypeStruct((s, LANES), jnp.int32),
                   jax.ShapeDtypeStruct((s, LANES), F32)),
        grid=(s // tm,),
        in_specs=[pl.BlockSpec((tm, d), lambda i: (i, 0)), vec(), vec(), vec(),
                  pl.BlockSpec((d, LANES), lambda i: (0, 0))],
        out_specs=(pl.BlockSpec((tm, d), lambda i: (i, 0)), lanes_out(), lanes_out()),
        compiler_params=_cparams(("arbitrary",)),
        name="router",
    )(x2, g_ffn, sc2, sh2, w_r)


def _rank_kernel(eid_ref, rank_ref, cnt_ref, run_ref):
    @pl.when(pl.program_id(0) == 0)
    def _():
        run_ref[...] = jnp.zeros_like(run_ref)

    tb = eid_ref.shape[0]
    lane = lax.broadcasted_iota(jnp.int32, (tb, LANES), 1)
    row = lax.broadcasted_iota(jnp.int32, (tb, tb), 0)
    col = lax.broadcasted_iota(jnp.int32, (tb, tb), 1)
    earlier = (col < row).astype(BF16)
    e = eid_ref[...]
    run = run_ref[...]
    ranks = []
    for k in range(2):
        onehot = e[:, k:k + 1] == lane
        oh = onehot.astype(F32)
        before = jnp.dot(earlier, oh.astype(BF16), preferred_element_type=F32) + run
        ranks.append(jnp.sum(jnp.where(onehot, before, 0.0), axis=-1, keepdims=True))
        run = run + jnp.sum(oh, axis=0, keepdims=True)
    rank_ref[...] = jnp.where(lane == 0, ranks[0], jnp.where(lane == 1, ranks[1], 0.0)).astype(jnp.int32)
    run_ref[...] = run
    cnt_ref[...] = run


def _rank(eid):
    s = eid.shape[0]
    tb = min(512, s)
    return pl.pallas_call(
        _rank_kernel,
        out_shape=(jax.ShapeDtypeStruct((s, LANES), jnp.int32), jax.ShapeDtypeStruct((1, LANES), F32)),
        grid=(s // tb,),
        in_specs=[pl.BlockSpec((tb, LANES), lambda i: (i, 0))],
        out_specs=(pl.BlockSpec((tb, LANES), lambda i: (i, 0)), pl.BlockSpec((1, LANES), lambda i: (0, 0))),
        scratch_shapes=[pltpu.VMEM((1, LANES), F32)],
        compiler_params=_cparams(("arbitrary",)),
        name="rank",
    )(eid)


def _plan_kernel(eid_ref, rank_ref, cnt_ref, dest_ref, src_ref, blke_ref, blkc_ref, nused_ref, pst_ref,
                 *, blk, n_blk):
    def per_expert(e, nb):
        c = cnt_ref[e]
        pst_ref[e] = nb * blk
        nbe = (c + (blk - 1)) // blk

        def per_block(jb, carry):
            blke_ref[nb + jb] = e
            blkc_ref[nb + jb] = jnp.minimum(blk, c - jb * blk)
            return carry

        lax.fori_loop(0, nbe, per_block, 0)

        def pad_slot(r, carry):
            src_ref[nb * blk + r] = 0
            return carry

        lax.fori_loop(c, (c + (GATHER_GROUP - 1)) // GATHER_GROUP * GATHER_GROUP, pad_slot, 0)
        return nb + nbe

    n_used = lax.fori_loop(0, N_EXPERTS, per_expert, jnp.int32(0))
    nused_ref[0] = n_used
    last_e = blke_ref[n_used - 1]

    def unused(i, carry):
        blke_ref[i] = last_e
        blkc_ref[i] = 0
        return carry

    lax.fori_loop(n_used, n_blk, unused, 0)

    n_tok = eid_ref.shape[0] // 2

    def place(a, carry):
        dst = pst_ref[eid_ref[a]] + rank_ref[a]
        dest_ref[a] = dst
        src_ref[dst] = jnp.where(a >= n_tok, a - n_tok, a)
        return carry

    lax.fori_loop(0, eid_ref.shape[0], place, 0, unroll=8)


def _plan(eid_flat, rank_flat, counts, blk):
    a = eid_flat.shape[0]
    n_blk = a // blk + N_EXPERTS
    smem = lambda: pl.BlockSpec(memory_space=pltpu.SMEM)
    i32 = jnp.int32
    return pl.pallas_call(
        functools.partial(_plan_kernel, blk=blk, n_blk=n_blk),
        out_shape=(jax.ShapeDtypeStruct((a,), i32), jax.ShapeDtypeStruct((n_blk * blk,), i32),
                   jax.ShapeDtypeStruct((n_blk,), i32), jax.ShapeDtypeStruct((n_blk,), i32),
                   jax.ShapeDtypeStruct((1,), i32)),
        in_specs=[smem(), smem(), smem()],
        out_specs=(smem(), smem(), smem(), smem(), smem()),
        scratch_shapes=[pltpu.SMEM((N_EXPERTS,), i32)],
        name="plan",
    )(eid_flat, rank_flat, counts)


def _gather_rows(idx_of_row, n_rows, src_hbm, dst, sem):
    def body(g, carry):
        for rr in range(GATHER_GROUP):
            r = g * GATHER_GROUP + rr
            pltpu.make_async_copy(src_hbm.at[pl.ds(idx_of_row(r), 1), :], dst.at[pl.ds(r, 1), :], sem).start()
        return carry
    lax.fori_loop(0, (n_rows + (GATHER_GROUP - 1)) // GATHER_GROUP, body, 0)


def _wait_rows(n_rows, src_hbm, dst, sem):
    def body(g, carry):
        pltpu.make_async_copy(src_hbm.at[pl.ds(0, GATHER_GROUP), :], dst.at[pl.ds(0, GATHER_GROUP), :], sem).wait()
        return carry
    lax.fori_loop(0, (n_rows + (GATHER_GROUP - 1)) // GATHER_GROUP, body, 0)


def _expert_kernel(blke_ref, blkc_ref, nused_ref, src_ref, h_hbm, w1_hbm, w3_hbm, w2_hbm, o_ref,
                   xbuf, w1b, w3b, w2b, gsem, wsem, *, blk, kc):
    i = pl.program_id(0)
    last = pl.num_programs(0) - 1
    n_used = nused_ref[0]
    e = blke_ref[i]
    e_next = blke_ref[jnp.minimum(i + 1, last)]
    first_of_expert = (i == 0) | (blke_ref[jnp.maximum(i - 1, 0)] != e)
    fetch_next = (i + 1 < n_used) & (e_next != e)
    wbufs = ((w1_hbm, w1b), (w3_hbm, w3b), (w2_hbm, w2b))

    def weight_copy(k, ee):
        return pltpu.make_async_copy(wbufs[k][0].at[ee], wbufs[k][1], wsem.at[k])

    def issue_rows(b, slot):
        _gather_rows(lambda r: src_ref[b * blk + r], blkc_ref[b], h_hbm, xbuf.at[slot], gsem.at[slot])

    def weights_ready(k):
        @pl.when(first_of_expert)
        def _():
            weight_copy(k, e).wait()

    def weights_done(k):
        @pl.when(fetch_next)
        def _():
            weight_copy(k, e_next).start()

    slot = lax.rem(i, 2)

    @pl.when(i == 0)
    def _():
        for k in range(3):
            weight_copy(k, e).start()
        xbuf[...] = jnp.zeros_like(xbuf)
        issue_rows(0, 0)

    @pl.when(i < n_used)
    def _():
        _wait_rows(blkc_ref[i], h_hbm, xbuf.at[slot], gsem.at[slot])

        @pl.when(i + 1 < n_used)
        def _():
            issue_rows(i + 1, 1 - slot)

        d = xbuf.shape[2]

        def up_proj(wb):
            acc = None
            for k0 in range(0, d, kc):
                part = jnp.dot(xbuf[slot, :, k0:k0 + kc].astype(BF16), wb[k0:k0 + kc, :].astype(BF16),
                               preferred_element_type=F32)
                acc = part if acc is None else acc + part
            return acc

        weights_ready(0)
        a = up_proj(w1b)
        weights_done(0)
        weights_ready(1)
        b = up_proj(w3b)
        weights_done(1)
        hid = ((a * jax.nn.sigmoid(a)) * b).astype(BF16)
        weights_ready(2)
        for t in range(0, d, Y_TILE):
            o_ref[:, t:t + Y_TILE] = jnp.dot(hid, w2b[:, t:t + Y_TILE].astype(BF16), preferred_element_type=F32)
        weights_done(2)


def _experts(h2, w1, w3, w2, blk_e, blk_cnt, n_used, row_src, blk):
    s, d = h2.shape
    de = w1.shape[-1]
    kc = min(1024, d)
    assert d % kc == 0 and d % Y_TILE == 0 and blk % GATHER_GROUP == 0
    n_blk = blk_e.shape[0]
    anyspace = lambda: pl.BlockSpec(memory_space=pl.ANY)
    return pl.pallas_call(
        functools.partial(_expert_kernel, blk=blk, kc=kc),
        out_shape=jax.ShapeDtypeStruct((n_blk * blk, d), F32),
        grid_spec=pltpu.PrefetchScalarGridSpec(
            num_scalar_prefetch=4,
            grid=(n_blk,),
            in_specs=[anyspace(), anyspace(), anyspace(), anyspace()],
            out_specs=pl.BlockSpec((blk, d), lambda i, be, bc, nu, rs: (jnp.minimum(i, nu[0] - 1), 0)),
            scratch_shapes=[pltpu.VMEM((2, blk, d), F32),
                            pltpu.VMEM((d, de), F32), pltpu.VMEM((d, de), F32), pltpu.VMEM((de, d), F32),
                            pltpu.SemaphoreType.DMA((2,)), pltpu.SemaphoreType.DMA((3,))]),
        compiler_params=_cparams(("arbitrary",)),
        name="experts",
    )(blk_e, blk_cnt, n_used, row_src, h2, w1, w3, w2)


def _combine_kernel(dest_ref, ys_hbm, x_ref, wt_ref, gt_ref, o_ref, ybuf, sem):
    i = pl.program_id(0)
    nb = pl.num_programs(0)
    tm = x_ref.shape[0]
    n_tok = tm * nb

    def issue(blk_i, slot):
        for k in range(2):
            _gather_rows(lambda r: dest_ref[k * n_tok + blk_i * tm + r], tm, ys_hbm, ybuf.at[slot, k],
                         sem.at[slot])

    slot = lax.rem(i, 2)

    @pl.when(i == 0)
    def _():
        issue(0, 0)

    @pl.when(i + 1 < nb)
    def _():
        issue(i + 1, 1 - slot)

    for k in range(2):
        _wait_rows(tm, ys_hbm, ybuf.at[slot, k], sem.at[slot])
    w = wt_ref[...]
    y = ybuf[slot, 0] * w[:, 0:1] + ybuf[slot, 1] * w[:, 1:2]
    o_ref[...] = x_ref[...] + gt_ref[...] * y


def _combine(dest, ys, x2, wts, gt2):
    s, d = x2.shape
    tm = min(256, s)
    return pl.pallas_call(
        _combine_kernel,
        out_shape=jax.ShapeDtypeStruct((s, d), F32),
        grid_spec=pltpu.PrefetchScalarGridSpec(
            num_scalar_prefetch=1,
            grid=(s // tm,),
            in_specs=[pl.BlockSpec(memory_space=pl.ANY),
                      pl.BlockSpec((tm, d), lambda i, ds_: (i, 0)),
                      pl.BlockSpec((tm, LANES), lambda i, ds_: (i, 0)),
                      pl.BlockSpec((1, d), lambda i, ds_: (0, 0))],
            out_specs=pl.BlockSpec((tm, d), lambda i, ds_: (i, 0)),
            scratch_shapes=[pltpu.VMEM((2, 2, tm, d), F32), pltpu.SemaphoreType.DMA((2,))]),
        compiler_params=_cparams(("arbitrary",)),
        name="combine",
    )(dest, ys, x2, wts, gt2)


def _layer(x2d, c, w_ada, b_ada, g_mix, g_ffn, w_in, conv_w, q_norm_g, k_norm_g, g_branch, w_out,
           w_group, w_expert, w1, w3, w2):
    s, d = x2d.shape
    conv_ch = conv_w.shape[1]
    attn_w = d - conv_ch
    n_heads = attn_w // HEAD_DIM
    na = s // RES
    assert s % (RES * BAND) == 0
    row = lambda v: v.reshape(1, -1)

    mod = _ada(c, w_ada, b_ada)
    sh1, sc1, gt1, sh2, sc2, gt2 = [mod[:, i * d:(i + 1) * d] for i in range(6)]

    h1 = _norm1(x2d, row(g_mix), sc1, sh1)
    z = _inproj(h1.reshape(s, d), w_in, row(q_norm_g), row(k_norm_g), conv_ch, attn_w)
    z3 = z.reshape(RES, na, z.shape[1])
    y_conv = _conv(z3, conv_w, row(g_branch[:conv_ch]), conv_ch)
    slopes = jnp.exp2(-ALIBI_MAX_EXP * jnp.arange(1, n_heads + 1, dtype=F32) / n_heads)
    y_attn = _attention(z3, slopes, conv_ch, attn_w)
    x2 = _outproj(y_conv, y_attn, row(g_branch[conv_ch:]), w_out.astype(BF16), x2d, gt1)

    w_r = jnp.concatenate([w_group, w_expert,
                           jnp.zeros((d, LANES - N_GROUPS - N_EXPERTS), F32)], axis=1)
    h2, eid, wts = _router(x2, row(g_ffn), sc2, sh2, w_r)
    rank, counts = _rank(eid)
    eid_flat = eid[:, :2].T.reshape(-1)
    rank_flat = rank[:, :2].T.reshape(-1)
    cnt = counts[0, :N_EXPERTS].astype(jnp.int32)
    dest, row_src, blk_e, blk_cnt, n_used = _plan(eid_flat, rank_flat, cnt, MOE_BLOCK)
    ys = _experts(h2, w1, w3, w2, blk_e, blk_cnt, n_used, row_src, MOE_BLOCK)
    return _combine(dest, ys, x2, wts, gt2)


def kernel(x, c, w_ada, b_ada, g_mix, g_ffn, w_in, conv_w, q_norm_g, k_norm_g, g_branch, w_out,
           w_group, w_expert, w1, w3, w2):
    b, s, d = x.shape
    assert b == 1 and c.shape[0] == 1
    for l in range(w_ada.shape[0]):
        x = _layer(x.reshape(s, d), c[0], w_ada[l], b_ada[l], g_mix[l], g_ffn[l], w_in[l], conv_w[l],
                   q_norm_g[l], k_norm_g[l], g_branch[l], w_out[l], w_group[l], w_expert[l],
                   w1[l], w3[l], w2[l]).reshape(b, s, d)
    return x
```

```python
import functools

import jax
import jax.numpy as jnp
from jax import lax
from jax.experimental import pallas as pl
from jax.experimental.pallas import tpu as pltpu

EPS = 1e-6
HEAD_DIM = 128
BAND = 128
PATTERNS = ((128, 1), (512, 4), (2048, 16))
RES = 16
ALIBI_MAX_EXP = 8.0
N_GROUPS = 8
EXPERTS_PER_GROUP = 8
N_EXPERTS = N_GROUPS * EXPERTS_PER_GROUP
MOE_BLOCK = 256
GATHER_GROUP = 8
Y_TILE = 512
LANES = 128
SUBLANES = 8
VMEM_LIMIT = 56 * 1024 * 1024
F32 = jnp.float32
BF16 = jnp.bfloat16
NEG_INF = float("-inf")


def _cparams(sem):
    return pltpu.CompilerParams(dimension_semantics=sem, vmem_limit_bytes=VMEM_LIMIT)


def _ada_kernel(c_ref, w_ref, b_ref, o_ref, s_ref, *, rows):
    @pl.when(pl.program_id(0) == 0)
    def _():
        c = c_ref[...]
        s_ref[...] = c * jax.nn.sigmoid(c)

    d, tn = w_ref.shape

    def body(i, acc):
        r0 = pl.multiple_of(i * rows, rows)
        p = w_ref[pl.ds(r0, rows), :] * s_ref[pl.ds(r0, rows), :]
        return acc + jnp.sum(p.reshape(rows // SUBLANES, SUBLANES, tn), axis=0)

    acc = lax.fori_loop(0, d // rows, body, jnp.zeros((SUBLANES, tn), F32))
    o_ref[...] = jnp.sum(acc, axis=0, keepdims=True) + b_ref[...]


def _ada(c, w_ada, b_ada):
    d, n = w_ada.shape
    tn = min(512, n)
    rows = min(512, d)
    return pl.pallas_call(
        functools.partial(_ada_kernel, rows=rows),
        out_shape=jax.ShapeDtypeStruct((1, n), F32),
        grid=(n // tn,),
        in_specs=[pl.BlockSpec((d, 1), lambda j: (0, 0)),
                  pl.BlockSpec((d, tn), lambda j: (0, j)),
                  pl.BlockSpec((1, tn), lambda j: (0, j))],
        out_specs=pl.BlockSpec((1, tn), lambda j: (0, j)),
        scratch_shapes=[pltpu.VMEM((d, 1), F32)],
        compiler_params=_cparams(("arbitrary",)),
        name="ada",
    )(c.reshape(d, 1), w_ada, b_ada.reshape(1, n))


def _rms_mod(x, g, sc, sh):
    ms = jnp.mean(x * x, axis=-1, keepdims=True)
    return (x * lax.rsqrt(ms + EPS) * g) * (1.0 + sc) + sh


def _row_permutation(ta, to_residue_major):
    n = RES * ta
    assert ta & (ta - 1) == 0
    sh = ta.bit_length() - 1
    r = lax.broadcasted_iota(jnp.int32, (n, n), 0)
    c = lax.broadcasted_iota(jnp.int32, (n, n), 1)
    res_major, natural = (r, c) if to_residue_major else (c, r)
    return (natural == RES * (res_major & (ta - 1)) + (res_major >> sh)).astype(BF16)


def _norm1_kernel(x_ref, g_ref, sc_ref, sh_ref, o_ref):
    ta = o_ref.shape[1]
    h = _rms_mod(x_ref[...], g_ref[...], sc_ref[...], sh_ref[...]).astype(BF16)
    hp = jnp.dot(_row_permutation(ta, True), h, preferred_element_type=F32)
    o_ref[...] = hp.astype(BF16).reshape(o_ref.shape)


def _norm1(x2d, g_mix, sc1, sh1):
    s, d = x2d.shape
    na = s // RES
    ta = min(32, na)
    vec = lambda: pl.BlockSpec((1, d), lambda i: (0, 0))
    return pl.pallas_call(
        _norm1_kernel,
        out_shape=jax.ShapeDtypeStruct((RES, na, d), BF16),
        grid=(na // ta,),
        in_specs=[pl.BlockSpec((RES * ta, d), lambda i: (i, 0)), vec(), vec(), vec()],
        out_specs=pl.BlockSpec((RES, ta, d), lambda i: (0, i, 0)),
        compiler_params=_cparams(("arbitrary",)),
        name="norm1",
    )(x2d, g_mix, sc1, sh1)


def _inproj_kernel(h_ref, w_ref, qg_ref, kg_ref, z_ref, wb_ref, *, q0, k0, v0):
    j = pl.program_id(0)

    @pl.when(pl.program_id(1) == 0)
    def _():
        wb_ref[...] = w_ref[...].astype(BF16)

    acc = jnp.dot(h_ref[...], wb_ref[...], preferred_element_type=F32)
    is_q = (j >= q0) & (j < k0)
    is_k = (j >= k0) & (j < v0)

    g = jnp.where(is_q, qg_ref[...] * (HEAD_DIM ** -0.5), kg_ref[...])
    for hh in range(acc.shape[1] // HEAD_DIM):
        a = acc[:, hh * HEAD_DIM:(hh + 1) * HEAD_DIM]
        ms = jnp.mean(a * a, axis=-1, keepdims=True)
        scale = jnp.where(is_q | is_k, lax.rsqrt(ms + EPS) * g, 1.0)
        z_ref[:, hh * HEAD_DIM:(hh + 1) * HEAD_DIM] = a * scale


def _inproj(h1, w_in, q_g, k_g, conv_ch, attn_w):
    s, d = h1.shape
    n = w_in.shape[1]
    tm = min(1024, s)
    tn = min(512, conv_ch)
    assert conv_ch % tn == 0 and attn_w % tn == 0 and tn % HEAD_DIM == 0
    q0 = 3 * conv_ch // tn
    k0 = q0 + attn_w // tn
    v0 = k0 + attn_w // tn
    hvec = lambda: pl.BlockSpec((1, HEAD_DIM), lambda j, i: (0, 0))
    return pl.pallas_call(
        functools.partial(_inproj_kernel, q0=q0, k0=k0, v0=v0),
        out_shape=jax.ShapeDtypeStruct((s, n), F32),
        grid=(n // tn, s // tm),
        in_specs=[pl.BlockSpec((tm, d), lambda j, i: (i, 0)),
                  pl.BlockSpec((d, tn), lambda j, i: (0, j)), hvec(), hvec()],
        out_specs=pl.BlockSpec((tm, tn), lambda j, i: (i, j)),
        scratch_shapes=[pltpu.VMEM((d, tn), BF16)],
        compiler_params=_cparams(("arbitrary", "arbitrary")),
        name="inproj",
    )(h1, w_in, q_g, k_g)


def _conv_kernel(b_ref, c_ref, v_ref, cw_ref, g_ref, o_ref, ue_ref):
    i = pl.program_id(0)
    ta = b_ref.shape[1]
    w = cw_ref[...]
    assert w.shape[0] == 3
    for k in range(2):
        @pl.when(i == 0)
        def _():
            ue_ref[k, 0:SUBLANES, :] = jnp.zeros((SUBLANES, ue_ref.shape[2]), F32)

        @pl.when(i > 0)
        def _():
            ue_ref[k, 0:SUBLANES, :] = ue_ref[k, ta:ta + SUBLANES, :]

        r = RES - 2 + k
        ue_ref[k, SUBLANES:SUBLANES + ta, :] = c_ref[r] * v_ref[r]

    def u_at(bb):
        if bb >= 0:
            return c_ref[bb] * v_ref[bb]
        return ue_ref[2 + bb, SUBLANES - 1:SUBLANES - 1 + ta, :]

    for b in range(RES):
        y = b_ref[b] * (w[0:1, :] * u_at(b - 2) + w[1:2, :] * u_at(b - 1) + w[2:3, :] * u_at(b))
        ms = jnp.mean(y * y, axis=-1, keepdims=True)
        o_ref[b] = (y * lax.rsqrt(ms + EPS) * g_ref[...]).astype(BF16)


def _conv(z3, conv_w, g_conv, conv_ch):
    na = z3.shape[1]
    ta = min(64, na)
    blk = lambda jj: pl.BlockSpec((RES, ta, conv_ch), lambda i: (0, i, jj))
    return pl.pallas_call(
        _conv_kernel,
        out_shape=jax.ShapeDtypeStruct((RES, na, conv_ch), BF16),
        grid=(na // ta,),
        in_specs=[blk(0), blk(1), blk(2),
                  pl.BlockSpec(conv_w.shape, lambda i: (0, 0)),
                  pl.BlockSpec((1, conv_ch), lambda i: (0, 0))],
        out_specs=pl.BlockSpec((RES, ta, conv_ch), lambda i: (0, i, 0)),
        scratch_shapes=[pltpu.VMEM((2, ta + SUBLANES, conv_ch), F32)],
        compiler_params=_cparams(("arbitrary",)),
        name="conv",
    )(z3, z3, z3, conv_w, g_conv)


def _attn_kernel(slopes_ref, q_ref, kp_ref, kc_ref, vp_ref, vc_ref, o_ref, bias_ref, m_ref, l_ref, acc_ref):
    h = pl.program_id(0)
    cidx = pl.program_id(1)
    slope = slopes_ref[h]
    first_pen = jnp.where(cidx > 0, 0.0, NEG_INF).astype(F32)
    dn_t = (((1,), (1,)), ((), ()))

    j = lax.broadcasted_iota(jnp.int32, (BAND, 2 * BAND), 0)
    u = lax.broadcasted_iota(jnp.int32, (BAND, 2 * BAND), 1)
    is_prev = u < BAND
    uu = jnp.where(is_prev, u, u - BAND)
    for pi, (window, dil) in enumerate(PATTERNS):
        assert window // dil == BAND and RES % dil == 0
        ns, sr = RES // dil, BAND * dil // RES
        assert sr & (sr - 1) == 0
        sh = sr.bit_length() - 1
        mq = (j & (sr - 1)) * ns + (j >> sh)
        mk = (uu & (sr - 1)) * ns + (uu >> sh)
        steps = mq - mk + jnp.where(is_prev, BAND, 0)
        valid = (is_prev & (steps <= BAND)) | (jnp.logical_not(is_prev) & (steps >= 0))
        bias = jnp.where(valid, -slope * (steps * dil).astype(F32), NEG_INF)
        bias_ref[pi, 0] = bias
        bias_ref[pi, 1] = bias + jnp.where(is_prev, first_pen, 0.0)

    for pi, (window, dil) in enumerate(PATTERNS):
        ns, sr = RES // dil, BAND * dil // RES
        n_chunks = BAND // sr
        for n in range(n_chunks):
            for r in range(dil):
                def gather(ref, nn):
                    return jnp.concatenate([ref[dil * c + r, nn * sr:(nn + 1) * sr, :] for c in range(ns)], axis=0)

                def scatter(ref, val):
                    for c in range(ns):
                        ref[dil * c + r, n * sr:(n + 1) * sr, :] = val[c * sr:(c + 1) * sr, :]

                q = gather(q_ref, n).astype(BF16)
                if n > 0:
                    k_prev, v_prev = gather(kc_ref, n - 1), gather(vc_ref, n - 1)
                else:
                    k_prev, v_prev = gather(kp_ref, n_chunks - 1), gather(vp_ref, n_chunks - 1)
                kcat = jnp.concatenate([k_prev, gather(kc_ref, n)], axis=0).astype(BF16)
                vcat = jnp.concatenate([v_prev, gather(vc_ref, n)], axis=0).astype(BF16)
                s = lax.dot_general(q, kcat, dn_t, preferred_element_type=F32) + bias_ref[pi, 1 if n == 0 else 0]
                m = jnp.max(s, axis=-1, keepdims=True)
                p = jnp.exp(s - m)
                l = jnp.sum(p, axis=-1, keepdims=True)
                o = jnp.dot(p.astype(BF16), vcat, preferred_element_type=F32)
                m_b = jnp.broadcast_to(m, (BAND, HEAD_DIM))
                l_b = jnp.broadcast_to(l, (BAND, HEAD_DIM))
                if pi == 0:
                    scatter(m_ref, m_b)
                    scatter(l_ref, l_b)
                    scatter(acc_ref, o)
                else:
                    m_old = gather(m_ref, n)
                    m_new = jnp.maximum(m_old, m_b)
                    a_old = jnp.exp(m_old - m_new)
                    a_new = jnp.exp(m_b - m_new)
                    l_new = a_old * gather(l_ref, n) + a_new * l_b
                    o_new = a_old * gather(acc_ref, n) + a_new * o
                    if pi == len(PATTERNS) - 1:
                        scatter(acc_ref, o_new / l_new)
                    else:
                        scatter(m_ref, m_new)
                        scatter(l_ref, l_new)
                        scatter(acc_ref, o_new)
    o_ref[...] = acc_ref[...].astype(BF16)


def _attention(z3, slopes, conv_ch, attn_w):
    na = z3.shape[1]
    n_heads = attn_w // HEAD_DIM
    assert na % BAND == 0
    qb = 3 * conv_ch // HEAD_DIM
    kb = qb + n_heads
    vb = kb + n_heads
    blk = (RES, BAND, HEAD_DIM)
    cur = lambda base: pl.BlockSpec(blk, lambda h, c, sl: (0, c, base + h))
    prev = lambda base: pl.BlockSpec(blk, lambda h, c, sl: (0, jnp.maximum(c - 1, 0), base + h))
    return pl.pallas_call(
        _attn_kernel,
        out_shape=jax.ShapeDtypeStruct((RES, na, attn_w), BF16),
        grid_spec=pltpu.PrefetchScalarGridSpec(
            num_scalar_prefetch=1,
            grid=(n_heads, na // BAND),
            in_specs=[cur(qb), prev(kb), cur(kb), prev(vb), cur(vb)],
            out_specs=pl.BlockSpec(blk, lambda h, c, sl: (0, c, h)),
            scratch_shapes=[pltpu.VMEM((len(PATTERNS), 2, BAND, 2 * BAND), F32),
                            pltpu.VMEM(blk, F32), pltpu.VMEM(blk, F32), pltpu.VMEM(blk, F32)]),
        compiler_params=_cparams(("arbitrary", "arbitrary")),
        name="attn",
    )(slopes, z3, z3, z3, z3, z3)


def _outproj_kernel(yc_ref, ya_ref, ga_ref, w_ref, x_ref, gt_ref, o_ref, y_ref):
    _, ta, cc = yc_ref.shape

    @pl.when(pl.program_id(1) == 0)
    def _():
        n = RES * ta
        ya = ya_ref[...].reshape(n, ya_ref.shape[2]).astype(F32)
        ms = jnp.mean(ya * ya, axis=-1, keepdims=True)
        yn = (ya * lax.rsqrt(ms + EPS) * ga_ref[...]).astype(BF16)
        perm = _row_permutation(ta, False)
        y_ref[:, 0:cc] = jnp.dot(perm, yc_ref[...].reshape(n, cc), preferred_element_type=F32).astype(BF16)
        y_ref[:, cc:] = jnp.dot(perm, yn, preferred_element_type=F32).astype(BF16)

    acc = jnp.dot(y_ref[...], w_ref[...], preferred_element_type=F32)
    o_ref[...] = x_ref[...] + gt_ref[...] * acc


def _outproj(y_conv3, y_attn3, g_attn, w_out_bf, x2d, gt1):
    s, d = x2d.shape
    na = s // RES
    cc, aw = y_conv3.shape[2], y_attn3.shape[2]
    ta = min(32, na)
    tn = min(1024, d)
    return pl.pallas_call(
        _outproj_kernel,
        out_shape=jax.ShapeDtypeStruct((s, d), F32),
        grid=(na // ta, d // tn),
        in_specs=[pl.BlockSpec((RES, ta, cc), lambda i, j: (0, i, 0)),
                  pl.BlockSpec((RES, ta, aw), lambda i, j: (0, i, 0)),
                  pl.BlockSpec((1, aw), lambda i, j: (0, 0)),
                  pl.BlockSpec((cc + aw, tn), lambda i, j: (0, j)),
                  pl.BlockSpec((RES * ta, tn), lambda i, j: (i, j)),
                  pl.BlockSpec((1, tn), lambda i, j: (0, j))],
        out_specs=pl.BlockSpec((RES * ta, tn), lambda i, j: (i, j)),
        scratch_shapes=[pltpu.VMEM((RES * ta, cc + aw), BF16)],
        compiler_params=_cparams(("arbitrary", "arbitrary")),
        name="outproj",
    )(y_conv3, y_attn3, g_attn, w_out_bf, x2d, gt1)


def _router_kernel(x_ref, g_ref, sc_ref, sh_ref, wr_ref, h_ref, eid_ref, wt_ref):
    h = _rms_mod(x_ref[...], g_ref[...], sc_ref[...], sh_ref[...])
    h_ref[...] = h
    w = wr_ref[...]
    h_hi, w_hi = h.astype(BF16), w.astype(BF16)
    h_lo = (h - h_hi.astype(F32)).astype(BF16)
    w_lo = (w - w_hi.astype(F32)).astype(BF16)
    logits = (jnp.dot(h_hi, w_hi, preferred_element_type=F32) + jnp.dot(h_hi, w_lo, preferred_element_type=F32)
              + jnp.dot(h_lo, w_hi, preferred_element_type=F32))
    tm = logits.shape[0]
    lane = lax.broadcasted_iota(jnp.int32, (tm, LANES), 1).astype(F32)
    big = float(LANES)
    is_g = lane < N_GROUPS
    gl = jnp.where(is_g, logits, NEG_INF)
    gmax = jnp.max(gl, axis=-1, keepdims=True)
    gsel = jnp.min(jnp.where(gl == gmax, lane, big), axis=-1, keepdims=True)
    g_w = 1.0 / jnp.sum(jnp.where(is_g, jnp.exp(gl - gmax), 0.0), axis=-1, keepdims=True)
    lo = N_GROUPS + gsel * EXPERTS_PER_GROUP
    in_grp = (lane >= lo) & (lane < lo + EXPERTS_PER_GROUP)
    el = jnp.where(in_grp, logits, NEG_INF)
    e1 = jnp.max(el, axis=-1, keepdims=True)
    i1 = jnp.min(jnp.where(el == e1, lane, big), axis=-1, keepdims=True)
    el2 = jnp.where(lane == i1, NEG_INF, el)
    e2 = jnp.max(el2, axis=-1, keepdims=True)
    i2 = jnp.min(jnp.where(el2 == e2, lane, big), axis=-1, keepdims=True)
    t = jnp.exp(e2 - e1)
    w1 = g_w / (1.0 + t)
    w2 = g_w * t / (1.0 + t)
    eid = jnp.where(lane == 0.0, i1 - N_GROUPS, jnp.where(lane == 1.0, i2 - N_GROUPS, 0.0))
    eid_ref[...] = eid.astype(jnp.int32)
    wt_ref[...] = jnp.where(lane == 0.0, w1, jnp.where(lane == 1.0, w2, 0.0))


def _router(x2, g_ffn, sc2, sh2, w_r):
    s, d = x2.shape
    tm = min(512, s)
    vec = lambda: pl.BlockSpec((1, d), lambda i: (0, 0))
    lanes_out = lambda: pl.BlockSpec((tm, LANES), lambda i: (i, 0))
    return pl.pallas_call(
        _router_kernel,
        out_shape=(jax.ShapeDtypeStruct((s, d), F32),
                   jax.ShapeDtypeStruct((s, LANES), jnp.int32),
                   jax.ShapeDtypeStruct((s, LANES), F32)),
        grid=(s // tm,),
        in_specs=[pl.BlockSpec((tm, d), lambda i: (i, 0)), vec(), vec(), vec(),
                  pl.BlockSpec((d, LANES), lambda i: (0, 0))],
        out_specs=(pl.BlockSpec((tm, d), lambda i: (i, 0)), lanes_out(), lanes_out()),
        compiler_params=_cparams(("arbitrary",)),
        name="router",
    )(x2, g_ffn, sc2, sh2, w_r)


def _rank_kernel(eid_ref, rank_ref, cnt_ref, run_ref):
    @pl.when(pl.program_id(0) == 0)
    def _():
        run_ref[...] = jnp.zeros_like(run_ref)

    tb = eid_ref.shape[0]
    lane = lax.broadcasted_iota(jnp.int32, (tb, LANES), 1)
    row = lax.broadcasted_iota(jnp.int32, (tb, tb), 0)
    col = lax.broadcasted_iota(jnp.int32, (tb, tb), 1)
    earlier = (col < row).astype(BF16)
    e = eid_ref[...]
    run = run_ref[...]
    ranks = []
    for k in range(2):
        onehot = e[:, k:k + 1] == lane
        oh = onehot.astype(F32)
        before = jnp.dot(earlier, oh.astype(BF16), preferred_element_type=F32) + run
        ranks.append(jnp.sum(jnp.where(onehot, before, 0.0), axis=-1, keepdims=True))
        run = run + jnp.sum(oh, axis=0, keepdims=True)
    rank_ref[...] = jnp.where(lane == 0, ranks[0], jnp.where(lane == 1, ranks[1], 0.0)).astype(jnp.int32)
    run_ref[...] = run
    cnt_ref[...] = run


def _rank(eid):
    s = eid.shape[0]
    tb = min(512, s)
    return pl.pallas_call(
        _rank_kernel,
        out_shape=(jax.ShapeDtypeStruct((s, LANES), jnp.int32), jax.ShapeDtypeStruct((1, LANES), F32)),
        grid=(s // tb,),
        in_specs=[pl.BlockSpec((tb, LANES), lambda i: (i, 0))],
        out_specs=(pl.BlockSpec((tb, LANES), lambda i: (i, 0)), pl.BlockSpec((1, LANES), lambda i: (0, 0))),
        scratch_shapes=[pltpu.VMEM((1, LANES), F32)],
        compiler_params=_cparams(("arbitrary",)),
        name="rank",
    )(eid)


def _plan_kernel(eid_ref, rank_ref, cnt_ref, dest_ref, src_ref, blke_ref, blkc_ref, nused_ref, pst_ref,
                 *, blk, n_blk):
    def per_expert(e, nb):
        c = cnt_ref[e]
        pst_ref[e] = nb * blk
        nbe = (c + (blk - 1)) // blk

        def per_block(jb, carry):
            blke_ref[nb + jb] = e
            blkc_ref[nb + jb] = jnp.minimum(blk, c - jb * blk)
            return carry

        lax.fori_loop(0, nbe, per_block, 0)

        def pad_slot(r, carry):
            src_ref[nb * blk + r] = 0
            return carry

        lax.fori_loop(c, (c + (GATHER_GROUP - 1)) // GATHER_GROUP * GATHER_GROUP, pad_slot, 0)
        return nb + nbe

    n_used = lax.fori_loop(0, N_EXPERTS, per_expert, jnp.int32(0))
    nused_ref[0] = n_used
    last_e = blke_ref[n_used - 1]

    def unused(i, carry):
        blke_ref[i] = last_e
        blkc_ref[i] = 0
        return carry

    lax.fori_loop(n_used, n_blk, unused, 0)

    n_tok = eid_ref.shape[0] // 2

    def place(a, carry):
        dst = pst_ref[eid_ref[a]] + rank_ref[a]
        dest_ref[a] = dst
        src_ref[dst] = jnp.where(a >= n_tok, a - n_tok, a)
        return carry

    lax.fori_loop(0, eid_ref.shape[0], place, 0, unroll=8)


def _plan(eid_flat, rank_flat, counts, blk):
    a = eid_flat.shape[0]
    n_blk = a // blk + N_EXPERTS
    smem = lambda: pl.BlockSpec(memory_space=pltpu.SMEM)
    i32 = jnp.int32
    return pl.pallas_call(
        functools.partial(_plan_kernel, blk=blk, n_blk=n_blk),
        out_shape=(jax.ShapeDtypeStruct((a,), i32), jax.ShapeDtypeStruct((n_blk * blk,), i32),
                   jax.ShapeDtypeStruct((n_blk,), i32), jax.ShapeDtypeStruct((n_blk,), i32),
                   jax.ShapeDtypeStruct((1,), i32)),
        in_specs=[smem(), smem(), smem()],
        out_specs=(smem(), smem(), smem(), smem(), smem()),
        scratch_shapes=[pltpu.SMEM((N_EXPERTS,), i32)],
        name="plan",
    )(eid_flat, rank_flat, counts)


def _gather_rows(idx_of_row, n_rows, src_hbm, dst, sem):
    def body(g, carry):
        for rr in range(GATHER_GROUP):
            r = g * GATHER_GROUP + rr
            pltpu.make_async_copy(src_hbm.at[pl.ds(idx_of_row(r), 1), :], dst.at[pl.ds(r, 1), :], sem).start()
        return carry
    lax.fori_loop(0, (n_rows + (GATHER_GROUP - 1)) // GATHER_GROUP, body, 0)


def _wait_rows(n_rows, src_hbm, dst, sem):
    def body(g, carry):
        pltpu.make_async_copy(src_hbm.at[pl.ds(0, GATHER_GROUP), :], dst.at[pl.ds(0, GATHER_GROUP), :], sem).wait()
        return carry
    lax.fori_loop(0, (n_rows + (GATHER_GROUP - 1)) // GATHER_GROUP, body, 0)


def _expert_up_kernel(blke_ref, blkc_ref, nused_ref, src_ref, h_hbm, w1_ref, w3_ref, hid_ref, xbuf, gsem,
                      *, blk, kc):
    i = pl.program_id(0)
    n_used = nused_ref[0]

    def issue_rows(b, slot):
        _gather_rows(lambda r: src_ref[b * blk + r], blkc_ref[b], h_hbm, xbuf.at[slot], gsem.at[slot])

    slot = lax.rem(i, 2)

    @pl.when(i == 0)
    def _():
        xbuf[...] = jnp.zeros_like(xbuf)
        issue_rows(0, 0)

    @pl.when(i < n_used)
    def _():
        _wait_rows(blkc_ref[i], h_hbm, xbuf.at[slot], gsem.at[slot])

        @pl.when(i + 1 < n_used)
        def _():
            issue_rows(i + 1, 1 - slot)

        def up_proj(w_ref):
            acc = None
            for k0 in range(0, xbuf.shape[2], kc):
                part = jnp.dot(xbuf[slot, :, k0:k0 + kc].astype(BF16), w_ref[k0:k0 + kc, :].astype(BF16),
                               preferred_element_type=F32)
                acc = part if acc is None else acc + part
            return acc

        a = up_proj(w1_ref)
        b = up_proj(w3_ref)
        hid_ref[...] = ((a * jax.nn.sigmoid(a)) * b).astype(BF16)


def _expert_down_kernel(blke_ref, nused_ref, hid_ref, w2_ref, o_ref):
    @pl.when(pl.program_id(0) < nused_ref[0])
    def _():
        hid = hid_ref[...]
        for t in range(0, o_ref.shape[1], Y_TILE):
            o_ref[:, t:t + Y_TILE] = jnp.dot(hid, w2_ref[:, t:t + Y_TILE].astype(BF16),
                                             preferred_element_type=F32)


def _experts(h2, w1, w3, w2, blk_e, blk_cnt, n_used, row_src, blk):
    s, d = h2.shape
    de = w1.shape[-1]
    kc = min(1024, d)
    assert d % kc == 0 and d % Y_TILE == 0 and blk % GATHER_GROUP == 0
    n_blk = blk_e.shape[0]
    used = lambda i, nu: jnp.minimum(i, nu[0] - 1)
    hid = pl.pallas_call(
        functools.partial(_expert_up_kernel, blk=blk, kc=kc),
        out_shape=jax.ShapeDtypeStruct((n_blk * blk, de), BF16),
        grid_spec=pltpu.PrefetchScalarGridSpec(
            num_scalar_prefetch=4,
            grid=(n_blk,),
            in_specs=[pl.BlockSpec(memory_space=pl.ANY),
                      pl.BlockSpec((None, d, de), lambda i, be, bc, nu, rs: (be[i], 0, 0)),
                      pl.BlockSpec((None, d, de), lambda i, be, bc, nu, rs: (be[i], 0, 0))],
            out_specs=pl.BlockSpec((blk, de), lambda i, be, bc, nu, rs: (used(i, nu), 0)),
            scratch_shapes=[pltpu.VMEM((2, blk, d), F32), pltpu.SemaphoreType.DMA((2,))]),
        compiler_params=_cparams(("arbitrary",)),
        name="experts_up",
    )(blk_e, blk_cnt, n_used, row_src, h2, w1, w3)
    return pl.pallas_call(
        _expert_down_kernel,
        out_shape=jax.ShapeDtypeStruct((n_blk * blk, d), F32),
        grid_spec=pltpu.PrefetchScalarGridSpec(
            num_scalar_prefetch=2,
            grid=(n_blk,),
            in_specs=[pl.BlockSpec((blk, de), lambda i, be, nu: (used(i, nu), 0)),
                      pl.BlockSpec((None, de, d), lambda i, be, nu: (be[i], 0, 0))],
            out_specs=pl.BlockSpec((blk, d), lambda i, be, nu: (used(i, nu), 0))),
        compiler_params=_cparams(("arbitrary",)),
        name="experts_down",
    )(blk_e, n_used, hid, w2)


def _combine_kernel(dest_ref, ys_hbm, x_ref, wt_ref, gt_ref, o_ref, ybuf, sem):
    i = pl.program_id(0)
    nb = pl.num_programs(0)
    tm = x_ref.shape[0]
    n_tok = tm * nb

    def issue(blk_i, slot):
        for k in range(2):
            _gather_rows(lambda r: dest_ref[k * n_tok + blk_i * tm + r], tm, ys_hbm, ybuf.at[slot, k],
                         sem.at[slot])

    slot = lax.rem(i, 2)

    @pl.when(i == 0)
    def _():
        issue(0, 0)

    @pl.when(i + 1 < nb)
    def _():
        issue(i + 1, 1 - slot)

    for k in range(2):
        _wait_rows(tm, ys_hbm, ybuf.at[slot, k], sem.at[slot])
    w = wt_ref[...]
    y = ybuf[slot, 0] * w[:, 0:1] + ybuf[slot, 1] * w[:, 1:2]
    o_ref[...] = x_ref[...] + gt_ref[...] * y


def _combine(dest, ys, x2, wts, gt2):
    s, d = x2.shape
    tm = min(256, s)
    return pl.pallas_call(
        _combine_kernel,
        out_shape=jax.ShapeDtypeStruct((s, d), F32),
        grid_spec=pltpu.PrefetchScalarGridSpec(
            num_scalar_prefetch=1,
            grid=(s // tm,),
            in_specs=[pl.BlockSpec(memory_space=pl.ANY),
                      pl.BlockSpec((tm, d), lambda i, ds_: (i, 0)),
                      pl.BlockSpec((tm, LANES), lambda i, ds_: (i, 0)),
                      pl.BlockSpec((1, d), lambda i, ds_: (0, 0))],
            out_specs=pl.BlockSpec((tm, d), lambda i, ds_: (i, 0)),
            scratch_shapes=[pltpu.VMEM((2, 2, tm, d), F32), pltpu.SemaphoreType.DMA((2,))]),
        compiler_params=_cparams(("arbitrary",)),
        name="combine",
    )(dest, ys, x2, wts, gt2)


def _layer(x2d, c, w_ada, b_ada, g_mix, g_ffn, w_in, conv_w, q_norm_g, k_norm_g, g_branch, w_out,
           w_group, w_expert, w1, w3, w2):
    s, d = x2d.shape
    conv_ch = conv_w.shape[1]
    attn_w = d - conv_ch
    n_heads = attn_w // HEAD_DIM
    na = s // RES
    assert s % (RES * BAND) == 0
    row = lambda v: v.reshape(1, -1)

    mod = _ada(c, w_ada, b_ada)
    sh1, sc1, gt1, sh2, sc2, gt2 = [mod[:, i * d:(i + 1) * d] for i in range(6)]

    h1 = _norm1(x2d, row(g_mix), sc1, sh1)
    z = _inproj(h1.reshape(s, d), w_in, row(q_norm_g), row(k_norm_g), conv_ch, attn_w)
    z3 = z.reshape(RES, na, z.shape[1])
    y_conv = _conv(z3, conv_w, row(g_branch[:conv_ch]), conv_ch)
    slopes = jnp.exp2(-ALIBI_MAX_EXP * jnp.arange(1, n_heads + 1, dtype=F32) / n_heads)
    y_attn = _attention(z3, slopes, conv_ch, attn_w)
    x2 = _outproj(y_conv, y_attn, row(g_branch[conv_ch:]), w_out.astype(BF16), x2d, gt1)

    w_r = jnp.concatenate([w_group, w_expert,
                           jnp.zeros((d, LANES - N_GROUPS - N_EXPERTS), F32)], axis=1)
    h2, eid, wts = _router(x2, row(g_ffn), sc2, sh2, w_r)
    rank, counts = _rank(eid)
    eid_flat = eid[:, :2].T.reshape(-1)
    rank_flat = rank[:, :2].T.reshape(-1)
    cnt = counts[0, :N_EXPERTS].astype(jnp.int32)
    dest, row_src, blk_e, blk_cnt, n_used = _plan(eid_flat, rank_flat, cnt, MOE_BLOCK)
    ys = _experts(h2, w1, w3, w2, blk_e, blk_cnt, n_used, row_src, MOE_BLOCK)
    return _combine(dest, ys, x2, wts, gt2)


def kernel(x, c, w_ada, b_ada, g_mix, g_ffn, w_in, conv_w, q_norm_g, k_norm_g, g_branch, w_out,
           w_group, w_expert, w1, w3, w2):
    b, s, d = x.shape
    assert b == 1 and c.shape[0] == 1
    for l in range(w_ada.shape[0]):
        x = _layer(x.reshape(s, d), c[0], w_ada[l], b_ada[l], g_mix[l], g_ffn[l], w_in[l], conv_w[l],
                   q_norm_g[l], k_norm_g[l], g_branch[l], w_out[l], w_group[l], w_expert[l],
                   w1[l], w3[l], w2[l]).reshape(b, s, d)
    return x
```

```python
import functools

import jax
import jax.numpy as jnp
from jax import lax
from jax.experimental import pallas as pl
from jax.experimental.pallas import tpu as pltpu

EPS = 1e-6
HEAD_DIM = 128
BAND = 128
PATTERNS = ((128, 1), (512, 4), (2048, 16))
RES = 16
ALIBI_MAX_EXP = 8.0
N_GROUPS = 8
EXPERTS_PER_GROUP = 8
N_EXPERTS = N_GROUPS * EXPERTS_PER_GROUP
MOE_BLOCK = 256
GATHER_GROUP = 8
Y_TILE = 512
OUT_CHUNK = 64
LANES = 128
SUBLANES = 8
VMEM_LIMIT = 56 * 1024 * 1024
F32 = jnp.float32
BF16 = jnp.bfloat16
NEG_INF = float("-inf")


def _cparams(sem):
    return pltpu.CompilerParams(dimension_semantics=sem, vmem_limit_bytes=VMEM_LIMIT)


def _ada_kernel(c_ref, w_ref, b_ref, o_ref, s_ref, *, rows):
    @pl.when(pl.program_id(0) == 0)
    def _():
        c = c_ref[...]
        s_ref[...] = c * jax.nn.sigmoid(c)

    d, tn = w_ref.shape

    def body(i, acc):
        r0 = pl.multiple_of(i * rows, rows)
        p = w_ref[pl.ds(r0, rows), :] * s_ref[pl.ds(r0, rows), :]
        return acc + jnp.sum(p.reshape(rows // SUBLANES, SUBLANES, tn), axis=0)

    acc = lax.fori_loop(0, d // rows, body, jnp.zeros((SUBLANES, tn), F32))
    o_ref[...] = jnp.sum(acc, axis=0, keepdims=True) + b_ref[...]


def _ada(c, w_ada, b_ada):
    d, n = w_ada.shape
    tn = min(512, n)
    rows = min(512, d)
    return pl.pallas_call(
        functools.partial(_ada_kernel, rows=rows),
        out_shape=jax.ShapeDtypeStruct((1, n), F32),
        grid=(n // tn,),
        in_specs=[pl.BlockSpec((d, 1), lambda j: (0, 0)),
                  pl.BlockSpec((d, tn), lambda j: (0, j)),
                  pl.BlockSpec((1, tn), lambda j: (0, j))],
        out_specs=pl.BlockSpec((1, tn), lambda j: (0, j)),
        scratch_shapes=[pltpu.VMEM((d, 1), F32)],
        compiler_params=_cparams(("arbitrary",)),
        name="ada",
    )(c.reshape(d, 1), w_ada, b_ada.reshape(1, n))


def _rms_mod(x, g, sc, sh):
    ms = jnp.mean(x * x, axis=-1, keepdims=True)
    return (x * lax.rsqrt(ms + EPS) * g) * (1.0 + sc) + sh


def _row_permutation(ta, to_residue_major):
    n = RES * ta
    assert ta & (ta - 1) == 0
    sh = ta.bit_length() - 1
    r = lax.broadcasted_iota(jnp.int32, (n, n), 0)
    c = lax.broadcasted_iota(jnp.int32, (n, n), 1)
    res_major, natural = (r, c) if to_residue_major else (c, r)
    return (natural == RES * (res_major & (ta - 1)) + (res_major >> sh)).astype(BF16)


def _norm1_kernel(x_ref, g_ref, sc_ref, sh_ref, o_ref):
    ta = o_ref.shape[1]
    h = _rms_mod(x_ref[...], g_ref[...], sc_ref[...], sh_ref[...]).astype(BF16)
    hp = jnp.dot(_row_permutation(ta, True), h, preferred_element_type=F32)
    o_ref[...] = hp.astype(BF16).reshape(o_ref.shape)


def _norm1(x2d, g_mix, sc1, sh1):
    s, d = x2d.shape
    na = s // RES
    ta = min(32, na)
    vec = lambda: pl.BlockSpec((1, d), lambda i: (0, 0))
    return pl.pallas_call(
        _norm1_kernel,
        out_shape=jax.ShapeDtypeStruct((RES, na, d), BF16),
        grid=(na // ta,),
        in_specs=[pl.BlockSpec((RES * ta, d), lambda i: (i, 0)), vec(), vec(), vec()],
        out_specs=pl.BlockSpec((RES, ta, d), lambda i: (0, i, 0)),
        compiler_params=_cparams(("arbitrary",)),
        name="norm1",
    )(x2d, g_mix, sc1, sh1)


def _inproj_kernel(h_ref, w_ref, qg_ref, kg_ref, z_ref, wb_ref, *, q0, k0, v0):
    j = pl.program_id(0)

    @pl.when(pl.program_id(1) == 0)
    def _():
        wb_ref[...] = w_ref[...].astype(BF16)

    acc = jnp.dot(h_ref[...], wb_ref[...], preferred_element_type=F32)
    is_q = (j >= q0) & (j < k0)
    is_k = (j >= k0) & (j < v0)

    g = jnp.where(is_q, qg_ref[...] * (HEAD_DIM ** -0.5), kg_ref[...])
    for hh in range(acc.shape[1] // HEAD_DIM):
        a = acc[:, hh * HEAD_DIM:(hh + 1) * HEAD_DIM]
        ms = jnp.mean(a * a, axis=-1, keepdims=True)
        scale = jnp.where(is_q | is_k, lax.rsqrt(ms + EPS) * g, 1.0)
        z_ref[:, hh * HEAD_DIM:(hh + 1) * HEAD_DIM] = a * scale


def _inproj(h1, w_in, q_g, k_g, conv_ch, attn_w):
    s, d = h1.shape
    n = w_in.shape[1]
    tm = min(1024, s)
    tn = min(512, conv_ch)
    assert conv_ch % tn == 0 and attn_w % tn == 0 and tn % HEAD_DIM == 0
    q0 = 3 * conv_ch // tn
    k0 = q0 + attn_w // tn
    v0 = k0 + attn_w // tn
    hvec = lambda: pl.BlockSpec((1, HEAD_DIM), lambda j, i: (0, 0))
    return pl.pallas_call(
        functools.partial(_inproj_kernel, q0=q0, k0=k0, v0=v0),
        out_shape=jax.ShapeDtypeStruct((s, n), F32),
        grid=(n // tn, s // tm),
        in_specs=[pl.BlockSpec((tm, d), lambda j, i: (i, 0)),
                  pl.BlockSpec((d, tn), lambda j, i: (0, j)), hvec(), hvec()],
        out_specs=pl.BlockSpec((tm, tn), lambda j, i: (i, j)),
        scratch_shapes=[pltpu.VMEM((d, tn), BF16)],
        compiler_params=_cparams(("arbitrary", "arbitrary")),
        name="inproj",
    )(h1, w_in, q_g, k_g)


def _conv_kernel(b_ref, c_ref, v_ref, cw_ref, g_ref, o_ref, ue_ref):
    i = pl.program_id(0)
    ta = b_ref.shape[1]
    w = cw_ref[...]
    assert w.shape[0] == 3
    for k in range(2):
        @pl.when(i == 0)
        def _():
            ue_ref[k, 0:SUBLANES, :] = jnp.zeros((SUBLANES, ue_ref.shape[2]), F32)

        @pl.when(i > 0)
        def _():
            ue_ref[k, 0:SUBLANES, :] = ue_ref[k, ta:ta + SUBLANES, :]

        r = RES - 2 + k
        ue_ref[k, SUBLANES:SUBLANES + ta, :] = c_ref[r] * v_ref[r]

    def u_at(bb):
        if bb >= 0:
            return c_ref[bb] * v_ref[bb]
        return ue_ref[2 + bb, SUBLANES - 1:SUBLANES - 1 + ta, :]

    for b in range(RES):
        y = b_ref[b] * (w[0:1, :] * u_at(b - 2) + w[1:2, :] * u_at(b - 1) + w[2:3, :] * u_at(b))
        ms = jnp.mean(y * y, axis=-1, keepdims=True)
        o_ref[b] = (y * lax.rsqrt(ms + EPS) * g_ref[...]).astype(BF16)


def _conv(z3, conv_w, g_conv, conv_ch):
    na = z3.shape[1]
    ta = min(64, na)
    blk = lambda jj: pl.BlockSpec((RES, ta, conv_ch), lambda i: (0, i, jj))
    return pl.pallas_call(
        _conv_kernel,
        out_shape=jax.ShapeDtypeStruct((RES, na, conv_ch), BF16),
        grid=(na // ta,),
        in_specs=[blk(0), blk(1), blk(2),
                  pl.BlockSpec(conv_w.shape, lambda i: (0, 0)),
                  pl.BlockSpec((1, conv_ch), lambda i: (0, 0))],
        out_specs=pl.BlockSpec((RES, ta, conv_ch), lambda i: (0, i, 0)),
        scratch_shapes=[pltpu.VMEM((2, ta + SUBLANES, conv_ch), F32)],
        compiler_params=_cparams(("arbitrary",)),
        name="conv",
    )(z3, z3, z3, conv_w, g_conv)


def _attn_kernel(slopes_ref, q_ref, kp_ref, kc_ref, vp_ref, vc_ref, o_ref, bias_ref, m_ref, l_ref, acc_ref):
    h = pl.program_id(0)
    cidx = pl.program_id(1)
    slope = slopes_ref[h]
    first_pen = jnp.where(cidx > 0, 0.0, NEG_INF).astype(F32)
    dn_t = (((1,), (1,)), ((), ()))

    j = lax.broadcasted_iota(jnp.int32, (BAND, 2 * BAND), 0)
    u = lax.broadcasted_iota(jnp.int32, (BAND, 2 * BAND), 1)
    is_prev = u < BAND
    uu = jnp.where(is_prev, u, u - BAND)
    for pi, (window, dil) in enumerate(PATTERNS):
        assert window // dil == BAND and RES % dil == 0
        ns, sr = RES // dil, BAND * dil // RES
        assert sr & (sr - 1) == 0
        sh = sr.bit_length() - 1
        mq = (j & (sr - 1)) * ns + (j >> sh)
        mk = (uu & (sr - 1)) * ns + (uu >> sh)
        steps = mq - mk + jnp.where(is_prev, BAND, 0)
        valid = (is_prev & (steps <= BAND)) | (jnp.logical_not(is_prev) & (steps >= 0))
        bias = jnp.where(valid, -slope * (steps * dil).astype(F32), NEG_INF)
        bias_ref[pi, 0] = bias
        bias_ref[pi, 1] = bias + jnp.where(is_prev, first_pen, 0.0)

    for pi, (window, dil) in enumerate(PATTERNS):
        ns, sr = RES // dil, BAND * dil // RES
        n_chunks = BAND // sr
        for n in range(n_chunks):
            for r in range(dil):
                def gather(ref, nn):
                    return jnp.concatenate([ref[dil * c + r, nn * sr:(nn + 1) * sr, :] for c in range(ns)], axis=0)

                def scatter(ref, val):
                    for c in range(ns):
                        ref[dil * c + r, n * sr:(n + 1) * sr, :] = val[c * sr:(c + 1) * sr, :]

                q = gather(q_ref, n).astype(BF16)
                if n > 0:
                    k_prev, v_prev = gather(kc_ref, n - 1), gather(vc_ref, n - 1)
                else:
                    k_prev, v_prev = gather(kp_ref, n_chunks - 1), gather(vp_ref, n_chunks - 1)
                kcat = jnp.concatenate([k_prev, gather(kc_ref, n)], axis=0).astype(BF16)
                vcat = jnp.concatenate([v_prev, gather(vc_ref, n)], axis=0).astype(BF16)
                s = lax.dot_general(q, kcat, dn_t, preferred_element_type=F32) + bias_ref[pi, 1 if n == 0 else 0]
                m = jnp.max(s, axis=-1, keepdims=True)
                p = jnp.exp(s - m)
                l = jnp.sum(p, axis=-1, keepdims=True)
                o = jnp.dot(p.astype(BF16), vcat, preferred_element_type=F32)
                m_b = jnp.broadcast_to(m, (BAND, HEAD_DIM))
                l_b = jnp.broadcast_to(l, (BAND, HEAD_DIM))
                if pi == 0:
                    scatter(m_ref, m_b)
                    scatter(l_ref, l_b)
                    scatter(acc_ref, o)
                else:
                    m_old = gather(m_ref, n)
                    m_new = jnp.maximum(m_old, m_b)
                    a_old = jnp.exp(m_old - m_new)
                    a_new = jnp.exp(m_b - m_new)
                    l_new = a_old * gather(l_ref, n) + a_new * l_b
                    o_new = a_old * gather(acc_ref, n) + a_new * o
                    if pi == len(PATTERNS) - 1:
                        scatter(acc_ref, o_new / l_new)
                    else:
                        scatter(m_ref, m_new)
                        scatter(l_ref, l_new)
                        scatter(acc_ref, o_new)
    o_ref[...] = acc_ref[...].astype(BF16)


def _attention(z3, slopes, conv_ch, attn_w):
    na = z3.shape[1]
    n_heads = attn_w // HEAD_DIM
    assert na % BAND == 0
    qb = 3 * conv_ch // HEAD_DIM
    kb = qb + n_heads
    vb = kb + n_heads
    blk = (RES, BAND, HEAD_DIM)
    cur = lambda base: pl.BlockSpec(blk, lambda h, c, sl: (0, c, base + h))
    prev = lambda base: pl.BlockSpec(blk, lambda h, c, sl: (0, jnp.maximum(c - 1, 0), base + h))
    return pl.pallas_call(
        _attn_kernel,
        out_shape=jax.ShapeDtypeStruct((RES, na, attn_w), BF16),
        grid_spec=pltpu.PrefetchScalarGridSpec(
            num_scalar_prefetch=1,
            grid=(n_heads, na // BAND),
            in_specs=[cur(qb), prev(kb), cur(kb), prev(vb), cur(vb)],
            out_specs=pl.BlockSpec(blk, lambda h, c, sl: (0, c, h)),
            scratch_shapes=[pltpu.VMEM((len(PATTERNS), 2, BAND, 2 * BAND), F32),
                            pltpu.VMEM(blk, F32), pltpu.VMEM(blk, F32), pltpu.VMEM(blk, F32)]),
        compiler_params=_cparams(("arbitrary", "arbitrary")),
        name="attn",
    )(slopes, z3, z3, z3, z3, z3)


def _outproj_kernel(yc_ref, ya_ref, ga_ref, w_ref, x_ref, gt_ref, o_ref, y_ref):
    _, ta, cc = yc_ref.shape

    @pl.when(pl.program_id(1) == 0)
    def _():
        n = RES * ta
        ya = ya_ref[...].reshape(n, ya_ref.shape[2]).astype(F32)
        ms = jnp.mean(ya * ya, axis=-1, keepdims=True)
        yn = (ya * lax.rsqrt(ms + EPS) * ga_ref[...]).astype(BF16)
        perm = _row_permutation(ta, False)
        y_ref[:, 0:cc] = jnp.dot(perm, yc_ref[...].reshape(n, cc), preferred_element_type=F32).astype(BF16)
        y_ref[:, cc:] = jnp.dot(perm, yn, preferred_element_type=F32).astype(BF16)

    acc = jnp.dot(y_ref[...], w_ref[...], preferred_element_type=F32)
    o_ref[...] = x_ref[...] + gt_ref[...] * acc


def _outproj(y_conv3, y_attn3, g_attn, w_out_bf, x2d, gt1):
    s, d = x2d.shape
    na = s // RES
    cc, aw = y_conv3.shape[2], y_attn3.shape[2]
    ta = min(32, na)
    tn = min(1024, d)
    return pl.pallas_call(
        _outproj_kernel,
        out_shape=jax.ShapeDtypeStruct((s, d), F32),
        grid=(na // ta, d // tn),
        in_specs=[pl.BlockSpec((RES, ta, cc), lambda i, j: (0, i, 0)),
                  pl.BlockSpec((RES, ta, aw), lambda i, j: (0, i, 0)),
                  pl.BlockSpec((1, aw), lambda i, j: (0, 0)),
                  pl.BlockSpec((cc + aw, tn), lambda i, j: (0, j)),
                  pl.BlockSpec((RES * ta, tn), lambda i, j: (i, j)),
                  pl.BlockSpec((1, tn), lambda i, j: (0, j))],
        out_specs=pl.BlockSpec((RES * ta, tn), lambda i, j: (i, j)),
        scratch_shapes=[pltpu.VMEM((RES * ta, cc + aw), BF16)],
        compiler_params=_cparams(("arbitrary", "arbitrary")),
        name="outproj",
    )(y_conv3, y_attn3, g_attn, w_out_bf, x2d, gt1)


def _router_kernel(x_ref, g_ref, sc_ref, sh_ref, wr_ref, h_ref, eid_ref, wt_ref):
    h = _rms_mod(x_ref[...], g_ref[...], sc_ref[...], sh_ref[...])
    h_ref[...] = h
    w = wr_ref[...]
    h_hi, w_hi = h.astype(BF16), w.astype(BF16)
    h_lo = (h - h_hi.astype(F32)).astype(BF16)
    w_lo = (w - w_hi.astype(F32)).astype(BF16)
    logits = (jnp.dot(h_hi, w_hi, preferred_element_type=F32) + jnp.dot(h_hi, w_lo, preferred_element_type=F32)
              + jnp.dot(h_lo, w_hi, preferred_element_type=F32))
    tm = logits.shape[0]
    lane = lax.broadcasted_iota(jnp.int32, (tm, LANES), 1).astype(F32)
    big = float(LANES)
    is_g = lane < N_GROUPS
    gl = jnp.where(is_g, logits, NEG_INF)
    gmax = jnp.max(gl, axis=-1, keepdims=True)
    gsel = jnp.min(jnp.where(gl == gmax, lane, big), axis=-1, keepdims=True)
    g_w = 1.0 / jnp.sum(jnp.where(is_g, jnp.exp(gl - gmax), 0.0), axis=-1, keepdims=True)
    lo = N_GROUPS + gsel * EXPERTS_PER_GROUP
    in_grp = (lane >= lo) & (lane < lo + EXPERTS_PER_GROUP)
    el = jnp.where(in_grp, logits, NEG_INF)
    e1 = jnp.max(el, axis=-1, keepdims=True)
    i1 = jnp.min(jnp.where(el == e1, lane, big), axis=-1, keepdims=True)
    el2 = jnp.where(lane == i1, NEG_INF, el)
    e2 = jnp.max(el2, axis=-1, keepdims=True)
    i2 = jnp.min(jnp.where(el2 == e2, lane, big), axis=-1, keepdims=True)
    t = jnp.exp(e2 - e1)
    w1 = g_w / (1.0 + t)
    w2 = g_w * t / (1.0 + t)
    eid = jnp.where(lane == 0.0, i1 - N_GROUPS, jnp.where(lane == 1.0, i2 - N_GROUPS, 0.0))
    eid_ref[...] = eid.astype(jnp.int32)
    wt_ref[...] = jnp.where(lane == 0.0, w1, jnp.where(lane == 1.0, w2, 0.0))


def _router(x2, g_ffn, sc2, sh2, w_r):
    s, d = x2.shape
    tm = min(512, s)
    vec = lambda: pl.BlockSpec((1, d), lambda i: (0, 0))
    lanes_out = lambda: pl.BlockSpec((tm, LANES), lambda i: (i, 0))
    return pl.pallas_call(
        _router_kernel,
        out_shape=(jax.ShapeDtypeStruct((s, d), F32),
                   jax.ShapeDtypeStruct((s, LANES), jnp.int32),
                   jax.ShapeDtypeStruct((s, LANES), F32)),
        grid=(s // tm,),
        in_specs=[pl.BlockSpec((tm, d), lambda i: (i, 0)), vec(), vec(), vec(),
                  pl.BlockSpec((d, LANES), lambda i: (0, 0))],
        out_specs=(pl.BlockSpec((tm, d), lambda i: (i, 0)), lanes_out(), lanes_out()),
        compiler_params=_cparams(("arbitrary",)),
        name="router",
    )(x2, g_ffn, sc2, sh2, w_r)


def _rank_kernel(eid_ref, rank_ref, cnt_ref, run_ref):
    @pl.when(pl.program_id(0) == 0)
    def _():
        run_ref[...] = jnp.zeros_like(run_ref)

    tb = eid_ref.shape[0]
    lane = lax.broadcasted_iota(jnp.int32, (tb, LANES), 1)
    row = lax.broadcasted_iota(jnp.int32, (tb, tb), 0)
    col = lax.broadcasted_iota(jnp.int32, (tb, tb), 1)
    earlier = (col < row).astype(BF16)
    e = eid_ref[...]
    run = run_ref[...]
    ranks = []
    for k in range(2):
        onehot = e[:, k:k + 1] == lane
        oh = onehot.astype(F32)
        before = jnp.dot(earlier, oh.astype(BF16), preferred_element_type=F32) + run
        ranks.append(jnp.sum(jnp.where(onehot, before, 0.0), axis=-1, keepdims=True))
        run = run + jnp.sum(oh, axis=0, keepdims=True)
    rank_ref[...] = jnp.where(lane == 0, ranks[0], jnp.where(lane == 1, ranks[1], 0.0)).astype(jnp.int32)
    run_ref[...] = run
    cnt_ref[...] = run


def _rank(eid):
    s = eid.shape[0]
    tb = min(512, s)
    return pl.pallas_call(
        _rank_kernel,
        out_shape=(jax.ShapeDtypeStruct((s, LANES), jnp.int32), jax.ShapeDtypeStruct((1, LANES), F32)),
        grid=(s // tb,),
        in_specs=[pl.BlockSpec((tb, LANES), lambda i: (i, 0))],
        out_specs=(pl.BlockSpec((tb, LANES), lambda i: (i, 0)), pl.BlockSpec((1, LANES), lambda i: (0, 0))),
        scratch_shapes=[pltpu.VMEM((1, LANES), F32)],
        compiler_params=_cparams(("arbitrary",)),
        name="rank",
    )(eid)


def _plan_kernel(eid_ref, rank_ref, cnt_ref, dest_ref, src_ref, blke_ref, blkc_ref, nused_ref, pst_ref,
                 *, blk, n_blk):
    def per_expert(e, nb):
        c = cnt_ref[e]
        pst_ref[e] = nb * blk
        nbe = (c + (blk - 1)) // blk

        def per_block(jb, carry):
            blke_ref[nb + jb] = e
            blkc_ref[nb + jb] = jnp.minimum(blk, c - jb * blk)
            return carry

        lax.fori_loop(0, nbe, per_block, 0)

        def pad_slot(r, carry):
            src_ref[nb * blk + r] = 0
            return carry

        lax.fori_loop(c, (c + (GATHER_GROUP - 1)) // GATHER_GROUP * GATHER_GROUP, pad_slot, 0)
        return nb + nbe

    n_used = lax.fori_loop(0, N_EXPERTS, per_expert, jnp.int32(0))
    nused_ref[0] = n_used
    last_e = blke_ref[n_used - 1]

    def unused(i, carry):
        blke_ref[i] = last_e
        blkc_ref[i] = 0
        return carry

    lax.fori_loop(n_used, n_blk, unused, 0)

    n_tok = eid_ref.shape[0] // 2

    def place(a, carry):
        dst = pst_ref[eid_ref[a]] + rank_ref[a]
        dest_ref[a] = dst
        src_ref[dst] = jnp.where(a >= n_tok, a - n_tok, a)
        return carry

    lax.fori_loop(0, eid_ref.shape[0], place, 0, unroll=8)


def _plan(eid_flat, rank_flat, counts, blk):
    a = eid_flat.shape[0]
    n_blk = a // blk + N_EXPERTS
    smem = lambda: pl.BlockSpec(memory_space=pltpu.SMEM)
    i32 = jnp.int32
    return pl.pallas_call(
        functools.partial(_plan_kernel, blk=blk, n_blk=n_blk),
        out_shape=(jax.ShapeDtypeStruct((a,), i32), jax.ShapeDtypeStruct((n_blk * blk,), i32),
                   jax.ShapeDtypeStruct((n_blk,), i32), jax.ShapeDtypeStruct((n_blk,), i32),
                   jax.ShapeDtypeStruct((1,), i32)),
        in_specs=[smem(), smem(), smem()],
        out_specs=(smem(), smem(), smem(), smem(), smem()),
        scratch_shapes=[pltpu.SMEM((N_EXPERTS,), i32)],
        name="plan",
    )(eid_flat, rank_flat, counts)


def _gather_rows(idx_of_row, n_rows, src_hbm, dst, sem):
    def body(g, carry):
        for rr in range(GATHER_GROUP):
            r = g * GATHER_GROUP + rr
            pltpu.make_async_copy(src_hbm.at[pl.ds(idx_of_row(r), 1), :], dst.at[pl.ds(r, 1), :], sem).start()
        return carry
    lax.fori_loop(0, (n_rows + (GATHER_GROUP - 1)) // GATHER_GROUP, body, 0)


def _wait_rows(n_rows, src_hbm, dst, sem):
    def body(g, carry):
        pltpu.make_async_copy(src_hbm.at[pl.ds(0, GATHER_GROUP), :], dst.at[pl.ds(0, GATHER_GROUP), :], sem).wait()
        return carry
    lax.fori_loop(0, (n_rows + (GATHER_GROUP - 1)) // GATHER_GROUP, body, 0)


def _expert_kernel(blke_ref, blkc_ref, nused_ref, src_ref, h_hbm, w1_hbm, w3_hbm, w2_hbm, ys_hbm,
                   xbuf, obuf, w1b, w3b, w2b, gsem, wsem, osem, *, blk, kc):
    i = pl.program_id(0)
    last = pl.num_programs(0) - 1
    n_used = nused_ref[0]
    e = blke_ref[i]
    e_next = blke_ref[jnp.minimum(i + 1, last)]
    first_of_expert = (i == 0) | (blke_ref[jnp.maximum(i - 1, 0)] != e)
    fetch_next = (i + 1 < n_used) & (e_next != e)
    wbufs = ((w1_hbm, w1b), (w3_hbm, w3b), (w2_hbm, w2b))

    def weight_copy(k, ee):
        return pltpu.make_async_copy(wbufs[k][0].at[ee], wbufs[k][1], wsem.at[k])

    def issue_rows(b, slot):
        _gather_rows(lambda r: src_ref[b * blk + r], blkc_ref[b], h_hbm, xbuf.at[slot], gsem.at[slot])

    def weights_ready(k):
        @pl.when(first_of_expert)
        def _():
            weight_copy(k, e).wait()

    def weights_done(k):
        @pl.when(fetch_next)
        def _():
            weight_copy(k, e_next).start()

    def out_copy(b, slot, ch):
        return pltpu.make_async_copy(obuf.at[slot, pl.ds(ch * OUT_CHUNK, OUT_CHUNK), :],
                                     ys_hbm.at[pl.ds(b * blk + ch * OUT_CHUNK, OUT_CHUNK), :], osem.at[slot])

    def for_out_chunks(b, fn):
        def body(ch, carry):
            fn(ch)
            return carry
        lax.fori_loop(0, (blkc_ref[b] + (OUT_CHUNK - 1)) // OUT_CHUNK, body, 0)

    slot = lax.rem(i, 2)

    @pl.when(i == 0)
    def _():
        for k in range(3):
            weight_copy(k, e).start()
        xbuf[...] = jnp.zeros_like(xbuf)
        issue_rows(0, 0)

    @pl.when(i < n_used)
    def _():
        _wait_rows(blkc_ref[i], h_hbm, xbuf.at[slot], gsem.at[slot])

        @pl.when(i + 1 < n_used)
        def _():
            issue_rows(i + 1, 1 - slot)

        d = xbuf.shape[2]

        def up_proj(wb):
            acc = None
            for k0 in range(0, d, kc):
                part = jnp.dot(xbuf[slot, :, k0:k0 + kc].astype(BF16), wb[k0:k0 + kc, :].astype(BF16),
                               preferred_element_type=F32)
                acc = part if acc is None else acc + part
            return acc

        weights_ready(0)
        a = up_proj(w1b)
        weights_done(0)
        weights_ready(1)
        b = up_proj(w3b)
        weights_done(1)
        hid = ((a * jax.nn.sigmoid(a)) * b).astype(BF16)
        weights_ready(2)

        @pl.when(i >= 2)
        def _():
            for_out_chunks(i - 2, lambda ch: out_copy(i - 2, slot, 0).wait())

        for t in range(0, d, Y_TILE):
            obuf[slot, :, t:t + Y_TILE] = jnp.dot(hid, w2b[:, t:t + Y_TILE].astype(BF16),
                                                  preferred_element_type=F32)
        weights_done(2)
        for_out_chunks(i, lambda ch: out_copy(i, slot, ch).start())

        @pl.when(i == n_used - 1)
        def _():
            @pl.when(i >= 1)
            def _():
                for_out_chunks(i - 1, lambda ch: out_copy(i - 1, 1 - slot, 0).wait())
            for_out_chunks(i, lambda ch: out_copy(i, slot, 0).wait())


def _experts(h2, w1, w3, w2, blk_e, blk_cnt, n_used, row_src, blk):
    s, d = h2.shape
    de = w1.shape[-1]
    kc = min(1024, d)
    assert d % kc == 0 and d % Y_TILE == 0 and blk % GATHER_GROUP == 0 and blk % OUT_CHUNK == 0
    n_blk = blk_e.shape[0]
    anyspace = lambda: pl.BlockSpec(memory_space=pl.ANY)
    return pl.pallas_call(
        functools.partial(_expert_kernel, blk=blk, kc=kc),
        out_shape=jax.ShapeDtypeStruct((n_blk * blk, d), F32),
        grid_spec=pltpu.PrefetchScalarGridSpec(
            num_scalar_prefetch=4,
            grid=(n_blk,),
            in_specs=[anyspace(), anyspace(), anyspace(), anyspace()],
            out_specs=anyspace(),
            scratch_shapes=[pltpu.VMEM((2, blk, d), F32), pltpu.VMEM((2, blk, d), F32),
                            pltpu.VMEM((d, de), F32), pltpu.VMEM((d, de), F32), pltpu.VMEM((de, d), F32),
                            pltpu.SemaphoreType.DMA((2,)), pltpu.SemaphoreType.DMA((3,)),
                            pltpu.SemaphoreType.DMA((2,))]),
        compiler_params=_cparams(("arbitrary",)),
        name="experts",
    )(blk_e, blk_cnt, n_used, row_src, h2, w1, w3, w2)


def _combine_kernel(dest_ref, ys_hbm, x_ref, wt_ref, gt_ref, o_ref, ybuf, sem):
    i = pl.program_id(0)
    nb = pl.num_programs(0)
    tm = x_ref.shape[0]
    n_tok = tm * nb

    def issue(blk_i, slot):
        for k in range(2):
            _gather_rows(lambda r: dest_ref[k * n_tok + blk_i * tm + r], tm, ys_hbm, ybuf.at[slot, k],
                         sem.at[slot])

    slot = lax.rem(i, 2)

    @pl.when(i == 0)
    def _():
        issue(0, 0)

    @pl.when(i + 1 < nb)
    def _():
        issue(i + 1, 1 - slot)

    for k in range(2):
        _wait_rows(tm, ys_hbm, ybuf.at[slot, k], sem.at[slot])
    w = wt_ref[...]
    y = ybuf[slot, 0] * w[:, 0:1] + ybuf[slot, 1] * w[:, 1:2]
    o_ref[...] = x_ref[...] + gt_ref[...] * y


def _combine(dest, ys, x2, wts, gt2):
    s, d = x2.shape
    tm = min(256, s)
    return pl.pallas_call(
        _combine_kernel,
        out_shape=jax.ShapeDtypeStruct((s, d), F32),
        grid_spec=pltpu.PrefetchScalarGridSpec(
            num_scalar_prefetch=1,
            grid=(s // tm,),
            in_specs=[pl.BlockSpec(memory_space=pl.ANY),
                      pl.BlockSpec((tm, d), lambda i, ds_: (i, 0)),
                      pl.BlockSpec((tm, LANES), lambda i, ds_: (i, 0)),
                      pl.BlockSpec((1, d), lambda i, ds_: (0, 0))],
            out_specs=pl.BlockSpec((tm, d), lambda i, ds_: (i, 0)),
            scratch_shapes=[pltpu.VMEM((2, 2, tm, d), F32), pltpu.SemaphoreType.DMA((2,))]),
        compiler_params=_cparams(("arbitrary",)),
        name="combine",
    )(dest, ys, x2, wts, gt2)


def _layer(x2d, c, w_ada, b_ada, g_mix, g_ffn, w_in, conv_w, q_norm_g, k_norm_g, g_branch, w_out,
           w_group, w_expert, w1, w3, w2):
    s, d = x2d.shape
    conv_ch = conv_w.shape[1]
    attn_w = d - conv_ch
    n_heads = attn_w // HEAD_DIM
    na = s // RES
    assert s % (RES * BAND) == 0
    row = lambda v: v.reshape(1, -1)

    mod = _ada(c, w_ada, b_ada)
    sh1, sc1, gt1, sh2, sc2, gt2 = [mod[:, i * d:(i + 1) * d] for i in range(6)]

    h1 = _norm1(x2d, row(g_mix), sc1, sh1)
    z = _inproj(h1.reshape(s, d), w_in, row(q_norm_g), row(k_norm_g), conv_ch, attn_w)
    z3 = z.reshape(RES, na, z.shape[1])
    y_conv = _conv(z3, conv_w, row(g_branch[:conv_ch]), conv_ch)
    slopes = jnp.exp2(-ALIBI_MAX_EXP * jnp.arange(1, n_heads + 1, dtype=F32) / n_heads)
    y_attn = _attention(z3, slopes, conv_ch, attn_w)
    x2 = _outproj(y_conv, y_attn, row(g_branch[conv_ch:]), w_out.astype(BF16), x2d, gt1)

    w_r = jnp.concatenate([w_group, w_expert,
                           jnp.zeros((d, LANES - N_GROUPS - N_EXPERTS), F32)], axis=1)
    h2, eid, wts = _router(x2, row(g_ffn), sc2, sh2, w_r)
    rank, counts = _rank(eid)
    eid_flat = eid[:, :2].T.reshape(-1)
    rank_flat = rank[:, :2].T.reshape(-1)
    cnt = counts[0, :N_EXPERTS].astype(jnp.int32)
    dest, row_src, blk_e, blk_cnt, n_used = _plan(eid_flat, rank_flat, cnt, MOE_BLOCK)
    ys = _experts(h2, w1, w3, w2, blk_e, blk_cnt, n_used, row_src, MOE_BLOCK)
    return _combine(dest, ys, x2, wts, gt2)


def kernel(x, c, w_ada, b_ada, g_mix, g_ffn, w_in, conv_w, q_norm_g, k_norm_g, g_branch, w_out,
           w_group, w_expert, w1, w3, w2):
    b, s, d = x.shape
    assert b == 1 and c.shape[0] == 1
    for l in range(w_ada.shape[0]):
        x = _layer(x.reshape(s, d), c[0], w_ada[l], b_ada[l], g_mix[l], g_ffn[l], w_in[l], conv_w[l],
                   q_norm_g[l], k_norm_g[l], g_branch[l], w_out[l], w_group[l], w_expert[l],
                   w1[l], w3[l], w2[l]).reshape(b, s, d)
    return x
```

```python
import functools

import jax
import jax.numpy as jnp
from jax import lax
from jax.experimental import pallas as pl
from jax.experimental.pallas import tpu as pltpu

EPS = 1e-6
HEAD_DIM = 128
BAND = 128
PATTERNS = ((128, 1), (512, 4), (2048, 16))
RES = 16
ALIBI_MAX_EXP = 8.0
N_GROUPS = 8
EXPERTS_PER_GROUP = 8
N_EXPERTS = N_GROUPS * EXPERTS_PER_GROUP
MOE_BLOCK = 256
GATHER_GROUP = 8
Y_TILE = 512
OUT_CHUNK = 64
ADA_SLAB = 256
LANES = 128
SUBLANES = 8
VMEM_LIMIT = 56 * 1024 * 1024
F32 = jnp.float32
BF16 = jnp.bfloat16
NEG_INF = float("-inf")


def _cparams(sem):
    return pltpu.CompilerParams(dimension_semantics=sem, vmem_limit_bytes=VMEM_LIMIT)


def _ada_kernel(c_ref, w_ref, b_ref, o_ref, s_ref, *, rows):
    @pl.when(pl.program_id(0) == 0)
    def _():
        c = c_ref[...]
        s_ref[...] = c * jax.nn.sigmoid(c)

    d, tn = w_ref.shape

    def body(i, acc):
        r0 = pl.multiple_of(i * rows, rows)
        p = w_ref[pl.ds(r0, rows), :] * s_ref[pl.ds(r0, rows), :]
        return acc + jnp.sum(p.reshape(rows // SUBLANES, SUBLANES, tn), axis=0)

    acc = lax.fori_loop(0, d // rows, body, jnp.zeros((SUBLANES, tn), F32))
    o_ref[...] = jnp.sum(acc, axis=0, keepdims=True) + b_ref[...]


def _ada_slab(w_ref, s_ref, b_ref):
    d, tn = w_ref.shape
    rows = min(512, d)
    acc = jnp.zeros((SUBLANES, tn), F32)
    for r0 in range(0, d, rows):
        p = w_ref[r0:r0 + rows, :] * s_ref[r0:r0 + rows, :]
        acc = acc + jnp.sum(p.reshape(rows // SUBLANES, SUBLANES, tn), axis=0)
    return jnp.sum(acc, axis=0, keepdims=True) + b_ref[...]


def _ada(c, w_ada, b_ada, n):
    d = w_ada.shape[0]
    tn = min(512, n)
    rows = min(512, d)
    return pl.pallas_call(
        functools.partial(_ada_kernel, rows=rows),
        out_shape=jax.ShapeDtypeStruct((1, n), F32),
        grid=(n // tn,),
        in_specs=[pl.BlockSpec((d, 1), lambda j: (0, 0)),
                  pl.BlockSpec((d, tn), lambda j: (0, j)),
                  pl.BlockSpec((1, tn), lambda j: (0, j))],
        out_specs=pl.BlockSpec((1, tn), lambda j: (0, j)),
        scratch_shapes=[pltpu.VMEM((d, 1), F32)],
        compiler_params=_cparams(("arbitrary",)),
        name="ada",
    )(c.reshape(d, 1), w_ada, b_ada.reshape(1, -1))


def _rms_mod(x, g, sc, sh):
    ms = jnp.mean(x * x, axis=-1, keepdims=True)
    return (x * lax.rsqrt(ms + EPS) * g) * (1.0 + sc) + sh


def _row_permutation(ta, to_residue_major):
    n = RES * ta
    assert ta & (ta - 1) == 0
    sh = ta.bit_length() - 1
    r = lax.broadcasted_iota(jnp.int32, (n, n), 0)
    c = lax.broadcasted_iota(jnp.int32, (n, n), 1)
    res_major, natural = (r, c) if to_residue_major else (c, r)
    return (natural == RES * (res_major & (ta - 1)) + (res_major >> sh)).astype(BF16)


def _norm1_kernel(x_ref, g_ref, sc_ref, sh_ref, o_ref):
    ta = o_ref.shape[1]
    h = _rms_mod(x_ref[...], g_ref[...], sc_ref[...], sh_ref[...]).astype(BF16)
    hp = jnp.dot(_row_permutation(ta, True), h, preferred_element_type=F32)
    o_ref[...] = hp.astype(BF16).reshape(o_ref.shape)


def _norm1(x2d, g_mix, sc1, sh1):
    s, d = x2d.shape
    na = s // RES
    ta = min(32, na)
    vec = lambda: pl.BlockSpec((1, d), lambda i: (0, 0))
    return pl.pallas_call(
        _norm1_kernel,
        out_shape=jax.ShapeDtypeStruct((RES, na, d), BF16),
        grid=(na // ta,),
        in_specs=[pl.BlockSpec((RES * ta, d), lambda i: (i, 0)), vec(), vec(), vec()],
        out_specs=pl.BlockSpec((RES, ta, d), lambda i: (0, i, 0)),
        compiler_params=_cparams(("arbitrary",)),
        name="norm1",
    )(x2d, g_mix, sc1, sh1)


def _inproj_kernel(h_ref, w_ref, qg_ref, kg_ref, c_ref, wa_ref, ba_ref, z_ref, mod_ref, wb_ref, s_ref,
                   *, q0, k0, v0):
    j = pl.program_id(0)

    @pl.when((j == 0) & (pl.program_id(1) == 0))
    def _():
        c = c_ref[...]
        s_ref[...] = c * jax.nn.sigmoid(c)

    @pl.when(pl.program_id(1) == 0)
    def _():
        wb_ref[...] = w_ref[...].astype(BF16)

    mod_ref[...] = _ada_slab(wa_ref, s_ref, ba_ref)
    acc = jnp.dot(h_ref[...], wb_ref[...], preferred_element_type=F32)
    is_q = (j >= q0) & (j < k0)
    is_k = (j >= k0) & (j < v0)

    g = jnp.where(is_q, qg_ref[...] * (HEAD_DIM ** -0.5), kg_ref[...])
    for hh in range(acc.shape[1] // HEAD_DIM):
        a = acc[:, hh * HEAD_DIM:(hh + 1) * HEAD_DIM]
        ms = jnp.mean(a * a, axis=-1, keepdims=True)
        scale = jnp.where(is_q | is_k, lax.rsqrt(ms + EPS) * g, 1.0)
        z_ref[:, hh * HEAD_DIM:(hh + 1) * HEAD_DIM] = a * scale


def _inproj(h1, w_in, q_g, k_g, conv_ch, attn_w, c, w_ada, b_ada, mod_done):
    s, d = h1.shape
    n = w_in.shape[1]
    tm = min(1024, s)
    tn = min(512, conv_ch)
    assert conv_ch % tn == 0 and attn_w % tn == 0 and tn % HEAD_DIM == 0
    q0 = 3 * conv_ch // tn
    k0 = q0 + attn_w // tn
    v0 = k0 + attn_w // tn
    ni = s // tm
    n_mod = w_ada.shape[1] - mod_done
    n_slab = n_mod // ADA_SLAB
    assert n_mod % ADA_SLAB == 0 and mod_done % ADA_SLAB == 0 and n_slab <= (n // tn) * ni
    slab = lambda j, i: jnp.minimum(j * ni + i, n_slab - 1)
    hvec = lambda: pl.BlockSpec((1, HEAD_DIM), lambda j, i: (0, 0))
    return pl.pallas_call(
        functools.partial(_inproj_kernel, q0=q0, k0=k0, v0=v0),
        out_shape=(jax.ShapeDtypeStruct((s, n), F32), jax.ShapeDtypeStruct((1, n_mod), F32)),
        grid=(n // tn, ni),
        in_specs=[pl.BlockSpec((tm, d), lambda j, i: (i, 0)),
                  pl.BlockSpec((d, tn), lambda j, i: (0, j)), hvec(), hvec(),
                  pl.BlockSpec((d, 1), lambda j, i: (0, 0)),
                  pl.BlockSpec((d, ADA_SLAB), lambda j, i: (0, mod_done // ADA_SLAB + slab(j, i))),
                  pl.BlockSpec((1, ADA_SLAB), lambda j, i: (0, mod_done // ADA_SLAB + slab(j, i)))],
        out_specs=(pl.BlockSpec((tm, tn), lambda j, i: (i, j)),
                   pl.BlockSpec((1, ADA_SLAB), lambda j, i: (0, slab(j, i)))),
        scratch_shapes=[pltpu.VMEM((d, tn), BF16), pltpu.VMEM((d, 1), F32)],
        compiler_params=_cparams(("arbitrary", "arbitrary")),
        name="inproj",
    )(h1, w_in, q_g, k_g, c.reshape(d, 1), w_ada, b_ada.reshape(1, -1))


def _conv_kernel(b_ref, c_ref, v_ref, cw_ref, g_ref, o_ref, ue_ref):
    i = pl.program_id(0)
    ta = b_ref.shape[1]
    w = cw_ref[...]
    assert w.shape[0] == 3
    for k in range(2):
        @pl.when(i == 0)
        def _():
            ue_ref[k, 0:SUBLANES, :] = jnp.zeros((SUBLANES, ue_ref.shape[2]), F32)

        @pl.when(i > 0)
        def _():
            ue_ref[k, 0:SUBLANES, :] = ue_ref[k, ta:ta + SUBLANES, :]

        r = RES - 2 + k
        ue_ref[k, SUBLANES:SUBLANES + ta, :] = c_ref[r] * v_ref[r]

    def u_at(bb):
        if bb >= 0:
            return c_ref[bb] * v_ref[bb]
        return ue_ref[2 + bb, SUBLANES - 1:SUBLANES - 1 + ta, :]

    for b in range(RES):
        y = b_ref[b] * (w[0:1, :] * u_at(b - 2) + w[1:2, :] * u_at(b - 1) + w[2:3, :] * u_at(b))
        ms = jnp.mean(y * y, axis=-1, keepdims=True)
        o_ref[b] = (y * lax.rsqrt(ms + EPS) * g_ref[...]).astype(BF16)


def _conv(z3, conv_w, g_conv, conv_ch):
    na = z3.shape[1]
    ta = min(64, na)
    blk = lambda jj: pl.BlockSpec((RES, ta, conv_ch), lambda i: (0, i, jj))
    return pl.pallas_call(
        _conv_kernel,
        out_shape=jax.ShapeDtypeStruct((RES, na, conv_ch), BF16),
        grid=(na // ta,),
        in_specs=[blk(0), blk(1), blk(2),
                  pl.BlockSpec(conv_w.shape, lambda i: (0, 0)),
                  pl.BlockSpec((1, conv_ch), lambda i: (0, 0))],
        out_specs=pl.BlockSpec((RES, ta, conv_ch), lambda i: (0, i, 0)),
        scratch_shapes=[pltpu.VMEM((2, ta + SUBLANES, conv_ch), F32)],
        compiler_params=_cparams(("arbitrary",)),
        name="conv",
    )(z3, z3, z3, conv_w, g_conv)


def _attn_kernel(slopes_ref, q_ref, kp_ref, kc_ref, vp_ref, vc_ref, o_ref, bias_ref, m_ref, l_ref, acc_ref):
    h = pl.program_id(0)
    cidx = pl.program_id(1)
    slope = slopes_ref[h]
    first_pen = jnp.where(cidx > 0, 0.0, NEG_INF).astype(F32)
    dn_t = (((1,), (1,)), ((), ()))

    j = lax.broadcasted_iota(jnp.int32, (BAND, 2 * BAND), 0)
    u = lax.broadcasted_iota(jnp.int32, (BAND, 2 * BAND), 1)
    is_prev = u < BAND
    uu = jnp.where(is_prev, u, u - BAND)
    for pi, (window, dil) in enumerate(PATTERNS):
        assert window // dil == BAND and RES % dil == 0
        ns, sr = RES // dil, BAND * dil // RES
        assert sr & (sr - 1) == 0
        sh = sr.bit_length() - 1
        mq = (j & (sr - 1)) * ns + (j >> sh)
        mk = (uu & (sr - 1)) * ns + (uu >> sh)
        steps = mq - mk + jnp.where(is_prev, BAND, 0)
        valid = (is_prev & (steps <= BAND)) | (jnp.logical_not(is_prev) & (steps >= 0))
        bias = jnp.where(valid, -slope * (steps * dil).astype(F32), NEG_INF)
        bias_ref[pi, 0] = bias
        bias_ref[pi, 1] = bias + jnp.where(is_prev, first_pen, 0.0)

    for pi, (window, dil) in enumerate(PATTERNS):
        ns, sr = RES // dil, BAND * dil // RES
        n_chunks = BAND // sr
        for n in range(n_chunks):
            for r in range(dil):
                def gather(ref, nn):
                    return jnp.concatenate([ref[dil * c + r, nn * sr:(nn + 1) * sr, :] for c in range(ns)], axis=0)

                def scatter(ref, val):
                    for c in range(ns):
                        ref[dil * c + r, n * sr:(n + 1) * sr, :] = val[c * sr:(c + 1) * sr, :]

                q = gather(q_ref, n).astype(BF16)
                if n > 0:
                    k_prev, v_prev = gather(kc_ref, n - 1), gather(vc_ref, n - 1)
                else:
                    k_prev, v_prev = gather(kp_ref, n_chunks - 1), gather(vp_ref, n_chunks - 1)
                kcat = jnp.concatenate([k_prev, gather(kc_ref, n)], axis=0).astype(BF16)
                vcat = jnp.concatenate([v_prev, gather(vc_ref, n)], axis=0).astype(BF16)
                s = lax.dot_general(q, kcat, dn_t, preferred_element_type=F32) + bias_ref[pi, 1 if n == 0 else 0]
                m = jnp.max(s, axis=-1, keepdims=True)
                p = jnp.exp(s - m)
                l = jnp.sum(p, axis=-1, keepdims=True)
                o = jnp.dot(p.astype(BF16), vcat, preferred_element_type=F32)
                m_b = jnp.broadcast_to(m, (BAND, HEAD_DIM))
                l_b = jnp.broadcast_to(l, (BAND, HEAD_DIM))
                if pi == 0:
                    scatter(m_ref, m_b)
                    scatter(l_ref, l_b)
                    scatter(acc_ref, o)
                else:
                    m_old = gather(m_ref, n)
                    m_new = jnp.maximum(m_old, m_b)
                    a_old = jnp.exp(m_old - m_new)
                    a_new = jnp.exp(m_b - m_new)
                    l_new = a_old * gather(l_ref, n) + a_new * l_b
                    o_new = a_old * gather(acc_ref, n) + a_new * o
                    if pi == len(PATTERNS) - 1:
                        scatter(acc_ref, o_new / l_new)
                    else:
                        scatter(m_ref, m_new)
                        scatter(l_ref, l_new)
                        scatter(acc_ref, o_new)
    o_ref[...] = acc_ref[...].astype(BF16)


def _attention(z3, slopes, conv_ch, attn_w):
    na = z3.shape[1]
    n_heads = attn_w // HEAD_DIM
    assert na % BAND == 0
    qb = 3 * conv_ch // HEAD_DIM
    kb = qb + n_heads
    vb = kb + n_heads
    blk = (RES, BAND, HEAD_DIM)
    cur = lambda base: pl.BlockSpec(blk, lambda h, c, sl: (0, c, base + h))
    prev = lambda base: pl.BlockSpec(blk, lambda h, c, sl: (0, jnp.maximum(c - 1, 0), base + h))
    return pl.pallas_call(
        _attn_kernel,
        out_shape=jax.ShapeDtypeStruct((RES, na, attn_w), BF16),
        grid_spec=pltpu.PrefetchScalarGridSpec(
            num_scalar_prefetch=1,
            grid=(n_heads, na // BAND),
            in_specs=[cur(qb), prev(kb), cur(kb), prev(vb), cur(vb)],
            out_specs=pl.BlockSpec(blk, lambda h, c, sl: (0, c, h)),
            scratch_shapes=[pltpu.VMEM((len(PATTERNS), 2, BAND, 2 * BAND), F32),
                            pltpu.VMEM(blk, F32), pltpu.VMEM(blk, F32), pltpu.VMEM(blk, F32)]),
        compiler_params=_cparams(("arbitrary", "arbitrary")),
        name="attn",
    )(slopes, z3, z3, z3, z3, z3)


def _outproj_kernel(yc_ref, ya_ref, ga_ref, w_ref, x_ref, gt_ref, o_ref, y_ref):
    _, ta, cc = yc_ref.shape

    @pl.when(pl.program_id(1) == 0)
    def _():
        n = RES * ta
        ya = ya_ref[...].reshape(n, ya_ref.shape[2]).astype(F32)
        ms = jnp.mean(ya * ya, axis=-1, keepdims=True)
        yn = (ya * lax.rsqrt(ms + EPS) * ga_ref[...]).astype(BF16)
        perm = _row_permutation(ta, False)
        y_ref[:, 0:cc] = jnp.dot(perm, yc_ref[...].reshape(n, cc), preferred_element_type=F32).astype(BF16)
        y_ref[:, cc:] = jnp.dot(perm, yn, preferred_element_type=F32).astype(BF16)

    acc = jnp.dot(y_ref[...], w_ref[...], preferred_element_type=F32)
    o_ref[...] = x_ref[...] + gt_ref[...] * acc


def _outproj(y_conv3, y_attn3, g_attn, w_out_bf, x2d, gt1):
    s, d = x2d.shape
    na = s // RES
    cc, aw = y_conv3.shape[2], y_attn3.shape[2]
    ta = min(32, na)
    tn = min(1024, d)
    return pl.pallas_call(
        _outproj_kernel,
        out_shape=jax.ShapeDtypeStruct((s, d), F32),
        grid=(na // ta, d // tn),
        in_specs=[pl.BlockSpec((RES, ta, cc), lambda i, j: (0, i, 0)),
                  pl.BlockSpec((RES, ta, aw), lambda i, j: (0, i, 0)),
                  pl.BlockSpec((1, aw), lambda i, j: (0, 0)),
                  pl.BlockSpec((cc + aw, tn), lambda i, j: (0, j)),
                  pl.BlockSpec((RES * ta, tn), lambda i, j: (i, j)),
                  pl.BlockSpec((1, tn), lambda i, j: (0, j))],
        out_specs=pl.BlockSpec((RES * ta, tn), lambda i, j: (i, j)),
        scratch_shapes=[pltpu.VMEM((RES * ta, cc + aw), BF16)],
        compiler_params=_cparams(("arbitrary", "arbitrary")),
        name="outproj",
    )(y_conv3, y_attn3, g_attn, w_out_bf, x2d, gt1)


def _router_kernel(x_ref, g_ref, sc_ref, sh_ref, wr_ref, h_ref, eid_ref, wt_ref):
    h = _rms_mod(x_ref[...], g_ref[...], sc_ref[...], sh_ref[...])
    h_ref[...] = h
    w = wr_ref[...]
    h_hi, w_hi = h.astype(BF16), w.astype(BF16)
    h_lo = (h - h_hi.astype(F32)).astype(BF16)
    w_lo = (w - w_hi.astype(F32)).astype(BF16)
    logits = (jnp.dot(h_hi, w_hi, preferred_element_type=F32) + jnp.dot(h_hi, w_lo, preferred_element_type=F32)
              + jnp.dot(h_lo, w_hi, preferred_element_type=F32))
    tm = logits.shape[0]
    lane = lax.broadcasted_iota(jnp.int32, (tm, LANES), 1).astype(F32)
    big = float(LANES)
    is_g = lane < N_GROUPS
    gl = jnp.where(is_g, logits, NEG_INF)
    gmax = jnp.max(gl, axis=-1, keepdims=True)
    gsel = jnp.min(jnp.where(gl == gmax, lane, big), axis=-1, keepdims=True)
    g_w = 1.0 / jnp.sum(jnp.where(is_g, jnp.exp(gl - gmax), 0.0), axis=-1, keepdims=True)
    lo = N_GROUPS + gsel * EXPERTS_PER_GROUP
    in_grp = (lane >= lo) & (lane < lo + EXPERTS_PER_GROUP)
    el = jnp.where(in_grp, logits, NEG_INF)
    e1 = jnp.max(el, axis=-1, keepdims=True)
    i1 = jnp.min(jnp.where(el == e1, lane, big), axis=-1, keepdims=True)
    el2 = jnp.where(lane == i1, NEG_INF, el)
    e2 = jnp.max(el2, axis=-1, keepdims=True)
    i2 = jnp.min(jnp.where(el2 == e2, lane, big), axis=-1, keepdims=True)
    t = jnp.exp(e2 - e1)
    w1 = g_w / (1.0 + t)
    w2 = g_w * t / (1.0 + t)
    eid = jnp.where(lane == 0.0, i1 - N_GROUPS, jnp.where(lane == 1.0, i2 - N_GROUPS, 0.0))
    eid_ref[...] = eid.astype(jnp.int32)
    wt_ref[...] = jnp.where(lane == 0.0, w1, jnp.where(lane == 1.0, w2, 0.0))


def _router(x2, g_ffn, sc2, sh2, w_r):
    s, d = x2.shape
    tm = min(512, s)
    vec = lambda: pl.BlockSpec((1, d), lambda i: (0, 0))
    lanes_out = lambda: pl.BlockSpec((tm, LANES), lambda i: (i, 0))
    return pl.pallas_call(
        _router_kernel,
        out_shape=(jax.ShapeDtypeStruct((s, d), F32),
                   jax.ShapeDtypeStruct((s, LANES), jnp.int32),
                   jax.ShapeDtypeStruct((s, LANES), F32)),
        grid=(s // tm,),
        in_specs=[pl.BlockSpec((tm, d), lambda i: (i, 0)), vec(), vec(), vec(),
                  pl.BlockSpec((d, LANES), lambda i: (0, 0))],
        out_specs=(pl.BlockSpec((tm, d), lambda i: (i, 0)), lanes_out(), lanes_out()),
        compiler_params=_cparams(("arbitrary",)),
        name="router",
    )(x2, g_ffn, sc2, sh2, w_r)


def _rank_kernel(eid_ref, rank_ref, cnt_ref, run_ref):
    @pl.when(pl.program_id(0) == 0)
    def _():
        run_ref[...] = jnp.zeros_like(run_ref)

    tb = eid_ref.shape[0]
    lane = lax.broadcasted_iota(jnp.int32, (tb, LANES), 1)
    row = lax.broadcasted_iota(jnp.int32, (tb, tb), 0)
    col = lax.broadcasted_iota(jnp.int32, (tb, tb), 1)
    earlier = (col < row).astype(BF16)
    e = eid_ref[...]
    run = run_ref[...]
    ranks = []
    for k in range(2):
        onehot = e[:, k:k + 1] == lane
        oh = onehot.astype(F32)
        before = jnp.dot(earlier, oh.astype(BF16), preferred_element_type=F32) + run
        ranks.append(jnp.sum(jnp.where(onehot, before, 0.0), axis=-1, keepdims=True))
        run = run + jnp.sum(oh, axis=0, keepdims=True)
    rank_ref[...] = jnp.where(lane == 0, ranks[0], jnp.where(lane == 1, ranks[1], 0.0)).astype(jnp.int32)
    run_ref[...] = run
    cnt_ref[...] = run


def _rank(eid):
    s = eid.shape[0]
    tb = min(512, s)
    return pl.pallas_call(
        _rank_kernel,
        out_shape=(jax.ShapeDtypeStruct((s, LANES), jnp.int32), jax.ShapeDtypeStruct((1, LANES), F32)),
        grid=(s // tb,),
        in_specs=[pl.BlockSpec((tb, LANES), lambda i: (i, 0))],
        out_specs=(pl.BlockSpec((tb, LANES), lambda i: (i, 0)), pl.BlockSpec((1, LANES), lambda i: (0, 0))),
        scratch_shapes=[pltpu.VMEM((1, LANES), F32)],
        compiler_params=_cparams(("arbitrary",)),
        name="rank",
    )(eid)


def _plan_kernel(eid_ref, rank_ref, cnt_ref, dest_ref, src_ref, blke_ref, blkc_ref, nused_ref, pst_ref,
                 *, blk, n_blk):
    def per_expert(e, nb):
        c = cnt_ref[e]
        pst_ref[e] = nb * blk
        nbe = (c + (blk - 1)) // blk

        def per_block(jb, carry):
            blke_ref[nb + jb] = e
            blkc_ref[nb + jb] = jnp.minimum(blk, c - jb * blk)
            return carry

        lax.fori_loop(0, nbe, per_block, 0)

        def pad_slot(r, carry):
            src_ref[nb * blk + r] = 0
            return carry

        lax.fori_loop(c, (c + (GATHER_GROUP - 1)) // GATHER_GROUP * GATHER_GROUP, pad_slot, 0)
        return nb + nbe

    n_used = lax.fori_loop(0, N_EXPERTS, per_expert, jnp.int32(0))
    nused_ref[0] = n_used
    last_e = blke_ref[n_used - 1]

    def unused(i, carry):
        blke_ref[i] = last_e
        blkc_ref[i] = 0
        return carry

    lax.fori_loop(n_used, n_blk, unused, 0)

    n_tok = eid_ref.shape[0] // 2

    def place(a, carry):
        dst = pst_ref[eid_ref[a]] + rank_ref[a]
        dest_ref[a] = dst
        src_ref[dst] = jnp.where(a >= n_tok, a - n_tok, a)
        return carry

    lax.fori_loop(0, eid_ref.shape[0], place, 0, unroll=8)


def _plan(eid_flat, rank_flat, counts, blk):
    a = eid_flat.shape[0]
    n_blk = a // blk + N_EXPERTS
    smem = lambda: pl.BlockSpec(memory_space=pltpu.SMEM)
    i32 = jnp.int32
    return pl.pallas_call(
        functools.partial(_plan_kernel, blk=blk, n_blk=n_blk),
        out_shape=(jax.ShapeDtypeStruct((a,), i32), jax.ShapeDtypeStruct((n_blk * blk,), i32),
                   jax.ShapeDtypeStruct((n_blk,), i32), jax.ShapeDtypeStruct((n_blk,), i32),
                   jax.ShapeDtypeStruct((1,), i32)),
        in_specs=[smem(), smem(), smem()],
        out_specs=(smem(), smem(), smem(), smem(), smem()),
        scratch_shapes=[pltpu.SMEM((N_EXPERTS,), i32)],
        name="plan",
    )(eid_flat, rank_flat, counts)


def _gather_rows(idx_of_row, n_rows, src_hbm, dst, sem):
    def body(g, carry):
        for rr in range(GATHER_GROUP):
            r = g * GATHER_GROUP + rr
            pltpu.make_async_copy(src_hbm.at[pl.ds(idx_of_row(r), 1), :], dst.at[pl.ds(r, 1), :], sem).start()
        return carry
    lax.fori_loop(0, (n_rows + (GATHER_GROUP - 1)) // GATHER_GROUP, body, 0)


def _wait_rows(n_rows, src_hbm, dst, sem):
    def body(g, carry):
        pltpu.make_async_copy(src_hbm.at[pl.ds(0, GATHER_GROUP), :], dst.at[pl.ds(0, GATHER_GROUP), :], sem).wait()
        return carry
    lax.fori_loop(0, (n_rows + (GATHER_GROUP - 1)) // GATHER_GROUP, body, 0)


def _expert_kernel(blke_ref, blkc_ref, nused_ref, src_ref, h_hbm, w1_hbm, w3_hbm, w2_hbm, ys_hbm,
                   xbuf, obuf, w1b, w3b, w2b, gsem, wsem, osem, *, blk, kc):
    i = pl.program_id(0)
    last = pl.num_programs(0) - 1
    n_used = nused_ref[0]
    e = blke_ref[i]
    e_next = blke_ref[jnp.minimum(i + 1, last)]
    first_of_expert = (i == 0) | (blke_ref[jnp.maximum(i - 1, 0)] != e)
    fetch_next = (i + 1 < n_used) & (e_next != e)
    wbufs = ((w1_hbm, w1b), (w3_hbm, w3b), (w2_hbm, w2b))

    def weight_copy(k, ee):
        return pltpu.make_async_copy(wbufs[k][0].at[ee], wbufs[k][1], wsem.at[k])

    def issue_rows(b, slot):
        _gather_rows(lambda r: src_ref[b * blk + r], blkc_ref[b], h_hbm, xbuf.at[slot], gsem.at[slot])

    def weights_ready(k):
        @pl.when(first_of_expert)
        def _():
            weight_copy(k, e).wait()

    def weights_done(k):
        @pl.when(fetch_next)
        def _():
            weight_copy(k, e_next).start()

    def out_copy(b, slot, ch):
        return pltpu.make_async_copy(obuf.at[slot, pl.ds(ch * OUT_CHUNK, OUT_CHUNK), :],
                                     ys_hbm.at[pl.ds(b * blk + ch * OUT_CHUNK, OUT_CHUNK), :], osem.at[slot])

    def for_out_chunks(b, fn):
        def body(ch, carry):
            fn(ch)
            return carry
        lax.fori_loop(0, (blkc_ref[b] + (OUT_CHUNK - 1)) // OUT_CHUNK, body, 0)

    slot = lax.rem(i, 2)

    @pl.when(i == 0)
    def _():
        for k in range(3):
            weight_copy(k, e).start()
        xbuf[...] = jnp.zeros_like(xbuf)
        issue_rows(0, 0)

    @pl.when(i < n_used)
    def _():
        _wait_rows(blkc_ref[i], h_hbm, xbuf.at[slot], gsem.at[slot])

        @pl.when(i + 1 < n_used)
        def _():
            issue_rows(i + 1, 1 - slot)

        d = xbuf.shape[2]

        def up_proj(wb):
            acc = None
            for k0 in range(0, d, kc):
                part = jnp.dot(xbuf[slot, :, k0:k0 + kc].astype(BF16), wb[k0:k0 + kc, :].astype(BF16),
                               preferred_element_type=F32)
                acc = part if acc is None else acc + part
            return acc

        weights_ready(0)
        a = up_proj(w1b)
        weights_done(0)
        weights_ready(1)
        b = up_proj(w3b)
        weights_done(1)
        hid = ((a * jax.nn.sigmoid(a)) * b).astype(BF16)
        weights_ready(2)

        @pl.when(i >= 2)
        def _():
            for_out_chunks(i - 2, lambda ch: out_copy(i - 2, slot, 0).wait())

        for t in range(0, d, Y_TILE):
            obuf[slot, :, t:t + Y_TILE] = jnp.dot(hid, w2b[:, t:t + Y_TILE].astype(BF16),
                                                  preferred_element_type=F32)
        weights_done(2)
        for_out_chunks(i, lambda ch: out_copy(i, slot, ch).start())

        @pl.when(i == n_used - 1)
        def _():
            @pl.when(i >= 1)
            def _():
                for_out_chunks(i - 1, lambda ch: out_copy(i - 1, 1 - slot, 0).wait())
            for_out_chunks(i, lambda ch: out_copy(i, slot, 0).wait())


def _experts(h2, w1, w3, w2, blk_e, blk_cnt, n_used, row_src, blk):
    s, d = h2.shape
    de = w1.shape[-1]
    kc = min(1024, d)
    assert d % kc == 0 and d % Y_TILE == 0 and blk % GATHER_GROUP == 0 and blk % OUT_CHUNK == 0
    n_blk = blk_e.shape[0]
    anyspace = lambda: pl.BlockSpec(memory_space=pl.ANY)
    return pl.pallas_call(
        functools.partial(_expert_kernel, blk=blk, kc=kc),
        out_shape=jax.ShapeDtypeStruct((n_blk * blk, d), F32),
        grid_spec=pltpu.PrefetchScalarGridSpec(
            num_scalar_prefetch=4,
            grid=(n_blk,),
            in_specs=[anyspace(), anyspace(), anyspace(), anyspace()],
            out_specs=anyspace(),
            scratch_shapes=[pltpu.VMEM((2, blk, d), F32), pltpu.VMEM((2, blk, d), F32),
                            pltpu.VMEM((d, de), F32), pltpu.VMEM((d, de), F32), pltpu.VMEM((de, d), F32),
                            pltpu.SemaphoreType.DMA((2,)), pltpu.SemaphoreType.DMA((3,)),
                            pltpu.SemaphoreType.DMA((2,))]),
        compiler_params=_cparams(("arbitrary",)),
        name="experts",
    )(blk_e, blk_cnt, n_used, row_src, h2, w1, w3, w2)


def _combine_kernel(dest_ref, ys_hbm, x_ref, wt_ref, gt_ref, o_ref, ybuf, sem):
    i = pl.program_id(0)
    nb = pl.num_programs(0)
    tm = x_ref.shape[0]
    n_tok = tm * nb

    def issue(blk_i, slot):
        for k in range(2):
            _gather_rows(lambda r: dest_ref[k * n_tok + blk_i * tm + r], tm, ys_hbm, ybuf.at[slot, k],
                         sem.at[slot])

    slot = lax.rem(i, 2)

    @pl.when(i == 0)
    def _():
        issue(0, 0)

    @pl.when(i + 1 < nb)
    def _():
        issue(i + 1, 1 - slot)

    for k in range(2):
        _wait_rows(tm, ys_hbm, ybuf.at[slot, k], sem.at[slot])
    w = wt_ref[...]
    y = ybuf[slot, 0] * w[:, 0:1] + ybuf[slot, 1] * w[:, 1:2]
    o_ref[...] = x_ref[...] + gt_ref[...] * y


def _combine(dest, ys, x2, wts, gt2):
    s, d = x2.shape
    tm = min(256, s)
    return pl.pallas_call(
        _combine_kernel,
        out_shape=jax.ShapeDtypeStruct((s, d), F32),
        grid_spec=pltpu.PrefetchScalarGridSpec(
            num_scalar_prefetch=1,
            grid=(s // tm,),
            in_specs=[pl.BlockSpec(memory_space=pl.ANY),
                      pl.BlockSpec((tm, d), lambda i, ds_: (i, 0)),
                      pl.BlockSpec((tm, LANES), lambda i, ds_: (i, 0)),
                      pl.BlockSpec((1, d), lambda i, ds_: (0, 0))],
            out_specs=pl.BlockSpec((tm, d), lambda i, ds_: (i, 0)),
            scratch_shapes=[pltpu.VMEM((2, 2, tm, d), F32), pltpu.SemaphoreType.DMA((2,))]),
        compiler_params=_cparams(("arbitrary",)),
        name="combine",
    )(dest, ys, x2, wts, gt2)


def _layer(x2d, c, w_ada, b_ada, g_mix, g_ffn, w_in, conv_w, q_norm_g, k_norm_g, g_branch, w_out,
           w_group, w_expert, w1, w3, w2):
    s, d = x2d.shape
    conv_ch = conv_w.shape[1]
    attn_w = d - conv_ch
    n_heads = attn_w // HEAD_DIM
    na = s // RES
    assert s % (RES * BAND) == 0
    row = lambda v: v.reshape(1, -1)

    mod_a = _ada(c, w_ada, b_ada, 2 * d)
    sh1, sc1 = mod_a[:, :d], mod_a[:, d:]

    h1 = _norm1(x2d, row(g_mix), sc1, sh1)
    z, mod_b = _inproj(h1.reshape(s, d), w_in, row(q_norm_g), row(k_norm_g), conv_ch, attn_w,
                       c, w_ada, b_ada, 2 * d)
    gt1, sh2, sc2, gt2 = [mod_b[:, i * d:(i + 1) * d] for i in range(4)]
    z3 = z.reshape(RES, na, z.shape[1])
    y_conv = _conv(z3, conv_w, row(g_branch[:conv_ch]), conv_ch)
    slopes = jnp.exp2(-ALIBI_MAX_EXP * jnp.arange(1, n_heads + 1, dtype=F32) / n_heads)
    y_attn = _attention(z3, slopes, conv_ch, attn_w)
    x2 = _outproj(y_conv, y_attn, row(g_branch[conv_ch:]), w_out.astype(BF16), x2d, gt1)

    w_r = jnp.concatenate([w_group, w_expert,
                           jnp.zeros((d, LANES - N_GROUPS - N_EXPERTS), F32)], axis=1)
    h2, eid, wts = _router(x2, row(g_ffn), sc2, sh2, w_r)
    rank, counts = _rank(eid)
    eid_flat = eid[:, :2].T.reshape(-1)
    rank_flat = rank[:, :2].T.reshape(-1)
    cnt = counts[0, :N_EXPERTS].astype(jnp.int32)
    dest, row_src, blk_e, blk_cnt, n_used = _plan(eid_flat, rank_flat, cnt, MOE_BLOCK)
    ys = _experts(h2, w1, w3, w2, blk_e, blk_cnt, n_used, row_src, MOE_BLOCK)
    return _combine(dest, ys, x2, wts, gt2)


def kernel(x, c, w_ada, b_ada, g_mix, g_ffn, w_in, conv_w, q_norm_g, k_norm_g, g_branch, w_out,
           w_group, w_expert, w1, w3, w2):
    b, s, d = x.shape
    assert b == 1 and c.shape[0] == 1
    for l in range(w_ada.shape[0]):
        x = _layer(x.reshape(s, d), c[0], w_ada[l], b_ada[l], g_mix[l], g_ffn[l], w_in[l], conv_w[l],
                   q_norm_g[l], k_norm_g[l], g_branch[l], w_out[l], w_group[l], w_expert[l],
                   w1[l], w3[l], w2[l]).reshape(b, s, d)
    return x
```

```python
import functools

import jax
import jax.numpy as jnp
from jax import lax
from jax.experimental import pallas as pl
from jax.experimental.pallas import tpu as pltpu

EPS = 1e-6
HEAD_DIM = 128
BAND = 128
PATTERNS = ((128, 1), (512, 4), (2048, 16))
RES = 16
ALIBI_MAX_EXP = 8.0
N_GROUPS = 8
EXPERTS_PER_GROUP = 8
N_EXPERTS = N_GROUPS * EXPERTS_PER_GROUP
MOE_BLOCK = 256
GATHER_GROUP = 8
Y_TILE = 512
OUT_CHUNK = 64
ADA_SLAB = 128
LANES = 128
SUBLANES = 8
VMEM_LIMIT = 56 * 1024 * 1024
F32 = jnp.float32
BF16 = jnp.bfloat16
NEG_INF = float("-inf")


def _cparams(sem):
    return pltpu.CompilerParams(dimension_semantics=sem, vmem_limit_bytes=VMEM_LIMIT)


def _ada_kernel(c_ref, w_ref, b_ref, o_ref, s_ref, *, rows):
    @pl.when(pl.program_id(0) == 0)
    def _():
        c = c_ref[...]
        s_ref[...] = c * jax.nn.sigmoid(c)

    d, tn = w_ref.shape

    def body(i, acc):
        r0 = pl.multiple_of(i * rows, rows)
        p = w_ref[pl.ds(r0, rows), :] * s_ref[pl.ds(r0, rows), :]
        return acc + jnp.sum(p.reshape(rows // SUBLANES, SUBLANES, tn), axis=0)

    acc = lax.fori_loop(0, d // rows, body, jnp.zeros((SUBLANES, tn), F32))
    o_ref[...] = jnp.sum(acc, axis=0, keepdims=True) + b_ref[...]


def _ada_slab(w_ref, s_ref, b_ref):
    d, tn = w_ref.shape
    rows = min(512, d)
    acc = jnp.zeros((SUBLANES, tn), F32)
    for r0 in range(0, d, rows):
        p = w_ref[r0:r0 + rows, :] * s_ref[r0:r0 + rows, :]
        acc = acc + jnp.sum(p.reshape(rows // SUBLANES, SUBLANES, tn), axis=0)
    return jnp.sum(acc, axis=0, keepdims=True) + b_ref[...]


def _ada(c, w_ada, b_ada, n):
    d = w_ada.shape[0]
    tn = min(512, n)
    rows = min(512, d)
    return pl.pallas_call(
        functools.partial(_ada_kernel, rows=rows),
        out_shape=jax.ShapeDtypeStruct((1, n), F32),
        grid=(n // tn,),
        in_specs=[pl.BlockSpec((d, 1), lambda j: (0, 0)),
                  pl.BlockSpec((d, tn), lambda j: (0, j)),
                  pl.BlockSpec((1, tn), lambda j: (0, j))],
        out_specs=pl.BlockSpec((1, tn), lambda j: (0, j)),
        scratch_shapes=[pltpu.VMEM((d, 1), F32)],
        compiler_params=_cparams(("arbitrary",)),
        name="ada",
    )(c.reshape(d, 1), w_ada, b_ada.reshape(1, -1))


def _rms_mod(x, g, sc, sh):
    ms = jnp.mean(x * x, axis=-1, keepdims=True)
    return (x * lax.rsqrt(ms + EPS) * g) * (1.0 + sc) + sh


def _row_permutation(ta, to_residue_major):
    n = RES * ta
    assert ta & (ta - 1) == 0
    sh = ta.bit_length() - 1
    r = lax.broadcasted_iota(jnp.int32, (n, n), 0)
    c = lax.broadcasted_iota(jnp.int32, (n, n), 1)
    res_major, natural = (r, c) if to_residue_major else (c, r)
    return (natural == RES * (res_major & (ta - 1)) + (res_major >> sh)).astype(BF16)


def _norm1_kernel(x_ref, g_ref, sc_ref, sh_ref, o_ref):
    ta = o_ref.shape[1]
    h = _rms_mod(x_ref[...], g_ref[...], sc_ref[...], sh_ref[...]).astype(BF16)
    hp = jnp.dot(_row_permutation(ta, True), h, preferred_element_type=F32)
    o_ref[...] = hp.astype(BF16).reshape(o_ref.shape)


def _norm1(x2d, g_mix, sc1, sh1):
    s, d = x2d.shape
    na = s // RES
    ta = min(32, na)
    vec = lambda: pl.BlockSpec((1, d), lambda i: (0, 0))
    return pl.pallas_call(
        _norm1_kernel,
        out_shape=jax.ShapeDtypeStruct((RES, na, d), BF16),
        grid=(na // ta,),
        in_specs=[pl.BlockSpec((RES * ta, d), lambda i: (i, 0)), vec(), vec(), vec()],
        out_specs=pl.BlockSpec((RES, ta, d), lambda i: (0, i, 0)),
        compiler_params=_cparams(("arbitrary",)),
        name="norm1",
    )(x2d, g_mix, sc1, sh1)


def _inproj_kernel(h_ref, w_ref, qg_ref, kg_ref, c_ref, wa_ref, ba_ref, z_ref, mod_ref, wb_ref, s_ref,
                   *, q0, k0, v0):
    j = pl.program_id(0)

    @pl.when((j == 0) & (pl.program_id(1) == 0))
    def _():
        c = c_ref[...]
        s_ref[...] = c * jax.nn.sigmoid(c)

    @pl.when(pl.program_id(1) == 0)
    def _():
        wb_ref[...] = w_ref[...].astype(BF16)

    mod_ref[...] = _ada_slab(wa_ref, s_ref, ba_ref)
    acc = jnp.dot(h_ref[...], wb_ref[...], preferred_element_type=F32)
    is_q = (j >= q0) & (j < k0)
    is_k = (j >= k0) & (j < v0)

    g = jnp.where(is_q, qg_ref[...] * (HEAD_DIM ** -0.5), kg_ref[...])
    for hh in range(acc.shape[1] // HEAD_DIM):
        a = acc[:, hh * HEAD_DIM:(hh + 1) * HEAD_DIM]
        ms = jnp.mean(a * a, axis=-1, keepdims=True)
        scale = jnp.where(is_q | is_k, lax.rsqrt(ms + EPS) * g, 1.0)
        z_ref[:, hh * HEAD_DIM:(hh + 1) * HEAD_DIM] = a * scale


def _inproj(h1, w_in, q_g, k_g, conv_ch, attn_w, c, w_ada, b_ada, mod_done):
    s, d = h1.shape
    n = w_in.shape[1]
    tm = min(1024, s)
    tn = min(512, conv_ch)
    assert conv_ch % tn == 0 and attn_w % tn == 0 and tn % HEAD_DIM == 0
    q0 = 3 * conv_ch // tn
    k0 = q0 + attn_w // tn
    v0 = k0 + attn_w // tn
    ni = s // tm
    n_mod = w_ada.shape[1] - mod_done
    n_slab = n_mod // ADA_SLAB
    assert n_mod % ADA_SLAB == 0 and mod_done % ADA_SLAB == 0 and n_slab <= (n // tn) * ni
    slab = lambda j, i: jnp.minimum(j * ni + i, n_slab - 1)
    hvec = lambda: pl.BlockSpec((1, HEAD_DIM), lambda j, i: (0, 0))
    return pl.pallas_call(
        functools.partial(_inproj_kernel, q0=q0, k0=k0, v0=v0),
        out_shape=(jax.ShapeDtypeStruct((s, n), F32), jax.ShapeDtypeStruct((1, n_mod), F32)),
        grid=(n // tn, ni),
        in_specs=[pl.BlockSpec((tm, d), lambda j, i: (i, 0)),
                  pl.BlockSpec((d, tn), lambda j, i: (0, j)), hvec(), hvec(),
                  pl.BlockSpec((d, 1), lambda j, i: (0, 0)),
                  pl.BlockSpec((d, ADA_SLAB), lambda j, i: (0, mod_done // ADA_SLAB + slab(j, i))),
                  pl.BlockSpec((1, ADA_SLAB), lambda j, i: (0, mod_done // ADA_SLAB + slab(j, i)))],
        out_specs=(pl.BlockSpec((tm, tn), lambda j, i: (i, j)),
                   pl.BlockSpec((1, ADA_SLAB), lambda j, i: (0, slab(j, i)))),
        scratch_shapes=[pltpu.VMEM((d, tn), BF16), pltpu.VMEM((d, 1), F32)],
        compiler_params=_cparams(("arbitrary", "arbitrary")),
        name="inproj",
    )(h1, w_in, q_g, k_g, c.reshape(d, 1), w_ada, b_ada.reshape(1, -1))


def _conv_kernel(b_ref, c_ref, v_ref, cw_ref, g_ref, o_ref, ue_ref):
    i = pl.program_id(0)
    ta = b_ref.shape[1]
    w = cw_ref[...]
    assert w.shape[0] == 3
    for k in range(2):
        @pl.when(i == 0)
        def _():
            ue_ref[k, 0:SUBLANES, :] = jnp.zeros((SUBLANES, ue_ref.shape[2]), F32)

        @pl.when(i > 0)
        def _():
            ue_ref[k, 0:SUBLANES, :] = ue_ref[k, ta:ta + SUBLANES, :]

        r = RES - 2 + k
        ue_ref[k, SUBLANES:SUBLANES + ta, :] = c_ref[r] * v_ref[r]

    def u_at(bb):
        if bb >= 0:
            return c_ref[bb] * v_ref[bb]
        return ue_ref[2 + bb, SUBLANES - 1:SUBLANES - 1 + ta, :]

    for b in range(RES):
        y = b_ref[b] * (w[0:1, :] * u_at(b - 2) + w[1:2, :] * u_at(b - 1) + w[2:3, :] * u_at(b))
        ms = jnp.mean(y * y, axis=-1, keepdims=True)
        o_ref[b] = (y * lax.rsqrt(ms + EPS) * g_ref[...]).astype(BF16)


def _conv(z3, conv_w, g_conv, conv_ch):
    na = z3.shape[1]
    ta = min(64, na)
    blk = lambda jj: pl.BlockSpec((RES, ta, conv_ch), lambda i: (0, i, jj))
    return pl.pallas_call(
        _conv_kernel,
        out_shape=jax.ShapeDtypeStruct((RES, na, conv_ch), BF16),
        grid=(na // ta,),
        in_specs=[blk(0), blk(1), blk(2),
                  pl.BlockSpec(conv_w.shape, lambda i: (0, 0)),
                  pl.BlockSpec((1, conv_ch), lambda i: (0, 0))],
        out_specs=pl.BlockSpec((RES, ta, conv_ch), lambda i: (0, i, 0)),
        scratch_shapes=[pltpu.VMEM((2, ta + SUBLANES, conv_ch), F32)],
        compiler_params=_cparams(("arbitrary",)),
        name="conv",
    )(z3, z3, z3, conv_w, g_conv)


def _attn_kernel(slopes_ref, q_ref, kp_ref, kc_ref, vp_ref, vc_ref, o_ref, bias_ref, m_ref, l_ref, acc_ref):
    h = pl.program_id(0)
    cidx = pl.program_id(1)
    slope = slopes_ref[h]
    first_pen = jnp.where(cidx > 0, 0.0, NEG_INF).astype(F32)
    dn_t = (((1,), (1,)), ((), ()))

    j = lax.broadcasted_iota(jnp.int32, (BAND, 2 * BAND), 0)
    u = lax.broadcasted_iota(jnp.int32, (BAND, 2 * BAND), 1)
    is_prev = u < BAND
    uu = jnp.where(is_prev, u, u - BAND)
    for pi, (window, dil) in enumerate(PATTERNS):
        assert window // dil == BAND and RES % dil == 0
        ns, sr = RES // dil, BAND * dil // RES
        assert sr & (sr - 1) == 0
        sh = sr.bit_length() - 1
        mq = (j & (sr - 1)) * ns + (j >> sh)
        mk = (uu & (sr - 1)) * ns + (uu >> sh)
        steps = mq - mk + jnp.where(is_prev, BAND, 0)
        valid = (is_prev & (steps <= BAND)) | (jnp.logical_not(is_prev) & (steps >= 0))
        bias = jnp.where(valid, -slope * (steps * dil).astype(F32), NEG_INF)
        bias_ref[pi, 0] = bias
        bias_ref[pi, 1] = bias + jnp.where(is_prev, first_pen, 0.0)

    for pi, (window, dil) in enumerate(PATTERNS):
        ns, sr = RES // dil, BAND * dil // RES
        n_chunks = BAND // sr
        for n in range(n_chunks):
            for r in range(dil):
                def gather(ref, nn):
                    return jnp.concatenate([ref[dil * c + r, nn * sr:(nn + 1) * sr, :] for c in range(ns)], axis=0)

                def scatter(ref, val):
                    for c in range(ns):
                        ref[dil * c + r, n * sr:(n + 1) * sr, :] = val[c * sr:(c + 1) * sr, :]

                q = gather(q_ref, n).astype(BF16)
                if n > 0:
                    k_prev, v_prev = gather(kc_ref, n - 1), gather(vc_ref, n - 1)
                else:
                    k_prev, v_prev = gather(kp_ref, n_chunks - 1), gather(vp_ref, n_chunks - 1)
                kcat = jnp.concatenate([k_prev, gather(kc_ref, n)], axis=0).astype(BF16)
                vcat = jnp.concatenate([v_prev, gather(vc_ref, n)], axis=0).astype(BF16)
                s = lax.dot_general(q, kcat, dn_t, preferred_element_type=F32) + bias_ref[pi, 1 if n == 0 else 0]
                m = jnp.max(s, axis=-1, keepdims=True)
                p = jnp.exp(s - m)
                l = jnp.sum(p, axis=-1, keepdims=True)
                o = jnp.dot(p.astype(BF16), vcat, preferred_element_type=F32)
                m_b = jnp.broadcast_to(m, (BAND, HEAD_DIM))
                l_b = jnp.broadcast_to(l, (BAND, HEAD_DIM))
                if pi == 0:
                    scatter(m_ref, m_b)
                    scatter(l_ref, l_b)
                    scatter(acc_ref, o)
                else:
                    m_old = gather(m_ref, n)
                    m_new = jnp.maximum(m_old, m_b)
                    a_old = jnp.exp(m_old - m_new)
                    a_new = jnp.exp(m_b - m_new)
                    l_new = a_old * gather(l_ref, n) + a_new * l_b
                    o_new = a_old * gather(acc_ref, n) + a_new * o
                    if pi == len(PATTERNS) - 1:
                        scatter(acc_ref, o_new / l_new)
                    else:
                        scatter(m_ref, m_new)
                        scatter(l_ref, l_new)
                        scatter(acc_ref, o_new)
    o_ref[...] = acc_ref[...].astype(BF16)


def _attention(z3, slopes, conv_ch, attn_w):
    na = z3.shape[1]
    n_heads = attn_w // HEAD_DIM
    assert na % BAND == 0
    qb = 3 * conv_ch // HEAD_DIM
    kb = qb + n_heads
    vb = kb + n_heads
    blk = (RES, BAND, HEAD_DIM)
    cur = lambda base: pl.BlockSpec(blk, lambda h, c, sl: (0, c, base + h))
    prev = lambda base: pl.BlockSpec(blk, lambda h, c, sl: (0, jnp.maximum(c - 1, 0), base + h))
    return pl.pallas_call(
        _attn_kernel,
        out_shape=jax.ShapeDtypeStruct((RES, na, attn_w), BF16),
        grid_spec=pltpu.PrefetchScalarGridSpec(
            num_scalar_prefetch=1,
            grid=(n_heads, na // BAND),
            in_specs=[cur(qb), prev(kb), cur(kb), prev(vb), cur(vb)],
            out_specs=pl.BlockSpec(blk, lambda h, c, sl: (0, c, h)),
            scratch_shapes=[pltpu.VMEM((len(PATTERNS), 2, BAND, 2 * BAND), F32),
                            pltpu.VMEM(blk, F32), pltpu.VMEM(blk, F32), pltpu.VMEM(blk, F32)]),
        compiler_params=_cparams(("arbitrary", "arbitrary")),
        name="attn",
    )(slopes, z3, z3, z3, z3, z3)


def _outproj_kernel(yc_ref, ya_ref, ga_ref, w_ref, x_ref, gt_ref, o_ref, y_ref):
    _, ta, cc = yc_ref.shape

    @pl.when(pl.program_id(1) == 0)
    def _():
        n = RES * ta
        ya = ya_ref[...].reshape(n, ya_ref.shape[2]).astype(F32)
        ms = jnp.mean(ya * ya, axis=-1, keepdims=True)
        yn = (ya * lax.rsqrt(ms + EPS) * ga_ref[...]).astype(BF16)
        perm = _row_permutation(ta, False)
        y_ref[:, 0:cc] = jnp.dot(perm, yc_ref[...].reshape(n, cc), preferred_element_type=F32).astype(BF16)
        y_ref[:, cc:] = jnp.dot(perm, yn, preferred_element_type=F32).astype(BF16)

    acc = jnp.dot(y_ref[...], w_ref[...], preferred_element_type=F32)
    o_ref[...] = x_ref[...] + gt_ref[...] * acc


def _outproj(y_conv3, y_attn3, g_attn, w_out_bf, x2d, gt1):
    s, d = x2d.shape
    na = s // RES
    cc, aw = y_conv3.shape[2], y_attn3.shape[2]
    ta = min(32, na)
    tn = min(1024, d)
    return pl.pallas_call(
        _outproj_kernel,
        out_shape=jax.ShapeDtypeStruct((s, d), F32),
        grid=(na // ta, d // tn),
        in_specs=[pl.BlockSpec((RES, ta, cc), lambda i, j: (0, i, 0)),
                  pl.BlockSpec((RES, ta, aw), lambda i, j: (0, i, 0)),
                  pl.BlockSpec((1, aw), lambda i, j: (0, 0)),
                  pl.BlockSpec((cc + aw, tn), lambda i, j: (0, j)),
                  pl.BlockSpec((RES * ta, tn), lambda i, j: (i, j)),
                  pl.BlockSpec((1, tn), lambda i, j: (0, j))],
        out_specs=pl.BlockSpec((RES * ta, tn), lambda i, j: (i, j)),
        scratch_shapes=[pltpu.VMEM((RES * ta, cc + aw), BF16)],
        compiler_params=_cparams(("arbitrary", "arbitrary")),
        name="outproj",
    )(y_conv3, y_attn3, g_attn, w_out_bf, x2d, gt1)


def _router_kernel(x_ref, g_ref, sc_ref, sh_ref, wr_ref, h_ref, eid_ref, wt_ref):
    h = _rms_mod(x_ref[...], g_ref[...], sc_ref[...], sh_ref[...])
    h_ref[...] = h
    w = wr_ref[...]
    h_hi, w_hi = h.astype(BF16), w.astype(BF16)
    h_lo = (h - h_hi.astype(F32)).astype(BF16)
    w_lo = (w - w_hi.astype(F32)).astype(BF16)
    logits = (jnp.dot(h_hi, w_hi, preferred_element_type=F32) + jnp.dot(h_hi, w_lo, preferred_element_type=F32)
              + jnp.dot(h_lo, w_hi, preferred_element_type=F32))
    tm = logits.shape[0]
    lane = lax.broadcasted_iota(jnp.int32, (tm, LANES), 1).astype(F32)
    big = float(LANES)
    is_g = lane < N_GROUPS
    gl = jnp.where(is_g, logits, NEG_INF)
    gmax = jnp.max(gl, axis=-1, keepdims=True)
    gsel = jnp.min(jnp.where(gl == gmax, lane, big), axis=-1, keepdims=True)
    g_w = 1.0 / jnp.sum(jnp.where(is_g, jnp.exp(gl - gmax), 0.0), axis=-1, keepdims=True)
    lo = N_GROUPS + gsel * EXPERTS_PER_GROUP
    in_grp = (lane >= lo) & (lane < lo + EXPERTS_PER_GROUP)
    el = jnp.where(in_grp, logits, NEG_INF)
    e1 = jnp.max(el, axis=-1, keepdims=True)
    i1 = jnp.min(jnp.where(el == e1, lane, big), axis=-1, keepdims=True)
    el2 = jnp.where(lane == i1, NEG_INF, el)
    e2 = jnp.max(el2, axis=-1, keepdims=True)
    i2 = jnp.min(jnp.where(el2 == e2, lane, big), axis=-1, keepdims=True)
    t = jnp.exp(e2 - e1)
    w1 = g_w / (1.0 + t)
    w2 = g_w * t / (1.0 + t)
    eid = jnp.where(lane == 0.0, i1 - N_GROUPS, jnp.where(lane == 1.0, i2 - N_GROUPS, 0.0))
    eid_ref[...] = eid.astype(jnp.int32)
    wt_ref[...] = jnp.where(lane == 0.0, w1, jnp.where(lane == 1.0, w2, 0.0))


def _router(x2, g_ffn, sc2, sh2, w_r):
    s, d = x2.shape
    tm = min(512, s)
    vec = lambda: pl.BlockSpec((1, d), lambda i: (0, 0))
    lanes_out = lambda: pl.BlockSpec((tm, LANES), lambda i: (i, 0))
    return pl.pallas_call(
        _router_kernel,
        out_shape=(jax.ShapeDtypeStruct((s, d), F32),
                   jax.ShapeDtypeStruct((s, LANES), jnp.int32),
                   jax.ShapeDtypeStruct((s, LANES), F32)),
        grid=(s // tm,),
        in_specs=[pl.BlockSpec((tm, d), lambda i: (i, 0)), vec(), vec(), vec(),
                  pl.BlockSpec((d, LANES), lambda i: (0, 0))],
        out_specs=(pl.BlockSpec((tm, d), lambda i: (i, 0)), lanes_out(), lanes_out()),
        compiler_params=_cparams(("arbitrary",)),
        name="router",
    )(x2, g_ffn, sc2, sh2, w_r)


def _rank_kernel(eid_ref, rank_ref, cnt_ref, run_ref):
    @pl.when(pl.program_id(0) == 0)
    def _():
        run_ref[...] = jnp.zeros_like(run_ref)

    tb = eid_ref.shape[0]
    lane = lax.broadcasted_iota(jnp.int32, (tb, LANES), 1)
    row = lax.broadcasted_iota(jnp.int32, (tb, tb), 0)
    col = lax.broadcasted_iota(jnp.int32, (tb, tb), 1)
    earlier = (col < row).astype(BF16)
    e = eid_ref[...]
    run = run_ref[...]
    ranks = []
    for k in range(2):
        onehot = e[:, k:k + 1] == lane
        oh = onehot.astype(F32)
        before = jnp.dot(earlier, oh.astype(BF16), preferred_element_type=F32) + run
        ranks.append(jnp.sum(jnp.where(onehot, before, 0.0), axis=-1, keepdims=True))
        run = run + jnp.sum(oh, axis=0, keepdims=True)
    rank_ref[...] = jnp.where(lane == 0, ranks[0], jnp.where(lane == 1, ranks[1], 0.0)).astype(jnp.int32)
    run_ref[...] = run
    cnt_ref[...] = run


def _rank(eid):
    s = eid.shape[0]
    tb = min(512, s)
    return pl.pallas_call(
        _rank_kernel,
        out_shape=(jax.ShapeDtypeStruct((s, LANES), jnp.int32), jax.ShapeDtypeStruct((1, LANES), F32)),
        grid=(s // tb,),
        in_specs=[pl.BlockSpec((tb, LANES), lambda i: (i, 0))],
        out_specs=(pl.BlockSpec((tb, LANES), lambda i: (i, 0)), pl.BlockSpec((1, LANES), lambda i: (0, 0))),
        scratch_shapes=[pltpu.VMEM((1, LANES), F32)],
        compiler_params=_cparams(("arbitrary",)),
        name="rank",
    )(eid)


def _plan_kernel(eid_ref, rank_ref, cnt_ref, dest_ref, src_ref, blke_ref, blkc_ref, nused_ref, pst_ref,
                 *, blk, n_blk):
    def per_expert(e, nb):
        c = cnt_ref[e]
        pst_ref[e] = nb * blk
        nbe = (c + (blk - 1)) // blk

        def per_block(jb, carry):
            blke_ref[nb + jb] = e
            blkc_ref[nb + jb] = jnp.minimum(blk, c - jb * blk)
            return carry

        lax.fori_loop(0, nbe, per_block, 0)

        def pad_slot(r, carry):
            src_ref[nb * blk + r] = 0
            return carry

        lax.fori_loop(c, (c + (GATHER_GROUP - 1)) // GATHER_GROUP * GATHER_GROUP, pad_slot, 0)
        return nb + nbe

    n_used = lax.fori_loop(0, N_EXPERTS, per_expert, jnp.int32(0))
    nused_ref[0] = n_used
    last_e = blke_ref[n_used - 1]

    def unused(i, carry):
        blke_ref[i] = last_e
        blkc_ref[i] = 0
        return carry

    lax.fori_loop(n_used, n_blk, unused, 0)

    n_tok = eid_ref.shape[0] // 2

    def place(a, carry):
        dst = pst_ref[eid_ref[a]] + rank_ref[a]
        dest_ref[a] = dst
        src_ref[dst] = jnp.where(a >= n_tok, a - n_tok, a)
        return carry

    lax.fori_loop(0, eid_ref.shape[0], place, 0, unroll=16)


def _plan(eid_flat, rank_flat, counts, blk):
    a = eid_flat.shape[0]
    n_blk = a // blk + N_EXPERTS
    smem = lambda: pl.BlockSpec(memory_space=pltpu.SMEM)
    i32 = jnp.int32
    return pl.pallas_call(
        functools.partial(_plan_kernel, blk=blk, n_blk=n_blk),
        out_shape=(jax.ShapeDtypeStruct((a,), i32), jax.ShapeDtypeStruct((n_blk * blk,), i32),
                   jax.ShapeDtypeStruct((n_blk,), i32), jax.ShapeDtypeStruct((n_blk,), i32),
                   jax.ShapeDtypeStruct((1,), i32)),
        in_specs=[smem(), smem(), smem()],
        out_specs=(smem(), smem(), smem(), smem(), smem()),
        scratch_shapes=[pltpu.SMEM((N_EXPERTS,), i32)],
        name="plan",
    )(eid_flat, rank_flat, counts)


def _gather_rows(idx_of_row, n_rows, src_hbm, dst, sem):
    def body(g, carry):
        for rr in range(GATHER_GROUP):
            r = g * GATHER_GROUP + rr
            pltpu.make_async_copy(src_hbm.at[pl.ds(idx_of_row(r), 1), :], dst.at[pl.ds(r, 1), :], sem).start()
        return carry
    lax.fori_loop(0, (n_rows + (GATHER_GROUP - 1)) // GATHER_GROUP, body, 0)


def _wait_rows(n_rows, src_hbm, dst, sem):
    def body(g, carry):
        pltpu.make_async_copy(src_hbm.at[pl.ds(0, GATHER_GROUP), :], dst.at[pl.ds(0, GATHER_GROUP), :], sem).wait()
        return carry
    lax.fori_loop(0, (n_rows + (GATHER_GROUP - 1)) // GATHER_GROUP, body, 0)


def _expert_kernel(blke_ref, blkc_ref, nused_ref, src_ref, h_hbm, w1_hbm, w3_hbm, w2_hbm, ys_hbm,
                   xbuf, obuf, w1b, w3b, w2b, gsem, wsem, osem, *, blk, kc):
    i = pl.program_id(0)
    last = pl.num_programs(0) - 1
    n_used = nused_ref[0]
    e = blke_ref[i]
    e_next = blke_ref[jnp.minimum(i + 1, last)]
    first_of_expert = (i == 0) | (blke_ref[jnp.maximum(i - 1, 0)] != e)
    fetch_next = (i + 1 < n_used) & (e_next != e)
    wbufs = ((w1_hbm, w1b), (w3_hbm, w3b), (w2_hbm, w2b))

    def weight_copy(k, ee):
        return pltpu.make_async_copy(wbufs[k][0].at[ee], wbufs[k][1], wsem.at[k])

    def issue_rows(b, slot):
        _gather_rows(lambda r: src_ref[b * blk + r], blkc_ref[b], h_hbm, xbuf.at[slot], gsem.at[slot])

    def weights_ready(k):
        @pl.when(first_of_expert)
        def _():
            weight_copy(k, e).wait()

    def weights_done(k):
        @pl.when(fetch_next)
        def _():
            weight_copy(k, e_next).start()

    def out_copy(b, slot, ch):
        return pltpu.make_async_copy(obuf.at[slot, pl.ds(ch * OUT_CHUNK, OUT_CHUNK), :],
                                     ys_hbm.at[pl.ds(b * blk + ch * OUT_CHUNK, OUT_CHUNK), :], osem.at[slot])

    def for_out_chunks(b, fn):
        def body(ch, carry):
            fn(ch)
            return carry
        lax.fori_loop(0, (blkc_ref[b] + (OUT_CHUNK - 1)) // OUT_CHUNK, body, 0)

    slot = lax.rem(i, 2)

    @pl.when(i == 0)
    def _():
        for k in range(3):
            weight_copy(k, e).start()
        xbuf[...] = jnp.zeros_like(xbuf)
        issue_rows(0, 0)

    @pl.when(i < n_used)
    def _():
        _wait_rows(blkc_ref[i], h_hbm, xbuf.at[slot], gsem.at[slot])

        @pl.when(i + 1 < n_used)
        def _():
            issue_rows(i + 1, 1 - slot)

        d = xbuf.shape[2]

        def up_proj(wb):
            acc = None
            for k0 in range(0, d, kc):
                part = jnp.dot(xbuf[slot, :, k0:k0 + kc].astype(BF16), wb[k0:k0 + kc, :].astype(BF16),
                               preferred_element_type=F32)
                acc = part if acc is None else acc + part
            return acc

        weights_ready(0)
        a = up_proj(w1b)
        weights_done(0)
        weights_ready(1)
        b = up_proj(w3b)
        weights_done(1)
        hid = ((a * jax.nn.sigmoid(a)) * b).astype(BF16)
        weights_ready(2)

        @pl.when(i >= 2)
        def _():
            for_out_chunks(i - 2, lambda ch: out_copy(i - 2, slot, 0).wait())

        for t in range(0, d, Y_TILE):
            obuf[slot, :, t:t + Y_TILE] = jnp.dot(hid, w2b[:, t:t + Y_TILE].astype(BF16),
                                                  preferred_element_type=F32)
        weights_done(2)
        for_out_chunks(i, lambda ch: out_copy(i, slot, ch).start())

        @pl.when(i == n_used - 1)
        def _():
            @pl.when(i >= 1)
            def _():
                for_out_chunks(i - 1, lambda ch: out_copy(i - 1, 1 - slot, 0).wait())
            for_out_chunks(i, lambda ch: out_copy(i, slot, 0).wait())


def _experts(h2, w1, w3, w2, blk_e, blk_cnt, n_used, row_src, blk):
    s, d = h2.shape
    de = w1.shape[-1]
    kc = min(1024, d)
    assert d % kc == 0 and d % Y_TILE == 0 and blk % GATHER_GROUP == 0 and blk % OUT_CHUNK == 0
    n_blk = blk_e.shape[0]
    anyspace = lambda: pl.BlockSpec(memory_space=pl.ANY)
    return pl.pallas_call(
        functools.partial(_expert_kernel, blk=blk, kc=kc),
        out_shape=jax.ShapeDtypeStruct((n_blk * blk, d), F32),
        grid_spec=pltpu.PrefetchScalarGridSpec(
            num_scalar_prefetch=4,
            grid=(n_blk,),
            in_specs=[anyspace(), anyspace(), anyspace(), anyspace()],
            out_specs=anyspace(),
            scratch_shapes=[pltpu.VMEM((2, blk, d), F32), pltpu.VMEM((2, blk, d), F32),
                            pltpu.VMEM((d, de), F32), pltpu.VMEM((d, de), F32), pltpu.VMEM((de, d), F32),
                            pltpu.SemaphoreType.DMA((2,)), pltpu.SemaphoreType.DMA((3,)),
                            pltpu.SemaphoreType.DMA((2,))]),
        compiler_params=_cparams(("arbitrary",)),
        name="experts",
    )(blk_e, blk_cnt, n_used, row_src, h2, w1, w3, w2)


def _combine_kernel(dest_ref, ys_hbm, x_ref, wt_ref, gt_ref, o_ref, ybuf, sem):
    i = pl.program_id(0)
    nb = pl.num_programs(0)
    tm = x_ref.shape[0]
    n_tok = tm * nb

    def issue(blk_i, slot):
        for k in range(2):
            _gather_rows(lambda r: dest_ref[k * n_tok + blk_i * tm + r], tm, ys_hbm, ybuf.at[slot, k],
                         sem.at[slot])

    slot = lax.rem(i, 2)

    @pl.when(i == 0)
    def _():
        issue(0, 0)

    @pl.when(i + 1 < nb)
    def _():
        issue(i + 1, 1 - slot)

    for k in range(2):
        _wait_rows(tm, ys_hbm, ybuf.at[slot, k], sem.at[slot])
    w = wt_ref[...]
    y = ybuf[slot, 0] * w[:, 0:1] + ybuf[slot, 1] * w[:, 1:2]
    o_ref[...] = x_ref[...] + gt_ref[...] * y


def _combine(dest, ys, x2, wts, gt2):
    s, d = x2.shape
    tm = min(256, s)
    return pl.pallas_call(
        _combine_kernel,
        out_shape=jax.ShapeDtypeStruct((s, d), F32),
        grid_spec=pltpu.PrefetchScalarGridSpec(
            num_scalar_prefetch=1,
            grid=(s // tm,),
            in_specs=[pl.BlockSpec(memory_space=pl.ANY),
                      pl.BlockSpec((tm, d), lambda i, ds_: (i, 0)),
                      pl.BlockSpec((tm, LANES), lambda i, ds_: (i, 0)),
                      pl.BlockSpec((1, d), lambda i, ds_: (0, 0))],
            out_specs=pl.BlockSpec((tm, d), lambda i, ds_: (i, 0)),
            scratch_shapes=[pltpu.VMEM((2, 2, tm, d), F32), pltpu.SemaphoreType.DMA((2,))]),
        compiler_params=_cparams(("arbitrary",)),
        name="combine",
    )(dest, ys, x2, wts, gt2)


def _layer(x2d, c, w_ada, b_ada, g_mix, g_ffn, w_in, conv_w, q_norm_g, k_norm_g, g_branch, w_out,
           w_group, w_expert, w1, w3, w2):
    s, d = x2d.shape
    conv_ch = conv_w.shape[1]
    attn_w = d - conv_ch
    n_heads = attn_w // HEAD_DIM
    na = s // RES
    assert s % (RES * BAND) == 0
    row = lambda v: v.reshape(1, -1)

    mod_a = _ada(c, w_ada, b_ada, 2 * d)
    sh1, sc1 = mod_a[:, :d], mod_a[:, d:]

    h1 = _norm1(x2d, row(g_mix), sc1, sh1)
    z, mod_b = _inproj(h1.reshape(s, d), w_in, row(q_norm_g), row(k_norm_g), conv_ch, attn_w,
                       c, w_ada, b_ada, 2 * d)
    gt1, sh2, sc2, gt2 = [mod_b[:, i * d:(i + 1) * d] for i in range(4)]
    z3 = z.reshape(RES, na, z.shape[1])
    y_conv = _conv(z3, conv_w, row(g_branch[:conv_ch]), conv_ch)
    slopes = jnp.exp2(-ALIBI_MAX_EXP * jnp.arange(1, n_heads + 1, dtype=F32) / n_heads)
    y_attn = _attention(z3, slopes, conv_ch, attn_w)
    x2 = _outproj(y_conv, y_attn, row(g_branch[conv_ch:]), w_out.astype(BF16), x2d, gt1)

    w_r = jnp.concatenate([w_group, w_expert,
                           jnp.zeros((d, LANES - N_GROUPS - N_EXPERTS), F32)], axis=1)
    h2, eid, wts = _router(x2, row(g_ffn), sc2, sh2, w_r)
    rank, counts = _rank(eid)
    eid_flat = eid[:, :2].T.reshape(-1)
    rank_flat = rank[:, :2].T.reshape(-1)
    cnt = counts[0, :N_EXPERTS].astype(jnp.int32)
    dest, row_src, blk_e, blk_cnt, n_used = _plan(eid_flat, rank_flat, cnt, MOE_BLOCK)
    ys = _experts(h2, w1, w3, w2, blk_e, blk_cnt, n_used, row_src, MOE_BLOCK)
    return _combine(dest, ys, x2, wts, gt2)


def kernel(x, c, w_ada, b_ada, g_mix, g_ffn, w_in, conv_w, q_norm_g, k_norm_g, g_branch, w_out,
           w_group, w_expert, w1, w3, w2):
    b, s, d = x.shape
    assert b == 1 and c.shape[0] == 1
    for l in range(w_ada.shape[0]):
        x = _layer(x.reshape(s, d), c[0], w_ada[l], b_ada[l], g_mix[l], g_ffn[l], w_in[l], conv_w[l],
                   q_norm_g[l], k_norm_g[l], g_branch[l], w_out[l], w_group[l], w_expert[l],
                   w1[l], w3[l], w2[l]).reshape(b, s, d)
    return x
```

```python
import functools

import jax
import jax.numpy as jnp
from jax import lax
from jax.experimental import pallas as pl
from jax.experimental.pallas import tpu as pltpu

EPS = 1e-6
HEAD_DIM = 128
BAND = 128
PATTERNS = ((128, 1), (512, 4), (2048, 16))
RES = 16
ALIBI_MAX_EXP = 8.0
N_GROUPS = 8
EXPERTS_PER_GROUP = 8
N_EXPERTS = N_GROUPS * EXPERTS_PER_GROUP
MOE_BLOCK = 256
GATHER_GROUP = 8
Y_TILE = 512
OUT_CHUNK = 64
ADA_SLAB = 128
LANES = 128
SUBLANES = 8
VMEM_LIMIT = 56 * 1024 * 1024
F32 = jnp.float32
BF16 = jnp.bfloat16
NEG_INF = float("-inf")
LOG2E = 1.4426950408889634


def _cparams(sem):
    return pltpu.CompilerParams(dimension_semantics=sem, vmem_limit_bytes=VMEM_LIMIT)


def _ada_kernel(c_ref, w_ref, b_ref, o_ref, s_ref, *, rows):
    @pl.when(pl.program_id(0) == 0)
    def _():
        c = c_ref[...]
        s_ref[...] = c * jax.nn.sigmoid(c)

    d, tn = w_ref.shape

    def body(i, acc):
        r0 = pl.multiple_of(i * rows, rows)
        p = w_ref[pl.ds(r0, rows), :] * s_ref[pl.ds(r0, rows), :]
        return acc + jnp.sum(p.reshape(rows // SUBLANES, SUBLANES, tn), axis=0)

    acc = lax.fori_loop(0, d // rows, body, jnp.zeros((SUBLANES, tn), F32))
    o_ref[...] = jnp.sum(acc, axis=0, keepdims=True) + b_ref[...]


def _ada_slab(w_ref, s_ref, b_ref):
    d, tn = w_ref.shape
    rows = min(512, d)
    acc = jnp.zeros((SUBLANES, tn), F32)
    for r0 in range(0, d, rows):
        p = w_ref[r0:r0 + rows, :] * s_ref[r0:r0 + rows, :]
        acc = acc + jnp.sum(p.reshape(rows // SUBLANES, SUBLANES, tn), axis=0)
    return jnp.sum(acc, axis=0, keepdims=True) + b_ref[...]


def _ada(c, w_ada, b_ada, n):
    d = w_ada.shape[0]
    tn = min(512, n)
    rows = min(512, d)
    return pl.pallas_call(
        functools.partial(_ada_kernel, rows=rows),
        out_shape=jax.ShapeDtypeStruct((1, n), F32),
        grid=(n // tn,),
        in_specs=[pl.BlockSpec((d, 1), lambda j: (0, 0)),
                  pl.BlockSpec((d, tn), lambda j: (0, j)),
                  pl.BlockSpec((1, tn), lambda j: (0, j))],
        out_specs=pl.BlockSpec((1, tn), lambda j: (0, j)),
        scratch_shapes=[pltpu.VMEM((d, 1), F32)],
        compiler_params=_cparams(("arbitrary",)),
        name="ada",
    )(c.reshape(d, 1), w_ada, b_ada.reshape(1, -1))


def _rms_mod(x, g, sc, sh):
    ms = jnp.mean(x * x, axis=-1, keepdims=True)
    return (x * lax.rsqrt(ms + EPS) * g) * (1.0 + sc) + sh


def _row_permutation(ta, to_residue_major):
    n = RES * ta
    assert ta & (ta - 1) == 0
    sh = ta.bit_length() - 1
    r = lax.broadcasted_iota(jnp.int32, (n, n), 0)
    c = lax.broadcasted_iota(jnp.int32, (n, n), 1)
    res_major, natural = (r, c) if to_residue_major else (c, r)
    return (natural == RES * (res_major & (ta - 1)) + (res_major >> sh)).astype(BF16)


def _norm1_kernel(x_ref, g_ref, sc_ref, sh_ref, o_ref):
    ta = o_ref.shape[1]
    h = _rms_mod(x_ref[...], g_ref[...], sc_ref[...], sh_ref[...]).astype(BF16)
    hp = jnp.dot(_row_permutation(ta, True), h, preferred_element_type=F32)
    o_ref[...] = hp.astype(BF16).reshape(o_ref.shape)


def _norm1(x2d, g_mix, sc1, sh1):
    s, d = x2d.shape
    na = s // RES
    ta = min(32, na)
    vec = lambda: pl.BlockSpec((1, d), lambda i: (0, 0))
    return pl.pallas_call(
        _norm1_kernel,
        out_shape=jax.ShapeDtypeStruct((RES, na, d), BF16),
        grid=(na // ta,),
        in_specs=[pl.BlockSpec((RES * ta, d), lambda i: (i, 0)), vec(), vec(), vec()],
        out_specs=pl.BlockSpec((RES, ta, d), lambda i: (0, i, 0)),
        compiler_params=_cparams(("arbitrary",)),
        name="norm1",
    )(x2d, g_mix, sc1, sh1)


def _inproj_kernel(h_ref, w_ref, qg_ref, kg_ref, c_ref, wa_ref, ba_ref, z_ref, mod_ref, wb_ref, s_ref,
                   *, q0, k0, v0):
    j = pl.program_id(0)

    @pl.when((j == 0) & (pl.program_id(1) == 0))
    def _():
        c = c_ref[...]
        s_ref[...] = c * jax.nn.sigmoid(c)

    @pl.when(pl.program_id(1) == 0)
    def _():
        wb_ref[...] = w_ref[...].astype(BF16)

    mod_ref[...] = _ada_slab(wa_ref, s_ref, ba_ref)
    acc = jnp.dot(h_ref[...], wb_ref[...], preferred_element_type=F32)
    is_q = (j >= q0) & (j < k0)
    is_k = (j >= k0) & (j < v0)

    g = jnp.where(is_q, qg_ref[...] * (HEAD_DIM ** -0.5 * LOG2E), kg_ref[...])
    for hh in range(acc.shape[1] // HEAD_DIM):
        a = acc[:, hh * HEAD_DIM:(hh + 1) * HEAD_DIM]
        ms = jnp.mean(a * a, axis=-1, keepdims=True)
        scale = jnp.where(is_q | is_k, lax.rsqrt(ms + EPS) * g, 1.0)
        z_ref[:, hh * HEAD_DIM:(hh + 1) * HEAD_DIM] = a * scale


def _inproj(h1, w_in, q_g, k_g, conv_ch, attn_w, c, w_ada, b_ada, mod_done):
    s, d = h1.shape
    n = w_in.shape[1]
    tm = min(1024, s)
    tn = min(512, conv_ch)
    assert conv_ch % tn == 0 and attn_w % tn == 0 and tn % HEAD_DIM == 0
    q0 = 3 * conv_ch // tn
    k0 = q0 + attn_w // tn
    v0 = k0 + attn_w // tn
    ni = s // tm
    n_mod = w_ada.shape[1] - mod_done
    n_slab = n_mod // ADA_SLAB
    assert n_mod % ADA_SLAB == 0 and mod_done % ADA_SLAB == 0 and n_slab <= (n // tn) * ni
    slab = lambda j, i: jnp.minimum(j * ni + i, n_slab - 1)
    hvec = lambda: pl.BlockSpec((1, HEAD_DIM), lambda j, i: (0, 0))
    return pl.pallas_call(
        functools.partial(_inproj_kernel, q0=q0, k0=k0, v0=v0),
        out_shape=(jax.ShapeDtypeStruct((s, n), F32), jax.ShapeDtypeStruct((1, n_mod), F32)),
        grid=(n // tn, ni),
        in_specs=[pl.BlockSpec((tm, d), lambda j, i: (i, 0)),
                  pl.BlockSpec((d, tn), lambda j, i: (0, j)), hvec(), hvec(),
                  pl.BlockSpec((d, 1), lambda j, i: (0, 0)),
                  pl.BlockSpec((d, ADA_SLAB), lambda j, i: (0, mod_done // ADA_SLAB + slab(j, i))),
                  pl.BlockSpec((1, ADA_SLAB), lambda j, i: (0, mod_done // ADA_SLAB + slab(j, i)))],
        out_specs=(pl.BlockSpec((tm, tn), lambda j, i: (i, j)),
                   pl.BlockSpec((1, ADA_SLAB), lambda j, i: (0, slab(j, i)))),
        scratch_shapes=[pltpu.VMEM((d, tn), BF16), pltpu.VMEM((d, 1), F32)],
        compiler_params=_cparams(("arbitrary", "arbitrary")),
        name="inproj",
    )(h1, w_in, q_g, k_g, c.reshape(d, 1), w_ada, b_ada.reshape(1, -1))


def _conv_kernel(b_ref, c_ref, v_ref, cw_ref, g_ref, o_ref, ue_ref):
    i = pl.program_id(0)
    ta = b_ref.shape[1]
    w = cw_ref[...]
    assert w.shape[0] == 3
    for k in range(2):
        @pl.when(i == 0)
        def _():
            ue_ref[k, 0:SUBLANES, :] = jnp.zeros((SUBLANES, ue_ref.shape[2]), F32)

        @pl.when(i > 0)
        def _():
            ue_ref[k, 0:SUBLANES, :] = ue_ref[k, ta:ta + SUBLANES, :]

        r = RES - 2 + k
        ue_ref[k, SUBLANES:SUBLANES + ta, :] = c_ref[r] * v_ref[r]

    def u_at(bb):
        if bb >= 0:
            return c_ref[bb] * v_ref[bb]
        return ue_ref[2 + bb, SUBLANES - 1:SUBLANES - 1 + ta, :]

    for b in range(RES):
        y = b_ref[b] * (w[0:1, :] * u_at(b - 2) + w[1:2, :] * u_at(b - 1) + w[2:3, :] * u_at(b))
        ms = jnp.mean(y * y, axis=-1, keepdims=True)
        o_ref[b] = (y * lax.rsqrt(ms + EPS) * g_ref[...]).astype(BF16)


def _conv(z3, conv_w, g_conv, conv_ch):
    na = z3.shape[1]
    ta = min(64, na)
    blk = lambda jj: pl.BlockSpec((RES, ta, conv_ch), lambda i: (0, i, jj))
    return pl.pallas_call(
        _conv_kernel,
        out_shape=jax.ShapeDtypeStruct((RES, na, conv_ch), BF16),
        grid=(na // ta,),
        in_specs=[blk(0), blk(1), blk(2),
                  pl.BlockSpec(conv_w.shape, lambda i: (0, 0)),
                  pl.BlockSpec((1, conv_ch), lambda i: (0, 0))],
        out_specs=pl.BlockSpec((RES, ta, conv_ch), lambda i: (0, i, 0)),
        scratch_shapes=[pltpu.VMEM((2, ta + SUBLANES, conv_ch), F32)],
        compiler_params=_cparams(("arbitrary",)),
        name="conv",
    )(z3, z3, z3, conv_w, g_conv)


def _attn_kernel(slopes_ref, q_ref, kp_ref, kc_ref, vp_ref, vc_ref, o_ref, bias_ref, m_ref, l_ref, acc_ref):
    h = pl.program_id(0)
    cidx = pl.program_id(1)
    slope = slopes_ref[h] * LOG2E
    first_pen = jnp.where(cidx > 0, 0.0, NEG_INF).astype(F32)
    dn_t = (((1,), (1,)), ((), ()))
    ones_kv = jnp.ones((2 * BAND, HEAD_DIM), BF16)

    j = lax.broadcasted_iota(jnp.int32, (BAND, 2 * BAND), 0)
    u = lax.broadcasted_iota(jnp.int32, (BAND, 2 * BAND), 1)
    is_prev = u < BAND
    uu = jnp.where(is_prev, u, u - BAND)
    for pi, (window, dil) in enumerate(PATTERNS):
        assert window // dil == BAND and RES % dil == 0
        ns, sr = RES // dil, BAND * dil // RES
        assert sr & (sr - 1) == 0
        sh = sr.bit_length() - 1
        mq = (j & (sr - 1)) * ns + (j >> sh)
        mk = (uu & (sr - 1)) * ns + (uu >> sh)
        steps = mq - mk + jnp.where(is_prev, BAND, 0)
        valid = (is_prev & (steps <= BAND)) | (jnp.logical_not(is_prev) & (steps >= 0))
        bias = jnp.where(valid, -slope * (steps * dil).astype(F32), NEG_INF)
        bias_ref[pi, 0] = bias
        bias_ref[pi, 1] = bias + jnp.where(is_prev, first_pen, 0.0)

    for pi, (window, dil) in enumerate(PATTERNS):
        ns, sr = RES // dil, BAND * dil // RES
        n_chunks = BAND // sr
        for n in range(n_chunks):
            for r in range(dil):
                def gather(ref, nn):
                    return jnp.concatenate([ref[dil * c + r, nn * sr:(nn + 1) * sr, :] for c in range(ns)], axis=0)

                q = gather(q_ref, n).astype(BF16)
                if n > 0:
                    k_prev, v_prev = gather(kc_ref, n - 1), gather(vc_ref, n - 1)
                else:
                    k_prev, v_prev = gather(kp_ref, n_chunks - 1), gather(vp_ref, n_chunks - 1)
                kcat = jnp.concatenate([k_prev, gather(kc_ref, n)], axis=0).astype(BF16)
                vcat = jnp.concatenate([v_prev, gather(vc_ref, n)], axis=0).astype(BF16)
                s = lax.dot_general(q, kcat, dn_t, preferred_element_type=F32) + bias_ref[pi, 1 if n == 0 else 0]
                m = jnp.max(s, axis=-1, keepdims=True)
                p = jnp.exp2(s - m).astype(BF16)
                ol = jnp.dot(p, jnp.concatenate([vcat, ones_kv], axis=1), preferred_element_type=F32)
                o, l_all = ol[:, :HEAD_DIM], ol[:, HEAD_DIM:]
                for c in range(ns):
                    rows = slice(c * sr, (c + 1) * sr)
                    at = (dil * c + r, slice(n * sr, (n + 1) * sr), slice(None))
                    m_b = jnp.broadcast_to(m[rows], (sr, HEAD_DIM))
                    l_b = l_all[rows]
                    if pi == 0:
                        m_ref[at] = m_b
                        l_ref[at] = l_b
                        acc_ref[at] = o[rows]
                    else:
                        m_old = m_ref[at]
                        m_new = jnp.maximum(m_old, m_b)
                        a_old = jnp.exp2(m_old - m_new)
                        a_new = jnp.exp2(m_b - m_new)
                        l_new = a_old * l_ref[at] + a_new * l_b
                        o_new = a_old * acc_ref[at] + a_new * o[rows]
                        if pi == len(PATTERNS) - 1:
                            acc_ref[at] = o_new / l_new
                        else:
                            m_ref[at] = m_new
                            l_ref[at] = l_new
                            acc_ref[at] = o_new
    o_ref[...] = acc_ref[...].astype(BF16)


def _attention(z3, slopes, conv_ch, attn_w):
    na = z3.shape[1]
    n_heads = attn_w // HEAD_DIM
    assert na % BAND == 0
    qb = 3 * conv_ch // HEAD_DIM
    kb = qb + n_heads
    vb = kb + n_heads
    blk = (RES, BAND, HEAD_DIM)
    cur = lambda base: pl.BlockSpec(blk, lambda h, c, sl: (0, c, base + h))
    prev = lambda base: pl.BlockSpec(blk, lambda h, c, sl: (0, jnp.maximum(c - 1, 0), base + h))
    return pl.pallas_call(
        _attn_kernel,
        out_shape=jax.ShapeDtypeStruct((RES, na, attn_w), BF16),
        grid_spec=pltpu.PrefetchScalarGridSpec(
            num_scalar_prefetch=1,
            grid=(n_heads, na // BAND),
            in_specs=[cur(qb), prev(kb), cur(kb), prev(vb), cur(vb)],
            out_specs=pl.BlockSpec(blk, lambda h, c, sl: (0, c, h)),
            scratch_shapes=[pltpu.VMEM((len(PATTERNS), 2, BAND, 2 * BAND), F32),
                            pltpu.VMEM(blk, F32), pltpu.VMEM(blk, F32), pltpu.VMEM(blk, F32)]),
        compiler_params=_cparams(("arbitrary", "arbitrary")),
        name="attn",
    )(slopes, z3, z3, z3, z3, z3)


def _outproj_kernel(yc_ref, ya_ref, ga_ref, w_ref, x_ref, gt_ref, o_ref, y_ref):
    _, ta, cc = yc_ref.shape

    @pl.when(pl.program_id(1) == 0)
    def _():
        n = RES * ta
        ya = ya_ref[...].reshape(n, ya_ref.shape[2]).astype(F32)
        ms = jnp.mean(ya * ya, axis=-1, keepdims=True)
        yn = (ya * lax.rsqrt(ms + EPS) * ga_ref[...]).astype(BF16)
        perm = _row_permutation(ta, False)
        y_ref[:, 0:cc] = jnp.dot(perm, yc_ref[...].reshape(n, cc), preferred_element_type=F32).astype(BF16)
        y_ref[:, cc:] = jnp.dot(perm, yn, preferred_element_type=F32).astype(BF16)

    acc = jnp.dot(y_ref[...], w_ref[...], preferred_element_type=F32)
    o_ref[...] = x_ref[...] + gt_ref[...] * acc


def _outproj(y_conv3, y_attn3, g_attn, w_out_bf, x2d, gt1):
    s, d = x2d.shape
    na = s // RES
    cc, aw = y_conv3.shape[2], y_attn3.shape[2]
    ta = min(32, na)
    tn = min(1024, d)
    return pl.pallas_call(
        _outproj_kernel,
        out_shape=jax.ShapeDtypeStruct((s, d), F32),
        grid=(na // ta, d // tn),
        in_specs=[pl.BlockSpec((RES, ta, cc), lambda i, j: (0, i, 0)),
                  pl.BlockSpec((RES, ta, aw), lambda i, j: (0, i, 0)),
                  pl.BlockSpec((1, aw), lambda i, j: (0, 0)),
                  pl.BlockSpec((cc + aw, tn), lambda i, j: (0, j)),
                  pl.BlockSpec((RES * ta, tn), lambda i, j: (i, j)),
                  pl.BlockSpec((1, tn), lambda i, j: (0, j))],
        out_specs=pl.BlockSpec((RES * ta, tn), lambda i, j: (i, j)),
        scratch_shapes=[pltpu.VMEM((RES * ta, cc + aw), BF16)],
        compiler_params=_cparams(("arbitrary", "arbitrary")),
        name="outproj",
    )(y_conv3, y_attn3, g_attn, w_out_bf, x2d, gt1)


def _router_kernel(x_ref, g_ref, sc_ref, sh_ref, wr_ref, h_ref, eid_ref, wt_ref):
    h = _rms_mod(x_ref[...], g_ref[...], sc_ref[...], sh_ref[...])
    h_ref[...] = h
    w = wr_ref[...]
    h_hi, w_hi = h.astype(BF16), w.astype(BF16)
    h_lo = (h - h_hi.astype(F32)).astype(BF16)
    w_lo = (w - w_hi.astype(F32)).astype(BF16)
    logits = (jnp.dot(h_hi, w_hi, preferred_element_type=F32) + jnp.dot(h_hi, w_lo, preferred_element_type=F32)
              + jnp.dot(h_lo, w_hi, preferred_element_type=F32))
    tm = logits.shape[0]
    lane = lax.broadcasted_iota(jnp.int32, (tm, LANES), 1).astype(F32)
    big = float(LANES)
    is_g = lane < N_GROUPS
    gl = jnp.where(is_g, logits, NEG_INF)
    gmax = jnp.max(gl, axis=-1, keepdims=True)
    gsel = jnp.min(jnp.where(gl == gmax, lane, big), axis=-1, keepdims=True)
    g_w = 1.0 / jnp.sum(jnp.where(is_g, jnp.exp(gl - gmax), 0.0), axis=-1, keepdims=True)
    lo = N_GROUPS + gsel * EXPERTS_PER_GROUP
    in_grp = (lane >= lo) & (lane < lo + EXPERTS_PER_GROUP)
    el = jnp.where(in_grp, logits, NEG_INF)
    e1 = jnp.max(el, axis=-1, keepdims=True)
    i1 = jnp.min(jnp.where(el == e1, lane, big), axis=-1, keepdims=True)
    el2 = jnp.where(lane == i1, NEG_INF, el)
    e2 = jnp.max(el2, axis=-1, keepdims=True)
    i2 = jnp.min(jnp.where(el2 == e2, lane, big), axis=-1, keepdims=True)
    t = jnp.exp(e2 - e1)
    w1 = g_w / (1.0 + t)
    w2 = g_w * t / (1.0 + t)
    eid = jnp.where(lane == 0.0, i1 - N_GROUPS, jnp.where(lane == 1.0, i2 - N_GROUPS, 0.0))
    eid_ref[...] = eid.astype(jnp.int32)
    wt_ref[...] = jnp.where(lane == 0.0, w1, jnp.where(lane == 1.0, w2, 0.0))


def _router(x2, g_ffn, sc2, sh2, w_r):
    s, d = x2.shape
    tm = min(512, s)
    vec = lambda: pl.BlockSpec((1, d), lambda i: (0, 0))
    lanes_out = lambda: pl.BlockSpec((tm, LANES), lambda i: (i, 0))
    return pl.pallas_call(
        _router_kernel,
        out_shape=(jax.ShapeDtypeStruct((s, d), F32),
                   jax.ShapeDtypeStruct((s, LANES), jnp.int32),
                   jax.ShapeDtypeStruct((s, LANES), F32)),
        grid=(s // tm,),
        in_specs=[pl.BlockSpec((tm, d), lambda i: (i, 0)), vec(), vec(), vec(),
                  pl.BlockSpec((d, LANES), lambda i: (0, 0))],
        out_specs=(pl.BlockSpec((tm, d), lambda i: (i, 0)), lanes_out(), lanes_out()),
        compiler_params=_cparams(("arbitrary",)),
        name="router",
    )(x2, g_ffn, sc2, sh2, w_r)


def _rank_kernel(eid_ref, rank_ref, cnt_ref, run_ref):
    @pl.when(pl.program_id(0) == 0)
    def _():
        run_ref[...] = jnp.zeros_like(run_ref)

    tb = eid_ref.shape[0]
    lane = lax.broadcasted_iota(jnp.int32, (tb, LANES), 1)
    row = lax.broadcasted_iota(jnp.int32, (tb, tb), 0)
    col = lax.broadcasted_iota(jnp.int32, (tb, tb), 1)
    earlier = (col < row).astype(BF16)
    e = eid_ref[...]
    run = run_ref[...]
    ranks = []
    for k in range(2):
        onehot = e[:, k:k + 1] == lane
        oh = onehot.astype(F32)
        before = jnp.dot(earlier, oh.astype(BF16), preferred_element_type=F32) + run
        ranks.append(jnp.sum(jnp.where(onehot, before, 0.0), axis=-1, keepdims=True))
        run = run + jnp.sum(oh, axis=0, keepdims=True)
    rank_ref[...] = jnp.where(lane == 0, ranks[0], jnp.where(lane == 1, ranks[1], 0.0)).astype(jnp.int32)
    run_ref[...] = run
    cnt_ref[...] = run


def _rank(eid):
    s = eid.shape[0]
    tb = min(512, s)
    return pl.pallas_call(
        _rank_kernel,
        out_shape=(jax.ShapeDtypeStruct((s, LANES), jnp.int32), jax.ShapeDtypeStruct((1, LANES), F32)),
        grid=(s // tb,),
        in_specs=[pl.BlockSpec((tb, LANES), lambda i: (i, 0))],
        out_specs=(pl.BlockSpec((tb, LANES), lambda i: (i, 0)), pl.BlockSpec((1, LANES), lambda i: (0, 0))),
        scratch_shapes=[pltpu.VMEM((1, LANES), F32)],
        compiler_params=_cparams(("arbitrary",)),
        name="rank",
    )(eid)


def _plan_kernel(eid_ref, rank_ref, cnt_ref, dest_ref, src_ref, blke_ref, blkc_ref, nused_ref, pst_ref,
                 *, blk, n_blk):
    def per_expert(e, nb):
        c = cnt_ref[e]
        pst_ref[e] = nb * blk
        nbe = (c + (blk - 1)) // blk

        def per_block(jb, carry):
            blke_ref[nb + jb] = e
            blkc_ref[nb + jb] = jnp.minimum(blk, c - jb * blk)
            return carry

        lax.fori_loop(0, nbe, per_block, 0)

        def pad_slot(r, carry):
            src_ref[nb * blk + r] = 0
            return carry

        lax.fori_loop(c, (c + (GATHER_GROUP - 1)) // GATHER_GROUP * GATHER_GROUP, pad_slot, 0)
        return nb + nbe

    n_used = lax.fori_loop(0, N_EXPERTS, per_expert, jnp.int32(0))
    nused_ref[0] = n_used
    last_e = blke_ref[n_used - 1]

    def unused(i, carry):
        blke_ref[i] = last_e
        blkc_ref[i] = 0
        return carry

    lax.fori_loop(n_used, n_blk, unused, 0)

    n_tok = eid_ref.shape[0] // 2

    def place(a, carry):
        dst = pst_ref[eid_ref[a]] + rank_ref[a]
        dest_ref[a] = dst
        src_ref[dst] = jnp.where(a >= n_tok, a - n_tok, a)
        return carry

    lax.fori_loop(0, eid_ref.shape[0], place, 0, unroll=16)


def _plan(eid_flat, rank_flat, counts, blk):
    a = eid_flat.shape[0]
    n_blk = a // blk + N_EXPERTS
    smem = lambda: pl.BlockSpec(memory_space=pltpu.SMEM)
    i32 = jnp.int32
    return pl.pallas_call(
        functools.partial(_plan_kernel, blk=blk, n_blk=n_blk),
        out_shape=(jax.ShapeDtypeStruct((a,), i32), jax.ShapeDtypeStruct((n_blk * blk,), i32),
                   jax.ShapeDtypeStruct((n_blk,), i32), jax.ShapeDtypeStruct((n_blk,), i32),
                   jax.ShapeDtypeStruct((1,), i32)),
        in_specs=[smem(), smem(), smem()],
        out_specs=(smem(), smem(), smem(), smem(), smem()),
        scratch_shapes=[pltpu.SMEM((N_EXPERTS,), i32)],
        name="plan",
    )(eid_flat, rank_flat, counts)


def _gather_rows(idx_of_row, n_rows, src_hbm, dst, sem):
    def body(g, carry):
        for rr in range(GATHER_GROUP):
            r = g * GATHER_GROUP + rr
            pltpu.make_async_copy(src_hbm.at[pl.ds(idx_of_row(r), 1), :], dst.at[pl.ds(r, 1), :], sem).start()
        return carry
    lax.fori_loop(0, (n_rows + (GATHER_GROUP - 1)) // GATHER_GROUP, body, 0)


def _wait_rows(n_rows, src_hbm, dst, sem):
    def body(g, carry):
        pltpu.make_async_copy(src_hbm.at[pl.ds(0, GATHER_GROUP), :], dst.at[pl.ds(0, GATHER_GROUP), :], sem).wait()
        return carry
    lax.fori_loop(0, (n_rows + (GATHER_GROUP - 1)) // GATHER_GROUP, body, 0)


def _expert_kernel(blke_ref, blkc_ref, nused_ref, src_ref, h_hbm, w1_hbm, w3_hbm, w2_hbm, ys_hbm,
                   xbuf, obuf, w1b, w3b, w2b, gsem, wsem, osem, *, blk, kc):
    i = pl.program_id(0)
    last = pl.num_programs(0) - 1
    n_used = nused_ref[0]
    e = blke_ref[i]
    e_next = blke_ref[jnp.minimum(i + 1, last)]
    first_of_expert = (i == 0) | (blke_ref[jnp.maximum(i - 1, 0)] != e)
    fetch_next = (i + 1 < n_used) & (e_next != e)
    wbufs = ((w1_hbm, w1b), (w3_hbm, w3b), (w2_hbm, w2b))

    def weight_copy(k, ee):
        return pltpu.make_async_copy(wbufs[k][0].at[ee], wbufs[k][1], wsem.at[k])

    def issue_rows(b, slot):
        _gather_rows(lambda r: src_ref[b * blk + r], blkc_ref[b], h_hbm, xbuf.at[slot], gsem.at[slot])

    def weights_ready(k):
        @pl.when(first_of_expert)
        def _():
            weight_copy(k, e).wait()

    def weights_done(k):
        @pl.when(fetch_next)
        def _():
            weight_copy(k, e_next).start()

    def out_copy(b, slot, ch):
        return pltpu.make_async_copy(obuf.at[slot, pl.ds(ch * OUT_CHUNK, OUT_CHUNK), :],
                                     ys_hbm.at[pl.ds(b * blk + ch * OUT_CHUNK, OUT_CHUNK), :], osem.at[slot])

    def for_out_chunks(b, fn):
        def body(ch, carry):
            fn(ch)
            return carry
        lax.fori_loop(0, (blkc_ref[b] + (OUT_CHUNK - 1)) // OUT_CHUNK, body, 0)

    slot = lax.rem(i, 2)

    @pl.when(i == 0)
    def _():
        for k in range(3):
            weight_copy(k, e).start()
        xbuf[...] = jnp.zeros_like(xbuf)
        issue_rows(0, 0)

    @pl.when(i < n_used)
    def _():
        _wait_rows(blkc_ref[i], h_hbm, xbuf.at[slot], gsem.at[slot])

        @pl.when(i + 1 < n_used)
        def _():
            issue_rows(i + 1, 1 - slot)

        d = xbuf.shape[2]

        def up_proj(wb):
            acc = None
            for k0 in range(0, d, kc):
                part = jnp.dot(xbuf[slot, :, k0:k0 + kc].astype(BF16), wb[k0:k0 + kc, :].astype(BF16),
                               preferred_element_type=F32)
                acc = part if acc is None else acc + part
            return acc

        weights_ready(0)
        a = up_proj(w1b)
        weights_done(0)
        weights_ready(1)
        b = up_proj(w3b)
        weights_done(1)
        hid = ((a * jax.nn.sigmoid(a)) * b).astype(BF16)
        weights_ready(2)

        @pl.when(i >= 2)
        def _():
            for_out_chunks(i - 2, lambda ch: out_copy(i - 2, slot, 0).wait())

        for t in range(0, d, Y_TILE):
            obuf[slot, :, t:t + Y_TILE] = jnp.dot(hid, w2b[:, t:t + Y_TILE].astype(BF16),
                                                  preferred_element_type=F32)
        weights_done(2)
        for_out_chunks(i, lambda ch: out_copy(i, slot, ch).start())

        @pl.when(i == n_used - 1)
        def _():
            @pl.when(i >= 1)
            def _():
                for_out_chunks(i - 1, lambda ch: out_copy(i - 1, 1 - slot, 0).wait())
            for_out_chunks(i, lambda ch: out_copy(i, slot, 0).wait())


def _experts(h2, w1, w3, w2, blk_e, blk_cnt, n_used, row_src, blk):
    s, d = h2.shape
    de = w1.shape[-1]
    kc = min(1024, d)
    assert d % kc == 0 and d % Y_TILE == 0 and blk % GATHER_GROUP == 0 and blk % OUT_CHUNK == 0
    n_blk = blk_e.shape[0]
    anyspace = lambda: pl.BlockSpec(memory_space=pl.ANY)
    return pl.pallas_call(
        functools.partial(_expert_kernel, blk=blk, kc=kc),
        out_shape=jax.ShapeDtypeStruct((n_blk * blk, d), F32),
        grid_spec=pltpu.PrefetchScalarGridSpec(
            num_scalar_prefetch=4,
            grid=(n_blk,),
            in_specs=[anyspace(), anyspace(), anyspace(), anyspace()],
            out_specs=anyspace(),
            scratch_shapes=[pltpu.VMEM((2, blk, d), F32), pltpu.VMEM((2, blk, d), F32),
                            pltpu.VMEM((d, de), F32), pltpu.VMEM((d, de), F32), pltpu.VMEM((de, d), F32),
                            pltpu.SemaphoreType.DMA((2,)), pltpu.SemaphoreType.DMA((3,)),
                            pltpu.SemaphoreType.DMA((2,))]),
        compiler_params=_cparams(("arbitrary",)),
        name="experts",
    )(blk_e, blk_cnt, n_used, row_src, h2, w1, w3, w2)


def _combine_kernel(dest_ref, ys_hbm, x_ref, wt_ref, gt_ref, o_ref, ybuf, sem):
    i = pl.program_id(0)
    nb = pl.num_programs(0)
    tm = x_ref.shape[0]
    n_tok = tm * nb

    def issue(blk_i, slot):
        for k in range(2):
            _gather_rows(lambda r: dest_ref[k * n_tok + blk_i * tm + r], tm, ys_hbm, ybuf.at[slot, k],
                         sem.at[slot])

    slot = lax.rem(i, 2)

    @pl.when(i == 0)
    def _():
        issue(0, 0)

    @pl.when(i + 1 < nb)
    def _():
        issue(i + 1, 1 - slot)

    for k in range(2):
        _wait_rows(tm, ys_hbm, ybuf.at[slot, k], sem.at[slot])
    w = wt_ref[...]
    y = ybuf[slot, 0] * w[:, 0:1] + ybuf[slot, 1] * w[:, 1:2]
    o_ref[...] = x_ref[...] + gt_ref[...] * y


def _combine(dest, ys, x2, wts, gt2):
    s, d = x2.shape
    tm = min(256, s)
    return pl.pallas_call(
        _combine_kernel,
        out_shape=jax.ShapeDtypeStruct((s, d), F32),
        grid_spec=pltpu.PrefetchScalarGridSpec(
            num_scalar_prefetch=1,
            grid=(s // tm,),
            in_specs=[pl.BlockSpec(memory_space=pl.ANY),
                      pl.BlockSpec((tm, d), lambda i, ds_: (i, 0)),
                      pl.BlockSpec((tm, LANES), lambda i, ds_: (i, 0)),
                      pl.BlockSpec((1, d), lambda i, ds_: (0, 0))],
            out_specs=pl.BlockSpec((tm, d), lambda i, ds_: (i, 0)),
            scratch_shapes=[pltpu.VMEM((2, 2, tm, d), F32), pltpu.SemaphoreType.DMA((2,))]),
        compiler_params=_cparams(("arbitrary",)),
        name="combine",
    )(dest, ys, x2, wts, gt2)


def _layer(x2d, c, w_ada, b_ada, g_mix, g_ffn, w_in, conv_w, q_norm_g, k_norm_g, g_branch, w_out,
           w_group, w_expert, w1, w3, w2):
    s, d = x2d.shape
    conv_ch = conv_w.shape[1]
    attn_w = d - conv_ch
    n_heads = attn_w // HEAD_DIM
    na = s // RES
    assert s % (RES * BAND) == 0
    row = lambda v: v.reshape(1, -1)

    mod_a = _ada(c, w_ada, b_ada, 2 * d)
    sh1, sc1 = mod_a[:, :d], mod_a[:, d:]

    h1 = _norm1(x2d, row(g_mix), sc1, sh1)
    z, mod_b = _inproj(h1.reshape(s, d), w_in, row(q_norm_g), row(k_norm_g), conv_ch, attn_w,
                       c, w_ada, b_ada, 2 * d)
    gt1, sh2, sc2, gt2 = [mod_b[:, i * d:(i + 1) * d] for i in range(4)]
    z3 = z.reshape(RES, na, z.shape[1])
    y_conv = _conv(z3, conv_w, row(g_branch[:conv_ch]), conv_ch)
    slopes = jnp.exp2(-ALIBI_MAX_EXP * jnp.arange(1, n_heads + 1, dtype=F32) / n_heads)
    y_attn = _attention(z3, slopes, conv_ch, attn_w)
    x2 = _outproj(y_conv, y_attn, row(g_branch[conv_ch:]), w_out.astype(BF16), x2d, gt1)

    w_r = jnp.concatenate([w_group, w_expert,
                           jnp.zeros((d, LANES - N_GROUPS - N_EXPERTS), F32)], axis=1)
    h2, eid, wts = _router(x2, row(g_ffn), sc2, sh2, w_r)
    rank, counts = _rank(eid)
    eid_flat = eid[:, :2].T.reshape(-1)
    rank_flat = rank[:, :2].T.reshape(-1)
    cnt = counts[0, :N_EXPERTS].astype(jnp.int32)
    dest, row_src, blk_e, blk_cnt, n_used = _plan(eid_flat, rank_flat, cnt, MOE_BLOCK)
    ys = _experts(h2, w1, w3, w2, blk_e, blk_cnt, n_used, row_src, MOE_BLOCK)
    return _combine(dest, ys, x2, wts, gt2)


def kernel(x, c, w_ada, b_ada, g_mix, g_ffn, w_in, conv_w, q_norm_g, k_norm_g, g_branch, w_out,
           w_group, w_expert, w1, w3, w2):
    b, s, d = x.shape
    assert b == 1 and c.shape[0] == 1
    for l in range(w_ada.shape[0]):
        x = _layer(x.reshape(s, d), c[0], w_ada[l], b_ada[l], g_mix[l], g_ffn[l], w_in[l], conv_w[l],
                   q_norm_g[l], k_norm_g[l], g_branch[l], w_out[l], w_group[l], w_expert[l],
                   w1[l], w3[l], w2[l]).reshape(b, s, d)
    return x
```

```python
import functools

import jax
import jax.numpy as jnp
from jax import lax
from jax.experimental import pallas as pl
from jax.experimental.pallas import tpu as pltpu

EPS = 1e-6
HEAD_DIM = 128
BAND = 128
PATTERNS = ((128, 1), (512, 4), (2048, 16))
RES = 16
ALIBI_MAX_EXP = 8.0
N_GROUPS = 8
EXPERTS_PER_GROUP = 8
N_EXPERTS = N_GROUPS * EXPERTS_PER_GROUP
MOE_BLOCK = 256
GATHER_GROUP = 8
Y_TILE = 512
OUT_CHUNK = 64
ADA_SLAB = 128
PERM_ROWS = 16
LANES = 128
SUBLANES = 8
VMEM_LIMIT = 56 * 1024 * 1024
F32 = jnp.float32
BF16 = jnp.bfloat16
NEG_INF = float("-inf")
LOG2E = 1.4426950408889634


def _cparams(sem):
    return pltpu.CompilerParams(dimension_semantics=sem, vmem_limit_bytes=VMEM_LIMIT)


def _ada_kernel(c_ref, w_ref, b_ref, o_ref, s_ref, *, rows):
    @pl.when(pl.program_id(0) == 0)
    def _():
        c = c_ref[...]
        s_ref[...] = c * jax.nn.sigmoid(c)

    d, tn = w_ref.shape

    def body(i, acc):
        r0 = pl.multiple_of(i * rows, rows)
        p = w_ref[pl.ds(r0, rows), :] * s_ref[pl.ds(r0, rows), :]
        return acc + jnp.sum(p.reshape(rows // SUBLANES, SUBLANES, tn), axis=0)

    acc = lax.fori_loop(0, d // rows, body, jnp.zeros((SUBLANES, tn), F32))
    o_ref[...] = jnp.sum(acc, axis=0, keepdims=True) + b_ref[...]


def _ada_slab(w_ref, s_ref, b_ref):
    d, tn = w_ref.shape
    rows = min(512, d)
    acc = jnp.zeros((SUBLANES, tn), F32)
    for r0 in range(0, d, rows):
        p = w_ref[r0:r0 + rows, :] * s_ref[r0:r0 + rows, :]
        acc = acc + jnp.sum(p.reshape(rows // SUBLANES, SUBLANES, tn), axis=0)
    return jnp.sum(acc, axis=0, keepdims=True) + b_ref[...]


def _ada(c, w_ada, b_ada, n):
    d = w_ada.shape[0]
    tn = min(512, n)
    rows = min(512, d)
    return pl.pallas_call(
        functools.partial(_ada_kernel, rows=rows),
        out_shape=jax.ShapeDtypeStruct((1, n), F32),
        grid=(n // tn,),
        in_specs=[pl.BlockSpec((d, 1), lambda j: (0, 0)),
                  pl.BlockSpec((d, tn), lambda j: (0, j)),
                  pl.BlockSpec((1, tn), lambda j: (0, j))],
        out_specs=pl.BlockSpec((1, tn), lambda j: (0, j)),
        scratch_shapes=[pltpu.VMEM((d, 1), F32)],
        compiler_params=_cparams(("arbitrary",)),
        name="ada",
    )(c.reshape(d, 1), w_ada, b_ada.reshape(1, -1))


def _rms_mod(x, g, sc, sh):
    ms = jnp.mean(x * x, axis=-1, keepdims=True)
    return (x * lax.rsqrt(ms + EPS) * g) * (1.0 + sc) + sh


def _row_permutation(ta, to_residue_major):
    n = RES * ta
    assert ta & (ta - 1) == 0
    sh = ta.bit_length() - 1
    r = lax.broadcasted_iota(jnp.int32, (n, n), 0)
    c = lax.broadcasted_iota(jnp.int32, (n, n), 1)
    res_major, natural = (r, c) if to_residue_major else (c, r)
    return (natural == RES * (res_major & (ta - 1)) + (res_major >> sh)).astype(BF16)


def _norm1_kernel(x_ref, g_ref, sc_ref, sh_ref, o_ref):
    ta = o_ref.shape[1]
    h = _rms_mod(x_ref[...], g_ref[...], sc_ref[...], sh_ref[...]).astype(BF16)
    hp = jnp.dot(_row_permutation(ta, True), h, preferred_element_type=F32)
    o_ref[...] = hp.astype(BF16).reshape(o_ref.shape)


def _norm1(x2d, g_mix, sc1, sh1):
    s, d = x2d.shape
    na = s // RES
    ta = min(32, na)
    vec = lambda: pl.BlockSpec((1, d), lambda i: (0, 0))
    return pl.pallas_call(
        _norm1_kernel,
        out_shape=jax.ShapeDtypeStruct((RES, na, d), BF16),
        grid=(na // ta,),
        in_specs=[pl.BlockSpec((RES * ta, d), lambda i: (i, 0)), vec(), vec(), vec()],
        out_specs=pl.BlockSpec((RES, ta, d), lambda i: (0, i, 0)),
        compiler_params=_cparams(("arbitrary",)),
        name="norm1",
    )(x2d, g_mix, sc1, sh1)


def _inproj_kernel(h_ref, w_ref, qg_ref, kg_ref, c_ref, wa_ref, ba_ref, z_ref, mod_ref, wb_ref, s_ref,
                   *, q0, k0, v0):
    j = pl.program_id(0)

    @pl.when((j == 0) & (pl.program_id(1) == 0))
    def _():
        c = c_ref[...]
        s_ref[...] = c * jax.nn.sigmoid(c)

    @pl.when(pl.program_id(1) == 0)
    def _():
        wb_ref[...] = w_ref[...].astype(BF16)

    mod_ref[...] = _ada_slab(wa_ref, s_ref, ba_ref)
    acc = jnp.dot(h_ref[...], wb_ref[...], preferred_element_type=F32)
    is_q = (j >= q0) & (j < k0)
    is_k = (j >= k0) & (j < v0)

    g = jnp.where(is_q, qg_ref[...] * (HEAD_DIM ** -0.5 * LOG2E), kg_ref[...])
    for hh in range(acc.shape[1] // HEAD_DIM):
        a = acc[:, hh * HEAD_DIM:(hh + 1) * HEAD_DIM]
        ms = jnp.mean(a * a, axis=-1, keepdims=True)
        scale = jnp.where(is_q | is_k, lax.rsqrt(ms + EPS) * g, 1.0)
        z_ref[:, hh * HEAD_DIM:(hh + 1) * HEAD_DIM] = a * scale


def _inproj(h1, w_in, q_g, k_g, conv_ch, attn_w, c, w_ada, b_ada, mod_done):
    s, d = h1.shape
    n = w_in.shape[1]
    tm = min(1024, s)
    tn = min(512, conv_ch)
    assert conv_ch % tn == 0 and attn_w % tn == 0 and tn % HEAD_DIM == 0
    q0 = 3 * conv_ch // tn
    k0 = q0 + attn_w // tn
    v0 = k0 + attn_w // tn
    ni = s // tm
    n_mod = w_ada.shape[1] - mod_done
    n_slab = n_mod // ADA_SLAB
    assert n_mod % ADA_SLAB == 0 and mod_done % ADA_SLAB == 0 and n_slab <= (n // tn) * ni
    slab = lambda j, i: jnp.minimum(j * ni + i, n_slab - 1)
    hvec = lambda: pl.BlockSpec((1, HEAD_DIM), lambda j, i: (0, 0))
    return pl.pallas_call(
        functools.partial(_inproj_kernel, q0=q0, k0=k0, v0=v0),
        out_shape=(jax.ShapeDtypeStruct((s, n), F32), jax.ShapeDtypeStruct((1, n_mod), F32)),
        grid=(n // tn, ni),
        in_specs=[pl.BlockSpec((tm, d), lambda j, i: (i, 0)),
                  pl.BlockSpec((d, tn), lambda j, i: (0, j)), hvec(), hvec(),
                  pl.BlockSpec((d, 1), lambda j, i: (0, 0)),
                  pl.BlockSpec((d, ADA_SLAB), lambda j, i: (0, mod_done // ADA_SLAB + slab(j, i))),
                  pl.BlockSpec((1, ADA_SLAB), lambda j, i: (0, mod_done // ADA_SLAB + slab(j, i)))],
        out_specs=(pl.BlockSpec((tm, tn), lambda j, i: (i, j)),
                   pl.BlockSpec((1, ADA_SLAB), lambda j, i: (0, slab(j, i)))),
        scratch_shapes=[pltpu.VMEM((d, tn), BF16), pltpu.VMEM((d, 1), F32)],
        compiler_params=_cparams(("arbitrary", "arbitrary")),
        name="inproj",
    )(h1, w_in, q_g, k_g, c.reshape(d, 1), w_ada, b_ada.reshape(1, -1))


def _conv_kernel(b_ref, c_ref, v_ref, cw_ref, g_ref, o_ref, ue_ref):
    i = pl.program_id(0)
    ta = b_ref.shape[1]
    w = cw_ref[...]
    assert w.shape[0] == 3
    for k in range(2):
        @pl.when(i == 0)
        def _():
            ue_ref[k, 0:SUBLANES, :] = jnp.zeros((SUBLANES, ue_ref.shape[2]), F32)

        @pl.when(i > 0)
        def _():
            ue_ref[k, 0:SUBLANES, :] = ue_ref[k, ta:ta + SUBLANES, :]

        r = RES - 2 + k
        ue_ref[k, SUBLANES:SUBLANES + ta, :] = c_ref[r] * v_ref[r]

    def u_at(bb):
        if bb >= 0:
            return c_ref[bb] * v_ref[bb]
        return ue_ref[2 + bb, SUBLANES - 1:SUBLANES - 1 + ta, :]

    for b in range(RES):
        y = b_ref[b] * (w[0:1, :] * u_at(b - 2) + w[1:2, :] * u_at(b - 1) + w[2:3, :] * u_at(b))
        ms = jnp.mean(y * y, axis=-1, keepdims=True)
        o_ref[b] = (y * lax.rsqrt(ms + EPS) * g_ref[...]).astype(BF16)


def _conv(z3, conv_w, g_conv, conv_ch):
    na = z3.shape[1]
    ta = min(64, na)
    blk = lambda jj: pl.BlockSpec((RES, ta, conv_ch), lambda i: (0, i, jj))
    return pl.pallas_call(
        _conv_kernel,
        out_shape=jax.ShapeDtypeStruct((RES, na, conv_ch), BF16),
        grid=(na // ta,),
        in_specs=[blk(0), blk(1), blk(2),
                  pl.BlockSpec(conv_w.shape, lambda i: (0, 0)),
                  pl.BlockSpec((1, conv_ch), lambda i: (0, 0))],
        out_specs=pl.BlockSpec((RES, ta, conv_ch), lambda i: (0, i, 0)),
        scratch_shapes=[pltpu.VMEM((2, ta + SUBLANES, conv_ch), F32)],
        compiler_params=_cparams(("arbitrary",)),
        name="conv",
    )(z3, z3, z3, conv_w, g_conv)


def _attn_kernel(slopes_ref, q_ref, kp_ref, kc_ref, vp_ref, vc_ref, o_ref, bias_ref, m_ref, l_ref, acc_ref):
    h = pl.program_id(0)
    cidx = pl.program_id(1)
    slope = slopes_ref[h] * LOG2E
    first_pen = jnp.where(cidx > 0, 0.0, NEG_INF).astype(F32)
    dn_t = (((1,), (1,)), ((), ()))
    ones_kv = jnp.ones((2 * BAND, HEAD_DIM), BF16)

    j = lax.broadcasted_iota(jnp.int32, (BAND, 2 * BAND), 0)
    u = lax.broadcasted_iota(jnp.int32, (BAND, 2 * BAND), 1)
    is_prev = u < BAND
    uu = jnp.where(is_prev, u, u - BAND)
    for pi, (window, dil) in enumerate(PATTERNS):
        assert window // dil == BAND and RES % dil == 0
        ns, sr = RES // dil, BAND * dil // RES
        assert sr & (sr - 1) == 0
        sh = sr.bit_length() - 1
        mq = (j & (sr - 1)) * ns + (j >> sh)
        mk = (uu & (sr - 1)) * ns + (uu >> sh)
        steps = mq - mk + jnp.where(is_prev, BAND, 0)
        valid = (is_prev & (steps <= BAND)) | (jnp.logical_not(is_prev) & (steps >= 0))
        bias = jnp.where(valid, -slope * (steps * dil).astype(F32), NEG_INF)
        bias_ref[pi, 0] = bias
        bias_ref[pi, 1] = bias + jnp.where(is_prev, first_pen, 0.0)

    for pi, (window, dil) in enumerate(PATTERNS):
        ns, sr = RES // dil, BAND * dil // RES
        n_chunks = BAND // sr
        for r in range(dil):
            k_prev = v_prev = None
            for n in range(n_chunks):
                def gather(ref, nn):
                    return jnp.concatenate([ref[dil * c + r, nn * sr:(nn + 1) * sr, :] for c in range(ns)], axis=0)

                q = gather(q_ref, n).astype(BF16)
                if n == 0:
                    k_prev = gather(kp_ref, n_chunks - 1).astype(BF16)
                    v_prev = gather(vp_ref, n_chunks - 1).astype(BF16)
                k_cur, v_cur = gather(kc_ref, n).astype(BF16), gather(vc_ref, n).astype(BF16)
                kcat = jnp.concatenate([k_prev, k_cur], axis=0)
                vcat = jnp.concatenate([v_prev, v_cur], axis=0)
                k_prev, v_prev = k_cur, v_cur
                s = lax.dot_general(q, kcat, dn_t, preferred_element_type=F32) + bias_ref[pi, 1 if n == 0 else 0]
                m = jnp.max(s, axis=-1, keepdims=True)
                p = jnp.exp2(s - m).astype(BF16)
                ol = jnp.dot(p, jnp.concatenate([vcat, ones_kv], axis=1), preferred_element_type=F32)
                o, l_all = ol[:, :HEAD_DIM], ol[:, HEAD_DIM:]
                for c in range(ns):
                    rows = slice(c * sr, (c + 1) * sr)
                    at = (dil * c + r, slice(n * sr, (n + 1) * sr), slice(None))
                    m_b = jnp.broadcast_to(m[rows], (sr, HEAD_DIM))
                    l_b = l_all[rows]
                    if pi == 0:
                        m_ref[at] = m_b
                        l_ref[at] = l_b
                        acc_ref[at] = o[rows]
                    else:
                        m_old = m_ref[at]
                        m_new = jnp.maximum(m_old, m_b)
                        a_old = jnp.exp2(m_old - m_new)
                        a_new = jnp.exp2(m_b - m_new)
                        l_new = a_old * l_ref[at] + a_new * l_b
                        o_new = a_old * acc_ref[at] + a_new * o[rows]
                        if pi == len(PATTERNS) - 1:
                            acc_ref[at] = o_new / l_new
                        else:
                            m_ref[at] = m_new
                            l_ref[at] = l_new
                            acc_ref[at] = o_new
    o_ref[...] = acc_ref[...].astype(BF16)


def _attention(z3, slopes, conv_ch, attn_w):
    na = z3.shape[1]
    n_heads = attn_w // HEAD_DIM
    assert na % BAND == 0
    qb = 3 * conv_ch // HEAD_DIM
    kb = qb + n_heads
    vb = kb + n_heads
    blk = (RES, BAND, HEAD_DIM)
    cur = lambda base: pl.BlockSpec(blk, lambda h, c, sl: (0, c, base + h))
    prev = lambda base: pl.BlockSpec(blk, lambda h, c, sl: (0, jnp.maximum(c - 1, 0), base + h))
    return pl.pallas_call(
        _attn_kernel,
        out_shape=jax.ShapeDtypeStruct((RES, na, attn_w), BF16),
        grid_spec=pltpu.PrefetchScalarGridSpec(
            num_scalar_prefetch=1,
            grid=(n_heads, na // BAND),
            in_specs=[cur(qb), prev(kb), cur(kb), prev(vb), cur(vb)],
            out_specs=pl.BlockSpec(blk, lambda h, c, sl: (0, c, h)),
            scratch_shapes=[pltpu.VMEM((len(PATTERNS), 2, BAND, 2 * BAND), F32),
                            pltpu.VMEM(blk, F32), pltpu.VMEM(blk, F32), pltpu.VMEM(blk, F32)]),
        compiler_params=_cparams(("arbitrary", "arbitrary")),
        name="attn",
    )(slopes, z3, z3, z3, z3, z3)


def _outproj_kernel(yc_ref, ya_ref, ga_ref, w_ref, x_ref, gt_ref, o_ref, y_ref):
    _, ta, cc = yc_ref.shape

    @pl.when(pl.program_id(1) == 0)
    def _():
        tp = min(PERM_ROWS, ta)
        n = RES * tp
        perm = _row_permutation(tp, False)
        for t0 in range(0, ta, tp):
            ya = ya_ref[:, t0:t0 + tp, :].reshape(n, ya_ref.shape[2]).astype(F32)
            ms = jnp.mean(ya * ya, axis=-1, keepdims=True)
            yn = (ya * lax.rsqrt(ms + EPS) * ga_ref[...]).astype(BF16)
            yc = yc_ref[:, t0:t0 + tp, :].reshape(n, cc)
            rows = slice(RES * t0, RES * t0 + n)
            y_ref[rows, 0:cc] = jnp.dot(perm, yc, preferred_element_type=F32).astype(BF16)
            y_ref[rows, cc:] = jnp.dot(perm, yn, preferred_element_type=F32).astype(BF16)

    acc = jnp.dot(y_ref[...], w_ref[...], preferred_element_type=F32)
    o_ref[...] = x_ref[...] + gt_ref[...] * acc


def _outproj(y_conv3, y_attn3, g_attn, w_out_bf, x2d, gt1):
    s, d = x2d.shape
    na = s // RES
    cc, aw = y_conv3.shape[2], y_attn3.shape[2]
    ta = min(32, na)
    tn = min(1024, d)
    return pl.pallas_call(
        _outproj_kernel,
        out_shape=jax.ShapeDtypeStruct((s, d), F32),
        grid=(na // ta, d // tn),
        in_specs=[pl.BlockSpec((RES, ta, cc), lambda i, j: (0, i, 0)),
                  pl.BlockSpec((RES, ta, aw), lambda i, j: (0, i, 0)),
                  pl.BlockSpec((1, aw), lambda i, j: (0, 0)),
                  pl.BlockSpec((cc + aw, tn), lambda i, j: (0, j)),
                  pl.BlockSpec((RES * ta, tn), lambda i, j: (i, j)),
                  pl.BlockSpec((1, tn), lambda i, j: (0, j))],
        out_specs=pl.BlockSpec((RES * ta, tn), lambda i, j: (i, j)),
        scratch_shapes=[pltpu.VMEM((RES * ta, cc + aw), BF16)],
        compiler_params=_cparams(("arbitrary", "arbitrary")),
        name="outproj",
    )(y_conv3, y_attn3, g_attn, w_out_bf, x2d, gt1)


def _router_kernel(x_ref, g_ref, sc_ref, sh_ref, wr_ref, h_ref, eid_ref, wt_ref):
    h = _rms_mod(x_ref[...], g_ref[...], sc_ref[...], sh_ref[...])
    h_ref[...] = h
    w = wr_ref[...]
    h_hi, w_hi = h.astype(BF16), w.astype(BF16)
    h_lo = (h - h_hi.astype(F32)).astype(BF16)
    w_lo = (w - w_hi.astype(F32)).astype(BF16)
    logits = (jnp.dot(h_hi, w_hi, preferred_element_type=F32) + jnp.dot(h_hi, w_lo, preferred_element_type=F32)
              + jnp.dot(h_lo, w_hi, preferred_element_type=F32))
    tm = logits.shape[0]
    lane = lax.broadcasted_iota(jnp.int32, (tm, LANES), 1).astype(F32)
    big = float(LANES)
    is_g = lane < N_GROUPS
    gl = jnp.where(is_g, logits, NEG_INF)
    gmax = jnp.max(gl, axis=-1, keepdims=True)
    gsel = jnp.min(jnp.where(gl == gmax, lane, big), axis=-1, keepdims=True)
    g_w = 1.0 / jnp.sum(jnp.where(is_g, jnp.exp(gl - gmax), 0.0), axis=-1, keepdims=True)
    lo = N_GROUPS + gsel * EXPERTS_PER_GROUP
    in_grp = (lane >= lo) & (lane < lo + EXPERTS_PER_GROUP)
    el = jnp.where(in_grp, logits, NEG_INF)
    e1 = jnp.max(el, axis=-1, keepdims=True)
    i1 = jnp.min(jnp.where(el == e1, lane, big), axis=-1, keepdims=True)
    el2 = jnp.where(lane == i1, NEG_INF, el)
    e2 = jnp.max(el2, axis=-1, keepdims=True)
    i2 = jnp.min(jnp.where(el2 == e2, lane, big), axis=-1, keepdims=True)
    t = jnp.exp(e2 - e1)
    w1 = g_w / (1.0 + t)
    w2 = g_w * t / (1.0 + t)
    eid = jnp.where(lane == 0.0, i1 - N_GROUPS, jnp.where(lane == 1.0, i2 - N_GROUPS, 0.0))
    eid_ref[...] = eid.astype(jnp.int32)
    wt_ref[...] = jnp.where(lane == 0.0, w1, jnp.where(lane == 1.0, w2, 0.0))


def _router(x2, g_ffn, sc2, sh2, w_r):
    s, d = x2.shape
    tm = min(512, s)
    vec = lambda: pl.BlockSpec((1, d), lambda i: (0, 0))
    lanes_out = lambda: pl.BlockSpec((tm, LANES), lambda i: (i, 0))
    return pl.pallas_call(
        _router_kernel,
        out_shape=(jax.ShapeDtypeStruct((s, d), F32),
                   jax.ShapeDtypeStruct((s, LANES), jnp.int32),
                   jax.ShapeDtypeStruct((s, LANES), F32)),
        grid=(s // tm,),
        in_specs=[pl.BlockSpec((tm, d), lambda i: (i, 0)), vec(), vec(), vec(),
                  pl.BlockSpec((d, LANES), lambda i: (0, 0))],
        out_specs=(pl.BlockSpec((tm, d), lambda i: (i, 0)), lanes_out(), lanes_out()),
        compiler_params=_cparams(("arbitrary",)),
        name="router",
    )(x2, g_ffn, sc2, sh2, w_r)


def _rank_kernel(eid_ref, rank_ref, cnt_ref, run_ref):
    @pl.when(pl.program_id(0) == 0)
    def _():
        run_ref[...] = jnp.zeros_like(run_ref)

    tb = eid_ref.shape[0]
    lane = lax.broadcasted_iota(jnp.int32, (tb, LANES), 1)
    row = lax.broadcasted_iota(jnp.int32, (tb, tb), 0)
    col = lax.broadcasted_iota(jnp.int32, (tb, tb), 1)
    earlier = (col < row).astype(BF16)
    e = eid_ref[...]
    run = run_ref[...]
    ranks = []
    for k in range(2):
        onehot = e[:, k:k + 1] == lane
        oh = onehot.astype(F32)
        before = jnp.dot(earlier, oh.astype(BF16), preferred_element_type=F32) + run
        ranks.append(jnp.sum(jnp.where(onehot, before, 0.0), axis=-1, keepdims=True))
        run = run + jnp.sum(oh, axis=0, keepdims=True)
    rank_ref[...] = jnp.where(lane == 0, ranks[0], jnp.where(lane == 1, ranks[1], 0.0)).astype(jnp.int32)
    run_ref[...] = run
    cnt_ref[...] = run


def _rank(eid):
    s = eid.shape[0]
    tb = min(512, s)
    return pl.pallas_call(
        _rank_kernel,
        out_shape=(jax.ShapeDtypeStruct((s, LANES), jnp.int32), jax.ShapeDtypeStruct((1, LANES), F32)),
        grid=(s // tb,),
        in_specs=[pl.BlockSpec((tb, LANES), lambda i: (i, 0))],
        out_specs=(pl.BlockSpec((tb, LANES), lambda i: (i, 0)), pl.BlockSpec((1, LANES), lambda i: (0, 0))),
        scratch_shapes=[pltpu.VMEM((1, LANES), F32)],
        compiler_params=_cparams(("arbitrary",)),
        name="rank",
    )(eid)


def _plan_kernel(eid_ref, rank_ref, cnt_ref, dest_ref, src_ref, blke_ref, blkc_ref, nused_ref, pst_ref,
                 *, blk, n_blk):
    def per_expert(e, nb):
        c = cnt_ref[e]
        pst_ref[e] = nb * blk
        nbe = (c + (blk - 1)) // blk

        def per_block(jb, carry):
            blke_ref[nb + jb] = e
            blkc_ref[nb + jb] = jnp.minimum(blk, c - jb * blk)
            return carry

        lax.fori_loop(0, nbe, per_block, 0)

        def pad_slot(r, carry):
            src_ref[nb * blk + r] = 0
            return carry

        lax.fori_loop(c, (c + (GATHER_GROUP - 1)) // GATHER_GROUP * GATHER_GROUP, pad_slot, 0)
        return nb + nbe

    n_used = lax.fori_loop(0, N_EXPERTS, per_expert, jnp.int32(0))
    nused_ref[0] = n_used
    last_e = blke_ref[n_used - 1]

    def unused(i, carry):
        blke_ref[i] = last_e
        blkc_ref[i] = 0
        return carry

    lax.fori_loop(n_used, n_blk, unused, 0)

    n_tok = eid_ref.shape[0] // 2

    for k in range(eid_ref.shape[0] // n_tok):
        def place(tok, carry):
            a = k * n_tok + tok
            dst = pst_ref[eid_ref[a]] + rank_ref[a]
            dest_ref[a] = dst
            src_ref[dst] = tok
            return carry

        lax.fori_loop(0, n_tok, place, 0, unroll=16)


def _plan(eid_flat, rank_flat, counts, blk):
    a = eid_flat.shape[0]
    n_blk = a // blk + N_EXPERTS
    smem = lambda: pl.BlockSpec(memory_space=pltpu.SMEM)
    i32 = jnp.int32
    return pl.pallas_call(
        functools.partial(_plan_kernel, blk=blk, n_blk=n_blk),
        out_shape=(jax.ShapeDtypeStruct((a,), i32), jax.ShapeDtypeStruct((n_blk * blk,), i32),
                   jax.ShapeDtypeStruct((n_blk,), i32), jax.ShapeDtypeStruct((n_blk,), i32),
                   jax.ShapeDtypeStruct((1,), i32)),
        in_specs=[smem(), smem(), smem()],
        out_specs=(smem(), smem(), smem(), smem(), smem()),
        scratch_shapes=[pltpu.SMEM((N_EXPERTS,), i32)],
        name="plan",
    )(eid_flat, rank_flat, counts)


def _gather_rows(idx_of_row, n_rows, src_hbm, dst, sem):
    def body(g, carry):
        for rr in range(GATHER_GROUP):
            r = g * GATHER_GROUP + rr
            pltpu.make_async_copy(src_hbm.at[pl.ds(idx_of_row(r), 1), :], dst.at[pl.ds(r, 1), :], sem).start()
        return carry
    lax.fori_loop(0, (n_rows + (GATHER_GROUP - 1)) // GATHER_GROUP, body, 0)


def _wait_rows(n_rows, src_hbm, dst, sem):
    def body(g, carry):
        pltpu.make_async_copy(src_hbm.at[pl.ds(0, GATHER_GROUP), :], dst.at[pl.ds(0, GATHER_GROUP), :], sem).wait()
        return carry
    lax.fori_loop(0, (n_rows + (GATHER_GROUP - 1)) // GATHER_GROUP, body, 0)


def _expert_kernel(blke_ref, blkc_ref, nused_ref, src_ref, h_hbm, w1_hbm, w3_hbm, w2_hbm, ys_hbm,
                   xbuf, obuf, w1b, w3b, w2b, gsem, wsem, osem, *, blk, kc):
    i = pl.program_id(0)
    last = pl.num_programs(0) - 1
    n_used = nused_ref[0]
    e = blke_ref[i]
    e_next = blke_ref[jnp.minimum(i + 1, last)]
    first_of_expert = (i == 0) | (blke_ref[jnp.maximum(i - 1, 0)] != e)
    fetch_next = (i + 1 < n_used) & (e_next != e)
    wbufs = ((w1_hbm, w1b), (w3_hbm, w3b), (w2_hbm, w2b))

    def weight_copy(k, ee):
        return pltpu.make_async_copy(wbufs[k][0].at[ee], wbufs[k][1], wsem.at[k])

    def issue_rows(b, slot):
        _gather_rows(lambda r: src_ref[b * blk + r], blkc_ref[b], h_hbm, xbuf.at[slot], gsem.at[slot])

    def weights_ready(k):
        @pl.when(first_of_expert)
        def _():
            weight_copy(k, e).wait()

    def weights_done(k):
        @pl.when(fetch_next)
        def _():
            weight_copy(k, e_next).start()

    def out_copy(b, slot, ch):
        return pltpu.make_async_copy(obuf.at[slot, pl.ds(ch * OUT_CHUNK, OUT_CHUNK), :],
                                     ys_hbm.at[pl.ds(b * blk + ch * OUT_CHUNK, OUT_CHUNK), :], osem.at[slot])

    def for_out_chunks(b, fn):
        def body(ch, carry):
            fn(ch)
            return carry
        lax.fori_loop(0, (blkc_ref[b] + (OUT_CHUNK - 1)) // OUT_CHUNK, body, 0)

    slot = lax.rem(i, 2)

    @pl.when(i == 0)
    def _():
        for k in range(3):
            weight_copy(k, e).start()
        xbuf[...] = jnp.zeros_like(xbuf)
        issue_rows(0, 0)

    @pl.when(i < n_used)
    def _():
        _wait_rows(blkc_ref[i], h_hbm, xbuf.at[slot], gsem.at[slot])

        @pl.when(i + 1 < n_used)
        def _():
            issue_rows(i + 1, 1 - slot)

        d = xbuf.shape[2]

        def up_proj(wb):
            acc = None
            for k0 in range(0, d, kc):
                part = jnp.dot(xbuf[slot, :, k0:k0 + kc].astype(BF16), wb[k0:k0 + kc, :].astype(BF16),
                               preferred_element_type=F32)
                acc = part if acc is None else acc + part
            return acc

        weights_ready(0)
        a = up_proj(w1b)
        weights_done(0)
        weights_ready(1)
        b = up_proj(w3b)
        weights_done(1)
        hid = ((a * jax.nn.sigmoid(a)) * b).astype(BF16)
        weights_ready(2)

        @pl.when(i >= 2)
        def _():
            for_out_chunks(i - 2, lambda ch: out_copy(i - 2, slot, 0).wait())

        for t in range(0, d, Y_TILE):
            obuf[slot, :, t:t + Y_TILE] = jnp.dot(hid, w2b[:, t:t + Y_TILE].astype(BF16),
                                                  preferred_element_type=F32)
        weights_done(2)
        for_out_chunks(i, lambda ch: out_copy(i, slot, ch).start())

        @pl.when(i == n_used - 1)
        def _():
            @pl.when(i >= 1)
            def _():
                for_out_chunks(i - 1, lambda ch: out_copy(i - 1, 1 - slot, 0).wait())
            for_out_chunks(i, lambda ch: out_copy(i, slot, 0).wait())


def _experts(h2, w1, w3, w2, blk_e, blk_cnt, n_used, row_src, blk):
    s, d = h2.shape
    de = w1.shape[-1]
    kc = min(1024, d)
    assert d % kc == 0 and d % Y_TILE == 0 and blk % GATHER_GROUP == 0 and blk % OUT_CHUNK == 0
    n_blk = blk_e.shape[0]
    anyspace = lambda: pl.BlockSpec(memory_space=pl.ANY)
    return pl.pallas_call(
        functools.partial(_expert_kernel, blk=blk, kc=kc),
        out_shape=jax.ShapeDtypeStruct((n_blk * blk, d), F32),
        grid_spec=pltpu.PrefetchScalarGridSpec(
            num_scalar_prefetch=4,
            grid=(n_blk,),
            in_specs=[anyspace(), anyspace(), anyspace(), anyspace()],
            out_specs=anyspace(),
            scratch_shapes=[pltpu.VMEM((2, blk, d), F32), pltpu.VMEM((2, blk, d), F32),
                            pltpu.VMEM((d, de), F32), pltpu.VMEM((d, de), F32), pltpu.VMEM((de, d), F32),
                            pltpu.SemaphoreType.DMA((2,)), pltpu.SemaphoreType.DMA((3,)),
                            pltpu.SemaphoreType.DMA((2,))]),
        compiler_params=_cparams(("arbitrary",)),
        name="experts",
    )(blk_e, blk_cnt, n_used, row_src, h2, w1, w3, w2)


def _combine_kernel(dest_ref, ys_hbm, x_ref, wt_ref, gt_ref, o_ref, ybuf, sem):
    i = pl.program_id(0)
    nb = pl.num_programs(0)
    tm = x_ref.shape[0]
    n_tok = tm * nb

    def issue(blk_i, slot):
        for k in range(2):
            _gather_rows(lambda r: dest_ref[k * n_tok + blk_i * tm + r], tm, ys_hbm, ybuf.at[slot, k],
                         sem.at[slot])

    slot = lax.rem(i, 2)

    @pl.when(i == 0)
    def _():
        issue(0, 0)

    @pl.when(i + 1 < nb)
    def _():
        issue(i + 1, 1 - slot)

    for k in range(2):
        _wait_rows(tm, ys_hbm, ybuf.at[slot, k], sem.at[slot])
    w = wt_ref[...]
    y = ybuf[slot, 0] * w[:, 0:1] + ybuf[slot, 1] * w[:, 1:2]
    o_ref[...] = x_ref[...] + gt_ref[...] * y


def _combine(dest, ys, x2, wts, gt2):
    s, d = x2.shape
    tm = min(256, s)
    return pl.pallas_call(
        _combine_kernel,
        out_shape=jax.ShapeDtypeStruct((s, d), F32),
        grid_spec=pltpu.PrefetchScalarGridSpec(
            num_scalar_prefetch=1,
            grid=(s // tm,),
            in_specs=[pl.BlockSpec(memory_space=pl.ANY),
                      pl.BlockSpec((tm, d), lambda i, ds_: (i, 0)),
                      pl.BlockSpec((tm, LANES), lambda i, ds_: (i, 0)),
                      pl.BlockSpec((1, d), lambda i, ds_: (0, 0))],
            out_specs=pl.BlockSpec((tm, d), lambda i, ds_: (i, 0)),
            scratch_shapes=[pltpu.VMEM((2, 2, tm, d), F32), pltpu.SemaphoreType.DMA((2,))]),
        compiler_params=_cparams(("arbitrary",)),
        name="combine",
    )(dest, ys, x2, wts, gt2)


def _layer(x2d, c, w_ada, b_ada, g_mix, g_ffn, w_in, conv_w, q_norm_g, k_norm_g, g_branch, w_out,
           w_group, w_expert, w1, w3, w2):
    s, d = x2d.shape
    conv_ch = conv_w.shape[1]
    attn_w = d - conv_ch
    n_heads = attn_w // HEAD_DIM
    na = s // RES
    assert s % (RES * BAND) == 0
    row = lambda v: v.reshape(1, -1)

    mod_a = _ada(c, w_ada, b_ada, 2 * d)
    sh1, sc1 = mod_a[:, :d], mod_a[:, d:]

    h1 = _norm1(x2d, row(g_mix), sc1, sh1)
    z, mod_b = _inproj(h1.reshape(s, d), w_in, row(q_norm_g), row(k_norm_g), conv_ch, attn_w,
                       c, w_ada, b_ada, 2 * d)
    gt1, sh2, sc2, gt2 = [mod_b[:, i * d:(i + 1) * d] for i in range(4)]
    z3 = z.reshape(RES, na, z.shape[1])
    y_conv = _conv(z3, conv_w, row(g_branch[:conv_ch]), conv_ch)
    slopes = jnp.exp2(-ALIBI_MAX_EXP * jnp.arange(1, n_heads + 1, dtype=F32) / n_heads)
    y_attn = _attention(z3, slopes, conv_ch, attn_w)
    x2 = _outproj(y_conv, y_attn, row(g_branch[conv_ch:]), w_out.astype(BF16), x2d, gt1)

    w_r = jnp.concatenate([w_group, w_expert,
                           jnp.zeros((d, LANES - N_GROUPS - N_EXPERTS), F32)], axis=1)
    h2, eid, wts = _router(x2, row(g_ffn), sc2, sh2, w_r)
    rank, counts = _rank(eid)
    eid_flat = eid[:, :2].T.reshape(-1)
    rank_flat = rank[:, :2].T.reshape(-1)
    cnt = counts[0, :N_EXPERTS].astype(jnp.int32)
    dest, row_src, blk_e, blk_cnt, n_used = _plan(eid_flat, rank_flat, cnt, MOE_BLOCK)
    ys = _experts(h2, w1, w3, w2, blk_e, blk_cnt, n_used, row_src, MOE_BLOCK)
    return _combine(dest, ys, x2, wts, gt2)


def kernel(x, c, w_ada, b_ada, g_mix, g_ffn, w_in, conv_w, q_norm_g, k_norm_g, g_branch, w_out,
           w_group, w_expert, w1, w3, w2):
    b, s, d = x.shape
    assert b == 1 and c.shape[0] == 1
    for l in range(w_ada.shape[0]):
        x = _layer(x.reshape(s, d), c[0], w_ada[l], b_ada[l], g_mix[l], g_ffn[l], w_in[l], conv_w[l],
                   q_norm_g[l], k_norm_g[l], g_branch[l], w_out[l], w_group[l], w_expert[l],
                   w1[l], w3[l], w2[l]).reshape(b, s, d)
    return x
```

```python
import functools
import math

import jax
import jax.numpy as jnp
from jax import lax
from jax.experimental import pallas as pl
from jax.experimental.pallas import tpu as pltpu

EPS = 1e-6
HEAD_DIM = 128
BAND = 128
PATTERNS = ((128, 1), (512, 4), (2048, 16))
RES = 16
ALIBI_MAX_EXP = 8.0
N_GROUPS = 8
EXPERTS_PER_GROUP = 8
N_EXPERTS = N_GROUPS * EXPERTS_PER_GROUP
MOE_BLOCK = 256
GATHER_GROUP = 8
Y_TILE = 512
OUT_CHUNK = 64
ADA_SLAB = 128
PERM_ROWS = 16
CAST_ROWS = 64
LANES = 128
SUBLANES = 8
VMEM_LIMIT = 56 * 1024 * 1024
F32 = jnp.float32
BF16 = jnp.bfloat16
NEG_INF = float("-inf")
LOG2E = 1.4426950408889634


def _cparams(sem):
    return pltpu.CompilerParams(dimension_semantics=sem, vmem_limit_bytes=VMEM_LIMIT)


def _ada_kernel(c_ref, w_ref, b_ref, o_ref, s_ref, *, rows):
    @pl.when(pl.program_id(0) == 0)
    def _():
        c = c_ref[...]
        s_ref[...] = c * jax.nn.sigmoid(c)

    d, tn = w_ref.shape

    def body(i, acc):
        r0 = pl.multiple_of(i * rows, rows)
        p = w_ref[pl.ds(r0, rows), :] * s_ref[pl.ds(r0, rows), :]
        return acc + jnp.sum(p.reshape(rows // SUBLANES, SUBLANES, tn), axis=0)

    acc = lax.fori_loop(0, d // rows, body, jnp.zeros((SUBLANES, tn), F32))
    o_ref[...] = jnp.sum(acc, axis=0, keepdims=True) + b_ref[...]


def _ada_slab(w_ref, s_ref, b_ref):
    d, tn = w_ref.shape
    rows = min(512, d)
    acc = jnp.zeros((SUBLANES, tn), F32)
    for r0 in range(0, d, rows):
        p = w_ref[r0:r0 + rows, :] * s_ref[r0:r0 + rows, :]
        acc = acc + jnp.sum(p.reshape(rows // SUBLANES, SUBLANES, tn), axis=0)
    return jnp.sum(acc, axis=0, keepdims=True) + b_ref[...]


def _ada(c, w_ada, b_ada, n):
    d = w_ada.shape[0]
    tn = min(512, n)
    rows = min(512, d)
    return pl.pallas_call(
        functools.partial(_ada_kernel, rows=rows),
        out_shape=jax.ShapeDtypeStruct((1, n), F32),
        grid=(n // tn,),
        in_specs=[pl.BlockSpec((d, 1), lambda j: (0, 0)),
                  pl.BlockSpec((d, tn), lambda j: (0, j)),
                  pl.BlockSpec((1, tn), lambda j: (0, j))],
        out_specs=pl.BlockSpec((1, tn), lambda j: (0, j)),
        scratch_shapes=[pltpu.VMEM((d, 1), F32)],
        compiler_params=_cparams(("arbitrary",)),
        name="ada",
    )(c.reshape(d, 1), w_ada, b_ada.reshape(1, -1))


def _rms_mod(x, g, sc, sh):
    ms = jnp.mean(x * x, axis=-1, keepdims=True)
    return (x * lax.rsqrt(ms + EPS) * g) * (1.0 + sc) + sh


def _row_permutation(ta, to_residue_major):
    n = RES * ta
    assert ta & (ta - 1) == 0
    sh = ta.bit_length() - 1
    r = lax.broadcasted_iota(jnp.int32, (n, n), 0)
    c = lax.broadcasted_iota(jnp.int32, (n, n), 1)
    res_major, natural = (r, c) if to_residue_major else (c, r)
    return (natural == RES * (res_major & (ta - 1)) + (res_major >> sh)).astype(BF16)


def _norm1_kernel(x_ref, g_ref, sc_ref, sh_ref, o_ref):
    ta = o_ref.shape[1]
    h = _rms_mod(x_ref[...], g_ref[...], sc_ref[...], sh_ref[...]).astype(BF16)
    hp = jnp.dot(_row_permutation(ta, True), h, preferred_element_type=F32)
    o_ref[...] = hp.astype(BF16).reshape(o_ref.shape)


def _norm1(x2d, g_mix, sc1, sh1):
    s, d = x2d.shape
    na = s // RES
    ta = min(32, na)
    vec = lambda: pl.BlockSpec((1, d), lambda i: (0, 0))
    return pl.pallas_call(
        _norm1_kernel,
        out_shape=jax.ShapeDtypeStruct((RES, na, d), BF16),
        grid=(na // ta,),
        in_specs=[pl.BlockSpec((RES * ta, d), lambda i: (i, 0)), vec(), vec(), vec()],
        out_specs=pl.BlockSpec((RES, ta, d), lambda i: (0, i, 0)),
        compiler_params=_cparams(("arbitrary",)),
        name="norm1",
    )(x2d, g_mix, sc1, sh1)


def _inproj_kernel(h_ref, w_ref, qg_ref, kg_ref, c_ref, wa_ref, ba_ref, z_ref, mod_ref, wb_ref, s_ref,
                   *, q0, k0, v0):
    j = pl.program_id(0)

    @pl.when((j == 0) & (pl.program_id(1) == 0))
    def _():
        c = c_ref[...]
        s_ref[...] = c * jax.nn.sigmoid(c)

    @pl.when(pl.program_id(1) == 0)
    def _():
        wb_ref[...] = w_ref[...].astype(BF16)

    mod_ref[...] = _ada_slab(wa_ref, s_ref, ba_ref)
    acc = jnp.dot(h_ref[...], wb_ref[...], preferred_element_type=F32)
    is_q = (j >= q0) & (j < k0)
    is_k = (j >= k0) & (j < v0)

    g = jnp.where(is_q, qg_ref[...] * (HEAD_DIM ** -0.5 * LOG2E), kg_ref[...])
    for hh in range(acc.shape[1] // HEAD_DIM):
        a = acc[:, hh * HEAD_DIM:(hh + 1) * HEAD_DIM]
        ms = jnp.mean(a * a, axis=-1, keepdims=True)
        scale = jnp.where(is_q | is_k, lax.rsqrt(ms + EPS) * g, 1.0)
        z_ref[:, hh * HEAD_DIM:(hh + 1) * HEAD_DIM] = a * scale


def _inproj_conv_kernel(h_ref, w_ref, z_ref, wb_ref):
    @pl.when(pl.program_id(1) == 0)
    def _():
        wb_ref[...] = w_ref[...].astype(BF16)

    z_ref[...] = jnp.dot(h_ref[...], wb_ref[...], preferred_element_type=F32).astype(BF16)


def _inproj_conv(h1, w_in, n_cols):
    s, d = h1.shape
    tm = min(1024, s)
    tn = math.gcd(512, n_cols)
    return pl.pallas_call(
        _inproj_conv_kernel,
        out_shape=jax.ShapeDtypeStruct((s, n_cols), BF16),
        grid=(n_cols // tn, s // tm),
        in_specs=[pl.BlockSpec((tm, d), lambda j, i: (i, 0)),
                  pl.BlockSpec((d, tn), lambda j, i: (0, j))],
        out_specs=pl.BlockSpec((tm, tn), lambda j, i: (i, j)),
        scratch_shapes=[pltpu.VMEM((d, tn), BF16)],
        compiler_params=_cparams(("arbitrary", "arbitrary")),
        name="inproj_conv",
    )(h1, w_in)


def _inproj(h1, w_in, q_g, k_g, col0, attn_w, c, w_ada, b_ada, mod_done):
    s, d = h1.shape
    n = w_in.shape[1] - col0
    tm = min(1024, s)
    tn = math.gcd(512, attn_w, col0)
    assert n == 3 * attn_w and tn % HEAD_DIM == 0
    j0 = col0 // tn
    q0 = 0
    k0 = q0 + attn_w // tn
    v0 = k0 + attn_w // tn
    ni = s // tm
    n_mod = w_ada.shape[1] - mod_done
    n_slab = n_mod // ADA_SLAB
    assert n_mod % ADA_SLAB == 0 and mod_done % ADA_SLAB == 0 and n_slab <= (n // tn) * ni
    slab = lambda j, i: jnp.minimum(j * ni + i, n_slab - 1)
    hvec = lambda: pl.BlockSpec((1, HEAD_DIM), lambda j, i: (0, 0))
    return pl.pallas_call(
        functools.partial(_inproj_kernel, q0=q0, k0=k0, v0=v0),
        out_shape=(jax.ShapeDtypeStruct((s, n), F32), jax.ShapeDtypeStruct((1, n_mod), F32)),
        grid=(n // tn, ni),
        in_specs=[pl.BlockSpec((tm, d), lambda j, i: (i, 0)),
                  pl.BlockSpec((d, tn), lambda j, i: (0, j0 + j)), hvec(), hvec(),
                  pl.BlockSpec((d, 1), lambda j, i: (0, 0)),
                  pl.BlockSpec((d, ADA_SLAB), lambda j, i: (0, mod_done // ADA_SLAB + slab(j, i))),
                  pl.BlockSpec((1, ADA_SLAB), lambda j, i: (0, mod_done // ADA_SLAB + slab(j, i)))],
        out_specs=(pl.BlockSpec((tm, tn), lambda j, i: (i, j)),
                   pl.BlockSpec((1, ADA_SLAB), lambda j, i: (0, slab(j, i)))),
        scratch_shapes=[pltpu.VMEM((d, tn), BF16), pltpu.VMEM((d, 1), F32)],
        compiler_params=_cparams(("arbitrary", "arbitrary")),
        name="inproj",
    )(h1, w_in, q_g, k_g, c.reshape(d, 1), w_ada, b_ada.reshape(1, -1))


def _conv_kernel(b_ref, c_ref, v_ref, cw_ref, g_ref, o_ref, ue_ref):
    i = pl.program_id(0)
    ta = b_ref.shape[1]
    w = cw_ref[...]
    assert w.shape[0] == 3
    for k in range(2):
        @pl.when(i == 0)
        def _():
            ue_ref[k, 0:SUBLANES, :] = jnp.zeros((SUBLANES, ue_ref.shape[2]), F32)

        @pl.when(i > 0)
        def _():
            ue_ref[k, 0:SUBLANES, :] = ue_ref[k, ta:ta + SUBLANES, :]

        r = RES - 2 + k
        ue_ref[k, SUBLANES:SUBLANES + ta, :] = c_ref[r].astype(F32) * v_ref[r].astype(F32)

    def u_at(bb):
        if bb >= 0:
            return c_ref[bb].astype(F32) * v_ref[bb].astype(F32)
        return ue_ref[2 + bb, SUBLANES - 1:SUBLANES - 1 + ta, :]

    for b in range(RES):
        y = b_ref[b].astype(F32) * (w[0:1, :] * u_at(b - 2) + w[1:2, :] * u_at(b - 1) + w[2:3, :] * u_at(b))
        ms = jnp.mean(y * y, axis=-1, keepdims=True)
        o_ref[b] = (y * lax.rsqrt(ms + EPS) * g_ref[...]).astype(BF16)


def _conv(z3, conv_w, g_conv, conv_ch):
    na = z3.shape[1]
    ta = min(64, na)
    blk = lambda jj: pl.BlockSpec((RES, ta, conv_ch), lambda i: (0, i, jj))
    return pl.pallas_call(
        _conv_kernel,
        out_shape=jax.ShapeDtypeStruct((RES, na, conv_ch), BF16),
        grid=(na // ta,),
        in_specs=[blk(0), blk(1), blk(2),
                  pl.BlockSpec(conv_w.shape, lambda i: (0, 0)),
                  pl.BlockSpec((1, conv_ch), lambda i: (0, 0))],
        out_specs=pl.BlockSpec((RES, ta, conv_ch), lambda i: (0, i, 0)),
        scratch_shapes=[pltpu.VMEM((2, ta + SUBLANES, conv_ch), F32)],
        compiler_params=_cparams(("arbitrary",)),
        name="conv",
    )(z3, z3, z3, conv_w, g_conv)


def _attn_kernel(slopes_ref, q_ref, kp_ref, kc_ref, vp_ref, vc_ref, w_ref, o_ref, wb_ref,
                 bias_ref, m_ref, l_ref, acc_ref):
    h = pl.program_id(0)
    cidx = pl.program_id(1)
    wb_ref[...] = w_ref[...].astype(BF16)
    slope = slopes_ref[h] * LOG2E
    first_pen = jnp.where(cidx > 0, 0.0, NEG_INF).astype(F32)
    dn_t = (((1,), (1,)), ((), ()))
    ones_kv = jnp.ones((2 * BAND, HEAD_DIM), BF16)

    j = lax.broadcasted_iota(jnp.int32, (BAND, 2 * BAND), 0)
    u = lax.broadcasted_iota(jnp.int32, (BAND, 2 * BAND), 1)
    is_prev = u < BAND
    uu = jnp.where(is_prev, u, u - BAND)
    for pi, (window, dil) in enumerate(PATTERNS):
        assert window // dil == BAND and RES % dil == 0
        ns, sr = RES // dil, BAND * dil // RES
        assert sr & (sr - 1) == 0
        sh = sr.bit_length() - 1
        mq = (j & (sr - 1)) * ns + (j >> sh)
        mk = (uu & (sr - 1)) * ns + (uu >> sh)
        steps = mq - mk + jnp.where(is_prev, BAND, 0)
        valid = (is_prev & (steps <= BAND)) | (jnp.logical_not(is_prev) & (steps >= 0))
        bias = jnp.where(valid, -slope * (steps * dil).astype(F32), NEG_INF)
        bias_ref[pi, 0] = bias
        bias_ref[pi, 1] = bias + jnp.where(is_prev, first_pen, 0.0)

    for pi, (window, dil) in enumerate(PATTERNS):
        ns, sr = RES // dil, BAND * dil // RES
        n_chunks = BAND // sr
        for r in range(dil):
            k_prev = v_prev = None
            for n in range(n_chunks):
                def gather(ref, nn):
                    return jnp.concatenate([ref[dil * c + r, nn * sr:(nn + 1) * sr, :] for c in range(ns)], axis=0)

                q = gather(q_ref, n).astype(BF16)
                if n == 0:
                    k_prev = gather(kp_ref, n_chunks - 1).astype(BF16)
                    v_prev = gather(vp_ref, n_chunks - 1).astype(BF16)
                k_cur, v_cur = gather(kc_ref, n).astype(BF16), gather(vc_ref, n).astype(BF16)
                kcat = jnp.concatenate([k_prev, k_cur], axis=0)
                vcat = jnp.concatenate([v_prev, v_cur], axis=0)
                k_prev, v_prev = k_cur, v_cur
                s = lax.dot_general(q, kcat, dn_t, preferred_element_type=F32) + bias_ref[pi, 1 if n == 0 else 0]
                m = jnp.max(s, axis=-1, keepdims=True)
                p = jnp.exp2(s - m).astype(BF16)
                ol = jnp.dot(p, jnp.concatenate([vcat, ones_kv], axis=1), preferred_element_type=F32)
                o, l_all = ol[:, :HEAD_DIM], ol[:, HEAD_DIM:]
                for c in range(ns):
                    rows = slice(c * sr, (c + 1) * sr)
                    at = (dil * c + r, slice(n * sr, (n + 1) * sr), slice(None))
                    m_b = jnp.broadcast_to(m[rows], (sr, HEAD_DIM))
                    l_b = l_all[rows]
                    if pi == 0:
                        m_ref[at] = m_b
                        l_ref[at] = l_b
                        acc_ref[at] = o[rows]
                    else:
                        m_old = m_ref[at]
                        m_new = jnp.maximum(m_old, m_b)
                        a_old = jnp.exp2(m_old - m_new)
                        a_new = jnp.exp2(m_b - m_new)
                        l_new = a_old * l_ref[at] + a_new * l_b
                        o_new = a_old * acc_ref[at] + a_new * o[rows]
                        if pi == len(PATTERNS) - 1:
                            acc_ref[at] = o_new / l_new
                        else:
                            m_ref[at] = m_new
                            l_ref[at] = l_new
                            acc_ref[at] = o_new
    o_ref[...] = acc_ref[...].astype(BF16)


def _attention(z3, slopes, attn_w, w_out):
    na = z3.shape[1]
    n_heads = attn_w // HEAD_DIM
    assert na % BAND == 0
    qb = 0
    kb = qb + n_heads
    vb = kb + n_heads
    blk = (RES, BAND, HEAD_DIM)
    nc = na // BAND
    wr, wc = w_out.shape
    ws = min(CAST_ROWS, wr)
    while wr // ws > n_heads * nc:
        ws *= 2
    n_ws = wr // ws
    assert wr % ws == 0
    cur = lambda base: pl.BlockSpec(blk, lambda h, c, sl: (0, c, base + h))
    prev = lambda base: pl.BlockSpec(blk, lambda h, c, sl: (0, jnp.maximum(c - 1, 0), base + h))
    wslab = lambda: pl.BlockSpec((ws, wc), lambda h, c, sl: (jnp.minimum(h * nc + c, n_ws - 1), 0))
    return pl.pallas_call(
        _attn_kernel,
        out_shape=(jax.ShapeDtypeStruct((RES, na, attn_w), BF16), jax.ShapeDtypeStruct((wr, wc), BF16)),
        grid_spec=pltpu.PrefetchScalarGridSpec(
            num_scalar_prefetch=1,
            grid=(n_heads, nc),
            in_specs=[cur(qb), prev(kb), cur(kb), prev(vb), cur(vb), wslab()],
            out_specs=(pl.BlockSpec(blk, lambda h, c, sl: (0, c, h)), wslab()),
            scratch_shapes=[pltpu.VMEM((len(PATTERNS), 2, BAND, 2 * BAND), F32),
                            pltpu.VMEM(blk, F32), pltpu.VMEM(blk, F32), pltpu.VMEM(blk, F32)]),
        compiler_params=_cparams(("arbitrary", "arbitrary")),
        name="attn",
    )(slopes, z3, z3, z3, z3, z3, w_out)


def _outproj_kernel(yc_ref, ya_ref, ga_ref, w_ref, x_ref, gt_ref, o_ref, y_ref):
    _, ta, cc = yc_ref.shape

    @pl.when(pl.program_id(1) == 0)
    def _():
        tp = min(PERM_ROWS, ta)
        n = RES * tp
        perm = _row_permutation(tp, False)
        for t0 in range(0, ta, tp):
            ya = ya_ref[:, t0:t0 + tp, :].reshape(n, ya_ref.shape[2]).astype(F32)
            ms = jnp.mean(ya * ya, axis=-1, keepdims=True)
            yn = (ya * lax.rsqrt(ms + EPS) * ga_ref[...]).astype(BF16)
            yc = yc_ref[:, t0:t0 + tp, :].reshape(n, cc)
            rows = slice(RES * t0, RES * t0 + n)
            y_ref[rows, 0:cc] = jnp.dot(perm, yc, preferred_element_type=F32).astype(BF16)
            y_ref[rows, cc:] = jnp.dot(perm, yn, preferred_element_type=F32).astype(BF16)

    acc = jnp.dot(y_ref[...], w_ref[...], preferred_element_type=F32)
    o_ref[...] = x_ref[...] + gt_ref[...] * acc


def _outproj(y_conv3, y_attn3, g_attn, w_out_bf, x2d, gt1):
    s, d = x2d.shape
    na = s // RES
    cc, aw = y_conv3.shape[2], y_attn3.shape[2]
    ta = min(32, na)
    tn = min(1024, d)
    return pl.pallas_call(
        _outproj_kernel,
        out_shape=jax.ShapeDtypeStruct((s, d), F32),
        grid=(na // ta, d // tn),
        in_specs=[pl.BlockSpec((RES, ta, cc), lambda i, j: (0, i, 0)),
                  pl.BlockSpec((RES, ta, aw), lambda i, j: (0, i, 0)),
                  pl.BlockSpec((1, aw), lambda i, j: (0, 0)),
                  pl.BlockSpec((cc + aw, tn), lambda i, j: (0, j)),
                  pl.BlockSpec((RES * ta, tn), lambda i, j: (i, j)),
                  pl.BlockSpec((1, tn), lambda i, j: (0, j))],
        out_specs=pl.BlockSpec((RES * ta, tn), lambda i, j: (i, j)),
        scratch_shapes=[pltpu.VMEM((RES * ta, cc + aw), BF16)],
        compiler_params=_cparams(("arbitrary", "arbitrary")),
        name="outproj",
    )(y_conv3, y_attn3, g_attn, w_out_bf, x2d, gt1)


def _router_kernel(x_ref, g_ref, sc_ref, sh_ref, wr_ref, h_ref, eid_ref, wt_ref):
    h = _rms_mod(x_ref[...], g_ref[...], sc_ref[...], sh_ref[...])
    h_ref[...] = h
    w = wr_ref[...]
    h_hi, w_hi = h.astype(BF16), w.astype(BF16)
    h_lo = (h - h_hi.astype(F32)).astype(BF16)
    w_lo = (w - w_hi.astype(F32)).astype(BF16)
    logits = (jnp.dot(h_hi, w_hi, preferred_element_type=F32) + jnp.dot(h_hi, w_lo, preferred_element_type=F32)
              + jnp.dot(h_lo, w_hi, preferred_element_type=F32))
    tm = logits.shape[0]
    lane = lax.broadcasted_iota(jnp.int32, (tm, LANES), 1).astype(F32)
    big = float(LANES)
    is_g = lane < N_GROUPS
    gl = jnp.where(is_g, logits, NEG_INF)
    gmax = jnp.max(gl, axis=-1, keepdims=True)
    gsel = jnp.min(jnp.where(gl == gmax, lane, big), axis=-1, keepdims=True)
    g_w = 1.0 / jnp.sum(jnp.where(is_g, jnp.exp(gl - gmax), 0.0), axis=-1, keepdims=True)
    lo = N_GROUPS + gsel * EXPERTS_PER_GROUP
    in_grp = (lane >= lo) & (lane < lo + EXPERTS_PER_GROUP)
    el = jnp.where(in_grp, logits, NEG_INF)
    e1 = jnp.max(el, axis=-1, keepdims=True)
    i1 = jnp.min(jnp.where(el == e1, lane, big), axis=-1, keepdims=True)
    el2 = jnp.where(lane == i1, NEG_INF, el)
    e2 = jnp.max(el2, axis=-1, keepdims=True)
    i2 = jnp.min(jnp.where(el2 == e2, lane, big), axis=-1, keepdims=True)
    t = jnp.exp(e2 - e1)
    w1 = g_w / (1.0 + t)
    w2 = g_w * t / (1.0 + t)
    eid = jnp.where(lane == 0.0, i1 - N_GROUPS, jnp.where(lane == 1.0, i2 - N_GROUPS, 0.0))
    eid_ref[...] = eid.astype(jnp.int32)
    wt_ref[...] = jnp.where(lane == 0.0, w1, jnp.where(lane == 1.0, w2, 0.0))


def _router(x2, g_ffn, sc2, sh2, w_r):
    s, d = x2.shape
    tm = min(512, s)
    vec = lambda: pl.BlockSpec((1, d), lambda i: (0, 0))
    lanes_out = lambda: pl.BlockSpec((tm, LANES), lambda i: (i, 0))
    return pl.pallas_call(
        _router_kernel,
        out_shape=(jax.ShapeDtypeStruct((s, d), F32),
                   jax.ShapeDtypeStruct((s, LANES), jnp.int32),
                   jax.ShapeDtypeStruct((s, LANES), F32)),
        grid=(s // tm,),
        in_specs=[pl.BlockSpec((tm, d), lambda i: (i, 0)), vec(), vec(), vec(),
                  pl.BlockSpec((d, LANES), lambda i: (0, 0))],
        out_specs=(pl.BlockSpec((tm, d), lambda i: (i, 0)), lanes_out(), lanes_out()),
        compiler_params=_cparams(("arbitrary",)),
        name="router",
    )(x2, g_ffn, sc2, sh2, w_r)


def _rank_kernel(eid_ref, rank_ref, cnt_ref, run_ref):
    @pl.when(pl.program_id(0) == 0)
    def _():
        run_ref[...] = jnp.zeros_like(run_ref)

    tb = eid_ref.shape[0]
    lane = lax.broadcasted_iota(jnp.int32, (tb, LANES), 1)
    row = lax.broadcasted_iota(jnp.int32, (tb, tb), 0)
    col = lax.broadcasted_iota(jnp.int32, (tb, tb), 1)
    earlier = (col < row).astype(BF16)
    e = eid_ref[...]
    run = run_ref[...]
    ranks = []
    for k in range(2):
        onehot = e[:, k:k + 1] == lane
        oh = onehot.astype(F32)
        before = jnp.dot(earlier, oh.astype(BF16), preferred_element_type=F32) + run
        ranks.append(jnp.sum(jnp.where(onehot, before, 0.0), axis=-1, keepdims=True))
        run = run + jnp.sum(oh, axis=0, keepdims=True)
    rank_ref[...] = jnp.where(lane == 0, ranks[0], jnp.where(lane == 1, ranks[1], 0.0)).astype(jnp.int32)
    run_ref[...] = run
    cnt_ref[...] = run


def _rank(eid):
    s = eid.shape[0]
    tb = min(512, s)
    return pl.pallas_call(
        _rank_kernel,
        out_shape=(jax.ShapeDtypeStruct((s, LANES), jnp.int32), jax.ShapeDtypeStruct((1, LANES), F32)),
        grid=(s // tb,),
        in_specs=[pl.BlockSpec((tb, LANES), lambda i: (i, 0))],
        out_specs=(pl.BlockSpec((tb, LANES), lambda i: (i, 0)), pl.BlockSpec((1, LANES), lambda i: (0, 0))),
        scratch_shapes=[pltpu.VMEM((1, LANES), F32)],
        compiler_params=_cparams(("arbitrary",)),
        name="rank",
    )(eid)


def _plan_kernel(eid_ref, rank_ref, cnt_ref, dest_ref, src_ref, blke_ref, blkc_ref, nused_ref, pst_ref,
                 *, blk, n_blk):
    def per_expert(e, nb):
        c = cnt_ref[e]
        pst_ref[e] = nb * blk
        nbe = (c + (blk - 1)) // blk

        def per_block(jb, carry):
            blke_ref[nb + jb] = e
            blkc_ref[nb + jb] = jnp.minimum(blk, c - jb * blk)
            return carry

        lax.fori_loop(0, nbe, per_block, 0)

        def pad_slot(r, carry):
            src_ref[nb * blk + r] = 0
            return carry

        lax.fori_loop(c, (c + (GATHER_GROUP - 1)) // GATHER_GROUP * GATHER_GROUP, pad_slot, 0)
        return nb + nbe

    n_used = lax.fori_loop(0, N_EXPERTS, per_expert, jnp.int32(0))
    nused_ref[0] = n_used
    last_e = blke_ref[n_used - 1]

    def unused(i, carry):
        blke_ref[i] = last_e
        blkc_ref[i] = 0
        return carry

    lax.fori_loop(n_used, n_blk, unused, 0)

    n_tok = eid_ref.shape[0] // 2

    for k in range(eid_ref.shape[0] // n_tok):
        def place(tok, carry):
            a = k * n_tok + tok
            dst = pst_ref[eid_ref[a]] + rank_ref[a]
            dest_ref[a] = dst
            src_ref[dst] = tok
            return carry

        lax.fori_loop(0, n_tok, place, 0, unroll=16)


def _plan(eid_flat, rank_flat, counts, blk):
    a = eid_flat.shape[0]
    n_blk = a // blk + N_EXPERTS
    smem = lambda: pl.BlockSpec(memory_space=pltpu.SMEM)
    i32 = jnp.int32
    return pl.pallas_call(
        functools.partial(_plan_kernel, blk=blk, n_blk=n_blk),
        out_shape=(jax.ShapeDtypeStruct((a,), i32), jax.ShapeDtypeStruct((n_blk * blk,), i32),
                   jax.ShapeDtypeStruct((n_blk,), i32), jax.ShapeDtypeStruct((n_blk,), i32),
                   jax.ShapeDtypeStruct((1,), i32)),
        in_specs=[smem(), smem(), smem()],
        out_specs=(smem(), smem(), smem(), smem(), smem()),
        scratch_shapes=[pltpu.SMEM((N_EXPERTS,), i32)],
        name="plan",
    )(eid_flat, rank_flat, counts)


def _gather_rows(idx_of_row, n_rows, src_hbm, dst, sem):
    def body(g, carry):
        for rr in range(GATHER_GROUP):
            r = g * GATHER_GROUP + rr
            pltpu.make_async_copy(src_hbm.at[pl.ds(idx_of_row(r), 1), :], dst.at[pl.ds(r, 1), :], sem).start()
        return carry
    lax.fori_loop(0, (n_rows + (GATHER_GROUP - 1)) // GATHER_GROUP, body, 0)


def _wait_rows(n_rows, src_hbm, dst, sem):
    def body(g, carry):
        pltpu.make_async_copy(src_hbm.at[pl.ds(0, GATHER_GROUP), :], dst.at[pl.ds(0, GATHER_GROUP), :], sem).wait()
        return carry
    lax.fori_loop(0, (n_rows + (GATHER_GROUP - 1)) // GATHER_GROUP, body, 0)


def _expert_kernel(blke_ref, blkc_ref, nused_ref, src_ref, h_hbm, w1_hbm, w3_hbm, w2_hbm, ys_hbm,
                   xbuf, obuf, w1b, w3b, w2b, gsem, wsem, osem, *, blk, kc):
    i = pl.program_id(0)
    last = pl.num_programs(0) - 1
    n_used = nused_ref[0]
    e = blke_ref[i]
    e_next = blke_ref[jnp.minimum(i + 1, last)]
    first_of_expert = (i == 0) | (blke_ref[jnp.maximum(i - 1, 0)] != e)
    fetch_next = (i + 1 < n_used) & (e_next != e)
    wbufs = ((w1_hbm, w1b), (w3_hbm, w3b), (w2_hbm, w2b))

    def weight_copy(k, ee):
        return pltpu.make_async_copy(wbufs[k][0].at[ee], wbufs[k][1], wsem.at[k])

    def issue_rows(b, slot):
        _gather_rows(lambda r: src_ref[b * blk + r], blkc_ref[b], h_hbm, xbuf.at[slot], gsem.at[slot])

    def weights_ready(k):
        @pl.when(first_of_expert)
        def _():
            weight_copy(k, e).wait()

    def weights_done(k):
        @pl.when(fetch_next)
        def _():
            weight_copy(k, e_next).start()

    def out_copy(b, slot, ch):
        return pltpu.make_async_copy(obuf.at[slot, pl.ds(ch * OUT_CHUNK, OUT_CHUNK), :],
                                     ys_hbm.at[pl.ds(b * blk + ch * OUT_CHUNK, OUT_CHUNK), :], osem.at[slot])

    def for_out_chunks(b, fn):
        def body(ch, carry):
            fn(ch)
            return carry
        lax.fori_loop(0, (blkc_ref[b] + (OUT_CHUNK - 1)) // OUT_CHUNK, body, 0)

    slot = lax.rem(i, 2)

    @pl.when(i == 0)
    def _():
        for k in range(3):
            weight_copy(k, e).start()
        xbuf[...] = jnp.zeros_like(xbuf)
        issue_rows(0, 0)

    @pl.when(i < n_used)
    def _():
        _wait_rows(blkc_ref[i], h_hbm, xbuf.at[slot], gsem.at[slot])

        @pl.when(i + 1 < n_used)
        def _():
            issue_rows(i + 1, 1 - slot)

        d = xbuf.shape[2]

        def up_proj(wb):
            acc = None
            for k0 in range(0, d, kc):
                part = jnp.dot(xbuf[slot, :, k0:k0 + kc].astype(BF16), wb[k0:k0 + kc, :].astype(BF16),
                               preferred_element_type=F32)
                acc = part if acc is None else acc + part
            return acc

        weights_ready(0)
        a = up_proj(w1b)
        weights_done(0)
        weights_ready(1)
        b = up_proj(w3b)
        weights_done(1)
        hid = ((a * jax.nn.sigmoid(a)) * b).astype(BF16)
        weights_ready(2)

        @pl.when(i >= 2)
        def _():
            for_out_chunks(i - 2, lambda ch: out_copy(i - 2, slot, 0).wait())

        for t in range(0, d, Y_TILE):
            obuf[slot, :, t:t + Y_TILE] = jnp.dot(hid, w2b[:, t:t + Y_TILE].astype(BF16),
                                                  preferred_element_type=F32)
        weights_done(2)
        for_out_chunks(i, lambda ch: out_copy(i, slot, ch).start())

        @pl.when(i == n_used - 1)
        def _():
            @pl.when(i >= 1)
            def _():
                for_out_chunks(i - 1, lambda ch: out_copy(i - 1, 1 - slot, 0).wait())
            for_out_chunks(i, lambda ch: out_copy(i, slot, 0).wait())


def _experts(h2, w1, w3, w2, blk_e, blk_cnt, n_used, row_src, blk):
    s, d = h2.shape
    de = w1.shape[-1]
    kc = min(1024, d)
    assert d % kc == 0 and d % Y_TILE == 0 and blk % GATHER_GROUP == 0 and blk % OUT_CHUNK == 0
    n_blk = blk_e.shape[0]
    anyspace = lambda: pl.BlockSpec(memory_space=pl.ANY)
    return pl.pallas_call(
        functools.partial(_expert_kernel, blk=blk, kc=kc),
        out_shape=jax.ShapeDtypeStruct((n_blk * blk, d), F32),
        grid_spec=pltpu.PrefetchScalarGridSpec(
            num_scalar_prefetch=4,
            grid=(n_blk,),
            in_specs=[anyspace(), anyspace(), anyspace(), anyspace()],
            out_specs=anyspace(),
            scratch_shapes=[pltpu.VMEM((2, blk, d), F32), pltpu.VMEM((2, blk, d), F32),
                            pltpu.VMEM((d, de), F32), pltpu.VMEM((d, de), F32), pltpu.VMEM((de, d), F32),
                            pltpu.SemaphoreType.DMA((2,)), pltpu.SemaphoreType.DMA((3,)),
                            pltpu.SemaphoreType.DMA((2,))]),
        compiler_params=_cparams(("arbitrary",)),
        name="experts",
    )(blk_e, blk_cnt, n_used, row_src, h2, w1, w3, w2)


def _combine_kernel(dest_ref, ys_hbm, x_ref, wt_ref, gt_ref, o_ref, ybuf, sem):
    i = pl.program_id(0)
    nb = pl.num_programs(0)
    tm = x_ref.shape[0]
    n_tok = tm * nb

    def issue(blk_i, slot):
        for k in range(2):
            _gather_rows(lambda r: dest_ref[k * n_tok + blk_i * tm + r], tm, ys_hbm, ybuf.at[slot, k],
                         sem.at[slot])

    slot = lax.rem(i, 2)

    @pl.when(i == 0)
    def _():
        issue(0, 0)

    @pl.when(i + 1 < nb)
    def _():
        issue(i + 1, 1 - slot)

    for k in range(2):
        _wait_rows(tm, ys_hbm, ybuf.at[slot, k], sem.at[slot])
    w = wt_ref[...]
    y = ybuf[slot, 0] * w[:, 0:1] + ybuf[slot, 1] * w[:, 1:2]
    o_ref[...] = x_ref[...] + gt_ref[...] * y


def _combine(dest, ys, x2, wts, gt2):
    s, d = x2.shape
    tm = min(256, s)
    return pl.pallas_call(
        _combine_kernel,
        out_shape=jax.ShapeDtypeStruct((s, d), F32),
        grid_spec=pltpu.PrefetchScalarGridSpec(
            num_scalar_prefetch=1,
            grid=(s // tm,),
            in_specs=[pl.BlockSpec(memory_space=pl.ANY),
                      pl.BlockSpec((tm, d), lambda i, ds_: (i, 0)),
                      pl.BlockSpec((tm, LANES), lambda i, ds_: (i, 0)),
                      pl.BlockSpec((1, d), lambda i, ds_: (0, 0))],
            out_specs=pl.BlockSpec((tm, d), lambda i, ds_: (i, 0)),
            scratch_shapes=[pltpu.VMEM((2, 2, tm, d), F32), pltpu.SemaphoreType.DMA((2,))]),
        compiler_params=_cparams(("arbitrary",)),
        name="combine",
    )(dest, ys, x2, wts, gt2)


def _layer(x2d, c, w_ada, b_ada, g_mix, g_ffn, w_in, conv_w, q_norm_g, k_norm_g, g_branch, w_out,
           w_group, w_expert, w1, w3, w2):
    s, d = x2d.shape
    conv_ch = conv_w.shape[1]
    attn_w = d - conv_ch
    n_heads = attn_w // HEAD_DIM
    na = s // RES
    assert s % (RES * BAND) == 0
    row = lambda v: v.reshape(1, -1)

    mod_a = _ada(c, w_ada, b_ada, 2 * d)
    sh1, sc1 = mod_a[:, :d], mod_a[:, d:]

    h1 = _norm1(x2d, row(g_mix), sc1, sh1).reshape(s, d)
    z_conv = _inproj_conv(h1, w_in, 3 * conv_ch)
    z_attn, mod_b = _inproj(h1, w_in, row(q_norm_g), row(k_norm_g), 3 * conv_ch, attn_w, c, w_ada, b_ada, 2 * d)
    gt1, sh2, sc2, gt2 = [mod_b[:, i * d:(i + 1) * d] for i in range(4)]
    y_conv = _conv(z_conv.reshape(RES, na, 3 * conv_ch), conv_w, row(g_branch[:conv_ch]), conv_ch)
    slopes = jnp.exp2(-ALIBI_MAX_EXP * jnp.arange(1, n_heads + 1, dtype=F32) / n_heads)
    y_attn, w_out_bf = _attention(z_attn.reshape(RES, na, 3 * attn_w), slopes, attn_w, w_out)
    x2 = _outproj(y_conv, y_attn, row(g_branch[conv_ch:]), w_out_bf, x2d, gt1)

    w_r = jnp.concatenate([w_group, w_expert,
                           jnp.zeros((d, LANES - N_GROUPS - N_EXPERTS), F32)], axis=1)
    h2, eid, wts = _router(x2, row(g_ffn), sc2, sh2, w_r)
    rank, counts = _rank(eid)
    eid_flat = eid[:, :2].T.reshape(-1)
    rank_flat = rank[:, :2].T.reshape(-1)
    cnt = counts[0, :N_EXPERTS].astype(jnp.int32)
    dest, row_src, blk_e, blk_cnt, n_used = _plan(eid_flat, rank_flat, cnt, MOE_BLOCK)
    ys = _experts(h2, w1, w3, w2, blk_e, blk_cnt, n_used, row_src, MOE_BLOCK)
    return _combine(dest, ys, x2, wts, gt2)


def kernel(x, c, w_ada, b_ada, g_mix, g_ffn, w_in, conv_w, q_norm_g, k_norm_g, g_branch, w_out,
           w_group, w_expert, w1, w3, w2):
    b, s, d = x.shape
    assert b == 1 and c.shape[0] == 1
    for l in range(w_ada.shape[0]):
        x = _layer(x.reshape(s, d), c[0], w_ada[l], b_ada[l], g_mix[l], g_ffn[l], w_in[l], conv_w[l],
                   q_norm_g[l], k_norm_g[l], g_branch[l], w_out[l], w_group[l], w_expert[l],
                   w1[l], w3[l], w2[l]).reshape(b, s, d)
    return x
```

```python
import functools
import math

import jax
import jax.numpy as jnp
from jax import lax
from jax.experimental import pallas as pl
from jax.experimental.pallas import tpu as pltpu

EPS = 1e-6
HEAD_DIM = 128
BAND = 128
PATTERNS = ((128, 1), (512, 4), (2048, 16))
RES = 16
ALIBI_MAX_EXP = 8.0
N_GROUPS = 8
EXPERTS_PER_GROUP = 8
N_EXPERTS = N_GROUPS * EXPERTS_PER_GROUP
MOE_BLOCK = 256
GATHER_GROUP = 8
Y_TILE = 512
OUT_CHUNK = 64
ADA_SLAB = 128
PERM_ROWS = 16
CAST_ROWS = 128
LANES = 128
SUBLANES = 8
VMEM_LIMIT = 56 * 1024 * 1024
F32 = jnp.float32
BF16 = jnp.bfloat16
NEG_INF = float("-inf")
LOG2E = 1.4426950408889634


def _cparams(sem):
    return pltpu.CompilerParams(dimension_semantics=sem, vmem_limit_bytes=VMEM_LIMIT)


def _ada_kernel(c_ref, w_ref, b_ref, o_ref, s_ref, *, rows):
    @pl.when(pl.program_id(0) == 0)
    def _():
        c = c_ref[...]
        s_ref[...] = c * jax.nn.sigmoid(c)

    d, tn = w_ref.shape

    def body(i, acc):
        r0 = pl.multiple_of(i * rows, rows)
        p = w_ref[pl.ds(r0, rows), :] * s_ref[pl.ds(r0, rows), :]
        return acc + jnp.sum(p.reshape(rows // SUBLANES, SUBLANES, tn), axis=0)

    acc = lax.fori_loop(0, d // rows, body, jnp.zeros((SUBLANES, tn), F32))
    o_ref[...] = jnp.sum(acc, axis=0, keepdims=True) + b_ref[...]


def _ada_slab(w_ref, s_ref, b_ref):
    d, tn = w_ref.shape
    rows = min(512, d)
    acc = jnp.zeros((SUBLANES, tn), F32)
    for r0 in range(0, d, rows):
        p = w_ref[r0:r0 + rows, :] * s_ref[r0:r0 + rows, :]
        acc = acc + jnp.sum(p.reshape(rows // SUBLANES, SUBLANES, tn), axis=0)
    return jnp.sum(acc, axis=0, keepdims=True) + b_ref[...]


def _ada(c, w_ada, b_ada, n):
    d = w_ada.shape[0]
    tn = min(512, n)
    rows = min(512, d)
    return pl.pallas_call(
        functools.partial(_ada_kernel, rows=rows),
        out_shape=jax.ShapeDtypeStruct((1, n), F32),
        grid=(n // tn,),
        in_specs=[pl.BlockSpec((d, 1), lambda j: (0, 0)),
                  pl.BlockSpec((d, tn), lambda j: (0, j)),
                  pl.BlockSpec((1, tn), lambda j: (0, j))],
        out_specs=pl.BlockSpec((1, tn), lambda j: (0, j)),
        scratch_shapes=[pltpu.VMEM((d, 1), F32)],
        compiler_params=_cparams(("arbitrary",)),
        name="ada",
    )(c.reshape(d, 1), w_ada, b_ada.reshape(1, -1))


def _rms_mod(x, g, sc, sh):
    ms = jnp.mean(x * x, axis=-1, keepdims=True)
    return (x * lax.rsqrt(ms + EPS) * g) * (1.0 + sc) + sh


def _row_permutation(ta, to_residue_major):
    n = RES * ta
    assert ta & (ta - 1) == 0
    sh = ta.bit_length() - 1
    r = lax.broadcasted_iota(jnp.int32, (n, n), 0)
    c = lax.broadcasted_iota(jnp.int32, (n, n), 1)
    res_major, natural = (r, c) if to_residue_major else (c, r)
    return (natural == RES * (res_major & (ta - 1)) + (res_major >> sh)).astype(BF16)


def _norm1_kernel(x_ref, g_ref, sc_ref, sh_ref, o_ref):
    ta = o_ref.shape[1]
    h = _rms_mod(x_ref[...], g_ref[...], sc_ref[...], sh_ref[...]).astype(BF16)
    hp = jnp.dot(_row_permutation(ta, True), h, preferred_element_type=F32)
    o_ref[...] = hp.astype(BF16).reshape(o_ref.shape)


def _norm1(x2d, g_mix, sc1, sh1):
    s, d = x2d.shape
    na = s // RES
    ta = min(32, na)
    vec = lambda: pl.BlockSpec((1, d), lambda i: (0, 0))
    return pl.pallas_call(
        _norm1_kernel,
        out_shape=jax.ShapeDtypeStruct((RES, na, d), BF16),
        grid=(na // ta,),
        in_specs=[pl.BlockSpec((RES * ta, d), lambda i: (i, 0)), vec(), vec(), vec()],
        out_specs=pl.BlockSpec((RES, ta, d), lambda i: (0, i, 0)),
        compiler_params=_cparams(("arbitrary",)),
        name="norm1",
    )(x2d, g_mix, sc1, sh1)


def _inproj_kernel(h_ref, w_ref, qg_ref, kg_ref, c_ref, wa_ref, ba_ref, z_ref, mod_ref, wb_ref, s_ref,
                   *, q0, k0, v0):
    j = pl.program_id(0)

    @pl.when((j == 0) & (pl.program_id(1) == 0))
    def _():
        c = c_ref[...]
        s_ref[...] = c * jax.nn.sigmoid(c)

    @pl.when(pl.program_id(1) == 0)
    def _():
        wb_ref[...] = w_ref[...].astype(BF16)

    mod_ref[...] = _ada_slab(wa_ref, s_ref, ba_ref)
    acc = jnp.dot(h_ref[...], wb_ref[...], preferred_element_type=F32)
    is_q = (j >= q0) & (j < k0)
    is_k = (j >= k0) & (j < v0)

    g = jnp.where(is_q, qg_ref[...] * (HEAD_DIM ** -0.5 * LOG2E), kg_ref[...])
    for hh in range(acc.shape[1] // HEAD_DIM):
        a = acc[:, hh * HEAD_DIM:(hh + 1) * HEAD_DIM]
        ms = jnp.mean(a * a, axis=-1, keepdims=True)
        scale = jnp.where(is_q | is_k, lax.rsqrt(ms + EPS) * g, 1.0)
        z_ref[:, hh * HEAD_DIM:(hh + 1) * HEAD_DIM] = a * scale


def _inproj_conv_kernel(h_ref, w_ref, wo_ref, z_ref, wob_ref, wb_ref):
    @pl.when(pl.program_id(1) == 0)
    def _():
        wb_ref[...] = w_ref[...].astype(BF16)

    wob_ref[...] = wo_ref[...].astype(BF16)
    z_ref[...] = jnp.dot(h_ref[...], wb_ref[...], preferred_element_type=F32).astype(BF16)


def _inproj_conv(h1, w_in, n_cols, w_out):
    s, d = h1.shape
    tm = min(1024, s)
    tn = math.gcd(512, n_cols)
    ni = s // tm
    wr, wc = w_out.shape
    ws = min(CAST_ROWS, wr)
    while wr // ws > (n_cols // tn) * ni:
        ws *= 2
    n_ws = wr // ws
    assert wr % ws == 0
    wslab = lambda: pl.BlockSpec((ws, wc), lambda j, i: (jnp.minimum(j * ni + i, n_ws - 1), 0))
    return pl.pallas_call(
        _inproj_conv_kernel,
        out_shape=(jax.ShapeDtypeStruct((s, n_cols), BF16), jax.ShapeDtypeStruct((wr, wc), BF16)),
        grid=(n_cols // tn, ni),
        in_specs=[pl.BlockSpec((tm, d), lambda j, i: (i, 0)),
                  pl.BlockSpec((d, tn), lambda j, i: (0, j)), wslab()],
        out_specs=(pl.BlockSpec((tm, tn), lambda j, i: (i, j)), wslab()),
        scratch_shapes=[pltpu.VMEM((d, tn), BF16)],
        compiler_params=_cparams(("arbitrary", "arbitrary")),
        name="inproj_conv",
    )(h1, w_in, w_out)


def _inproj(h1, w_in, q_g, k_g, col0, attn_w, c, w_ada, b_ada, mod_done):
    s, d = h1.shape
    n = w_in.shape[1] - col0
    tm = min(1024, s)
    tn = math.gcd(512, attn_w, col0)
    assert n == 3 * attn_w and tn % HEAD_DIM == 0
    j0 = col0 // tn
    q0 = 0
    k0 = q0 + attn_w // tn
    v0 = k0 + attn_w // tn
    ni = s // tm
    n_mod = w_ada.shape[1] - mod_done
    n_slab = n_mod // ADA_SLAB
    assert n_mod % ADA_SLAB == 0 and mod_done % ADA_SLAB == 0 and n_slab <= (n // tn) * ni
    slab = lambda j, i: jnp.minimum(j * ni + i, n_slab - 1)
    hvec = lambda: pl.BlockSpec((1, HEAD_DIM), lambda j, i: (0, 0))
    return pl.pallas_call(
        functools.partial(_inproj_kernel, q0=q0, k0=k0, v0=v0),
        out_shape=(jax.ShapeDtypeStruct((s, n), F32), jax.ShapeDtypeStruct((1, n_mod), F32)),
        grid=(n // tn, ni),
        in_specs=[pl.BlockSpec((tm, d), lambda j, i: (i, 0)),
                  pl.BlockSpec((d, tn), lambda j, i: (0, j0 + j)), hvec(), hvec(),
                  pl.BlockSpec((d, 1), lambda j, i: (0, 0)),
                  pl.BlockSpec((d, ADA_SLAB), lambda j, i: (0, mod_done // ADA_SLAB + slab(j, i))),
                  pl.BlockSpec((1, ADA_SLAB), lambda j, i: (0, mod_done // ADA_SLAB + slab(j, i)))],
        out_specs=(pl.BlockSpec((tm, tn), lambda j, i: (i, j)),
                   pl.BlockSpec((1, ADA_SLAB), lambda j, i: (0, slab(j, i)))),
        scratch_shapes=[pltpu.VMEM((d, tn), BF16), pltpu.VMEM((d, 1), F32)],
        compiler_params=_cparams(("arbitrary", "arbitrary")),
        name="inproj",
    )(h1, w_in, q_g, k_g, c.reshape(d, 1), w_ada, b_ada.reshape(1, -1))


def _conv_kernel(b_ref, c_ref, v_ref, cw_ref, g_ref, o_ref, ue_ref):
    i = pl.program_id(0)
    ta = b_ref.shape[1]
    w = cw_ref[...]
    assert w.shape[0] == 3
    for k in range(2):
        @pl.when(i == 0)
        def _():
            ue_ref[k, 0:SUBLANES, :] = jnp.zeros((SUBLANES, ue_ref.shape[2]), F32)

        @pl.when(i > 0)
        def _():
            ue_ref[k, 0:SUBLANES, :] = ue_ref[k, ta:ta + SUBLANES, :]

        r = RES - 2 + k
        ue_ref[k, SUBLANES:SUBLANES + ta, :] = c_ref[r].astype(F32) * v_ref[r].astype(F32)

    def u_at(bb):
        if bb >= 0:
            return c_ref[bb].astype(F32) * v_ref[bb].astype(F32)
        return ue_ref[2 + bb, SUBLANES - 1:SUBLANES - 1 + ta, :]

    for b in range(RES):
        y = b_ref[b].astype(F32) * (w[0:1, :] * u_at(b - 2) + w[1:2, :] * u_at(b - 1) + w[2:3, :] * u_at(b))
        ms = jnp.mean(y * y, axis=-1, keepdims=True)
        o_ref[b] = (y * lax.rsqrt(ms + EPS) * g_ref[...]).astype(BF16)


def _conv(z3, conv_w, g_conv, conv_ch):
    na = z3.shape[1]
    ta = min(64, na)
    blk = lambda jj: pl.BlockSpec((RES, ta, conv_ch), lambda i: (0, i, jj))
    return pl.pallas_call(
        _conv_kernel,
        out_shape=jax.ShapeDtypeStruct((RES, na, conv_ch), BF16),
        grid=(na // ta,),
        in_specs=[blk(0), blk(1), blk(2),
                  pl.BlockSpec(conv_w.shape, lambda i: (0, 0)),
                  pl.BlockSpec((1, conv_ch), lambda i: (0, 0))],
        out_specs=pl.BlockSpec((RES, ta, conv_ch), lambda i: (0, i, 0)),
        scratch_shapes=[pltpu.VMEM((2, ta + SUBLANES, conv_ch), F32)],
        compiler_params=_cparams(("arbitrary",)),
        name="conv",
    )(z3, z3, z3, conv_w, g_conv)


def _attn_kernel(slopes_ref, q_ref, kp_ref, kc_ref, vp_ref, vc_ref, o_ref, bias_ref, m_ref, l_ref, acc_ref):
    h = pl.program_id(0)
    cidx = pl.program_id(1)
    slope = slopes_ref[h] * LOG2E
    first_pen = jnp.where(cidx > 0, 0.0, NEG_INF).astype(F32)
    dn_t = (((1,), (1,)), ((), ()))
    ones_kv = jnp.ones((2 * BAND, HEAD_DIM), BF16)

    j = lax.broadcasted_iota(jnp.int32, (BAND, 2 * BAND), 0)
    u = lax.broadcasted_iota(jnp.int32, (BAND, 2 * BAND), 1)
    is_prev = u < BAND
    uu = jnp.where(is_prev, u, u - BAND)
    for pi, (window, dil) in enumerate(PATTERNS):
        assert window // dil == BAND and RES % dil == 0
        ns, sr = RES // dil, BAND * dil // RES
        assert sr & (sr - 1) == 0
        sh = sr.bit_length() - 1
        mq = (j & (sr - 1)) * ns + (j >> sh)
        mk = (uu & (sr - 1)) * ns + (uu >> sh)
        steps = mq - mk + jnp.where(is_prev, BAND, 0)
        valid = (is_prev & (steps <= BAND)) | (jnp.logical_not(is_prev) & (steps >= 0))
        bias = jnp.where(valid, -slope * (steps * dil).astype(F32), NEG_INF)
        bias_ref[pi, 0] = bias
        bias_ref[pi, 1] = bias + jnp.where(is_prev, first_pen, 0.0)

    for pi, (window, dil) in enumerate(PATTERNS):
        ns, sr = RES // dil, BAND * dil // RES
        n_chunks = BAND // sr
        for r in range(dil):
            k_prev = v_prev = None
            for n in range(n_chunks):
                def gather(ref, nn):
                    return jnp.concatenate([ref[dil * c + r, nn * sr:(nn + 1) * sr, :] for c in range(ns)], axis=0)

                q = gather(q_ref, n).astype(BF16)
                if n == 0:
                    k_prev = gather(kp_ref, n_chunks - 1).astype(BF16)
                    v_prev = gather(vp_ref, n_chunks - 1).astype(BF16)
                k_cur, v_cur = gather(kc_ref, n).astype(BF16), gather(vc_ref, n).astype(BF16)
                kcat = jnp.concatenate([k_prev, k_cur], axis=0)
                vcat = jnp.concatenate([v_prev, v_cur], axis=0)
                k_prev, v_prev = k_cur, v_cur
                s = lax.dot_general(q, kcat, dn_t, preferred_element_type=F32) + bias_ref[pi, 1 if n == 0 else 0]
                m = jnp.max(s, axis=-1, keepdims=True)
                p = jnp.exp2(s - m).astype(BF16)
                ol = jnp.dot(p, jnp.concatenate([vcat, ones_kv], axis=1), preferred_element_type=F32)
                o, l_all = ol[:, :HEAD_DIM], ol[:, HEAD_DIM:]
                for c in range(ns):
                    rows = slice(c * sr, (c + 1) * sr)
                    at = (dil * c + r, slice(n * sr, (n + 1) * sr), slice(None))
                    m_b = jnp.broadcast_to(m[rows], (sr, HEAD_DIM))
                    l_b = l_all[rows]
                    if pi == 0:
                        m_ref[at] = m_b
                        l_ref[at] = l_b
                        acc_ref[at] = o[rows]
                    else:
                        m_old = m_ref[at]
                        m_new = jnp.maximum(m_old, m_b)
                        a_old = jnp.exp2(m_old - m_new)
                        a_new = jnp.exp2(m_b - m_new)
                        l_new = a_old * l_ref[at] + a_new * l_b
                        o_new = a_old * acc_ref[at] + a_new * o[rows]
                        if pi == len(PATTERNS) - 1:
                            acc_ref[at] = o_new / l_new
                        else:
                            m_ref[at] = m_new
                            l_ref[at] = l_new
                            acc_ref[at] = o_new
    o_ref[...] = acc_ref[...].astype(BF16)


def _attention(z3, slopes, attn_w):
    na = z3.shape[1]
    n_heads = attn_w // HEAD_DIM
    assert na % BAND == 0
    qb = 0
    kb = qb + n_heads
    vb = kb + n_heads
    blk = (RES, BAND, HEAD_DIM)
    cur = lambda base: pl.BlockSpec(blk, lambda h, c, sl: (0, c, base + h))
    prev = lambda base: pl.BlockSpec(blk, lambda h, c, sl: (0, jnp.maximum(c - 1, 0), base + h))
    return pl.pallas_call(
        _attn_kernel,
        out_shape=jax.ShapeDtypeStruct((RES, na, attn_w), BF16),
        grid_spec=pltpu.PrefetchScalarGridSpec(
            num_scalar_prefetch=1,
            grid=(n_heads, na // BAND),
            in_specs=[cur(qb), prev(kb), cur(kb), prev(vb), cur(vb)],
            out_specs=pl.BlockSpec(blk, lambda h, c, sl: (0, c, h)),
            scratch_shapes=[pltpu.VMEM((len(PATTERNS), 2, BAND, 2 * BAND), F32),
                            pltpu.VMEM(blk, F32), pltpu.VMEM(blk, F32), pltpu.VMEM(blk, F32)]),
        compiler_params=_cparams(("arbitrary", "arbitrary")),
        name="attn",
    )(slopes, z3, z3, z3, z3, z3)


def _outproj_kernel(yc_ref, ya_ref, ga_ref, w_ref, x_ref, gt_ref, o_ref, y_ref):
    _, ta, cc = yc_ref.shape

    @pl.when(pl.program_id(1) == 0)
    def _():
        tp = min(PERM_ROWS, ta)
        n = RES * tp
        perm = _row_permutation(tp, False)
        for t0 in range(0, ta, tp):
            ya = ya_ref[:, t0:t0 + tp, :].reshape(n, ya_ref.shape[2]).astype(F32)
            ms = jnp.mean(ya * ya, axis=-1, keepdims=True)
            yn = (ya * lax.rsqrt(ms + EPS) * ga_ref[...]).astype(BF16)
            yc = yc_ref[:, t0:t0 + tp, :].reshape(n, cc)
            rows = slice(RES * t0, RES * t0 + n)
            y_ref[rows, 0:cc] = jnp.dot(perm, yc, preferred_element_type=F32).astype(BF16)
            y_ref[rows, cc:] = jnp.dot(perm, yn, preferred_element_type=F32).astype(BF16)

    acc = jnp.dot(y_ref[...], w_ref[...], preferred_element_type=F32)
    o_ref[...] = x_ref[...] + gt_ref[...] * acc


def _outproj(y_conv3, y_attn3, g_attn, w_out_bf, x2d, gt1):
    s, d = x2d.shape
    na = s // RES
    cc, aw = y_conv3.shape[2], y_attn3.shape[2]
    ta = min(32, na)
    tn = min(1024, d)
    return pl.pallas_call(
        _outproj_kernel,
        out_shape=jax.ShapeDtypeStruct((s, d), F32),
        grid=(na // ta, d // tn),
        in_specs=[pl.BlockSpec((RES, ta, cc), lambda i, j: (0, i, 0)),
                  pl.BlockSpec((RES, ta, aw), lambda i, j: (0, i, 0)),
                  pl.BlockSpec((1, aw), lambda i, j: (0, 0)),
                  pl.BlockSpec((cc + aw, tn), lambda i, j: (0, j)),
                  pl.BlockSpec((RES * ta, tn), lambda i, j: (i, j)),
                  pl.BlockSpec((1, tn), lambda i, j: (0, j))],
        out_specs=pl.BlockSpec((RES * ta, tn), lambda i, j: (i, j)),
        scratch_shapes=[pltpu.VMEM((RES * ta, cc + aw), BF16)],
        compiler_params=_cparams(("arbitrary", "arbitrary")),
        name="outproj",
    )(y_conv3, y_attn3, g_attn, w_out_bf, x2d, gt1)


def _router_kernel(x_ref, g_ref, sc_ref, sh_ref, wr_ref, h_ref, eid_ref, wt_ref):
    h = _rms_mod(x_ref[...], g_ref[...], sc_ref[...], sh_ref[...])
    h_ref[...] = h
    w = wr_ref[...]
    h_hi, w_hi = h.astype(BF16), w.astype(BF16)
    h_lo = (h - h_hi.astype(F32)).astype(BF16)
    w_lo = (w - w_hi.astype(F32)).astype(BF16)
    logits = (jnp.dot(h_hi, w_hi, preferred_element_type=F32) + jnp.dot(h_hi, w_lo, preferred_element_type=F32)
              + jnp.dot(h_lo, w_hi, preferred_element_type=F32))
    tm = logits.shape[0]
    lane = lax.broadcasted_iota(jnp.int32, (tm, LANES), 1).astype(F32)
    big = float(LANES)
    is_g = lane < N_GROUPS
    gl = jnp.where(is_g, logits, NEG_INF)
    gmax = jnp.max(gl, axis=-1, keepdims=True)
    gsel = jnp.min(jnp.where(gl == gmax, lane, big), axis=-1, keepdims=True)
    g_w = 1.0 / jnp.sum(jnp.where(is_g, jnp.exp(gl - gmax), 0.0), axis=-1, keepdims=True)
    lo = N_GROUPS + gsel * EXPERTS_PER_GROUP
    in_grp = (lane >= lo) & (lane < lo + EXPERTS_PER_GROUP)
    el = jnp.where(in_grp, logits, NEG_INF)
    e1 = jnp.max(el, axis=-1, keepdims=True)
    i1 = jnp.min(jnp.where(el == e1, lane, big), axis=-1, keepdims=True)
    el2 = jnp.where(lane == i1, NEG_INF, el)
    e2 = jnp.max(el2, axis=-1, keepdims=True)
    i2 = jnp.min(jnp.where(el2 == e2, lane, big), axis=-1, keepdims=True)
    t = jnp.exp(e2 - e1)
    w1 = g_w / (1.0 + t)
    w2 = g_w * t / (1.0 + t)
    eid = jnp.where(lane == 0.0, i1 - N_GROUPS, jnp.where(lane == 1.0, i2 - N_GROUPS, 0.0))
    eid_ref[...] = eid.astype(jnp.int32)
    wt_ref[...] = jnp.where(lane == 0.0, w1, jnp.where(lane == 1.0, w2, 0.0))


def _router(x2, g_ffn, sc2, sh2, w_r):
    s, d = x2.shape
    tm = min(512, s)
    vec = lambda: pl.BlockSpec((1, d), lambda i: (0, 0))
    lanes_out = lambda: pl.BlockSpec((tm, LANES), lambda i: (i, 0))
    return pl.pallas_call(
        _router_kernel,
        out_shape=(jax.ShapeDtypeStruct((s, d), F32),
                   jax.ShapeDtypeStruct((s, LANES), jnp.int32),
                   jax.ShapeDtypeStruct((s, LANES), F32)),
        grid=(s // tm,),
        in_specs=[pl.BlockSpec((tm, d), lambda i: (i, 0)), vec(), vec(), vec(),
                  pl.BlockSpec((d, LANES), lambda i: (0, 0))],
        out_specs=(pl.BlockSpec((tm, d), lambda i: (i, 0)), lanes_out(), lanes_out()),
        compiler_params=_cparams(("arbitrary",)),
        name="router",
    )(x2, g_ffn, sc2, sh2, w_r)


def _rank_kernel(eid_ref, rank_ref, cnt_ref, run_ref):
    @pl.when(pl.program_id(0) == 0)
    def _():
        run_ref[...] = jnp.zeros_like(run_ref)

    tb = eid_ref.shape[0]
    lane = lax.broadcasted_iota(jnp.int32, (tb, LANES), 1)
    row = lax.broadcasted_iota(jnp.int32, (tb, tb), 0)
    col = lax.broadcasted_iota(jnp.int32, (tb, tb), 1)
    earlier = (col < row).astype(BF16)
    e = eid_ref[...]
    run = run_ref[...]
    ranks = []
    for k in range(2):
        onehot = e[:, k:k + 1] == lane
        oh = onehot.astype(F32)
        before = jnp.dot(earlier, oh.astype(BF16), preferred_element_type=F32) + run
        ranks.append(jnp.sum(jnp.where(onehot, before, 0.0), axis=-1, keepdims=True))
        run = run + jnp.sum(oh, axis=0, keepdims=True)
    rank_ref[...] = jnp.where(lane == 0, ranks[0], jnp.where(lane == 1, ranks[1], 0.0)).astype(jnp.int32)
    run_ref[...] = run
    cnt_ref[...] = run


def _rank(eid):
    s = eid.shape[0]
    tb = min(512, s)
    return pl.pallas_call(
        _rank_kernel,
        out_shape=(jax.ShapeDtypeStruct((s, LANES), jnp.int32), jax.ShapeDtypeStruct((1, LANES), F32)),
        grid=(s // tb,),
        in_specs=[pl.BlockSpec((tb, LANES), lambda i: (i, 0))],
        out_specs=(pl.BlockSpec((tb, LANES), lambda i: (i, 0)), pl.BlockSpec((1, LANES), lambda i: (0, 0))),
        scratch_shapes=[pltpu.VMEM((1, LANES), F32)],
        compiler_params=_cparams(("arbitrary",)),
        name="rank",
    )(eid)


def _plan_kernel(eid_ref, rank_ref, cnt_ref, dest_ref, src_ref, blke_ref, blkc_ref, nused_ref, pst_ref,
                 *, blk, n_blk):
    def per_expert(e, nb):
        c = cnt_ref[e]
        pst_ref[e] = nb * blk
        nbe = (c + (blk - 1)) // blk

        def per_block(jb, carry):
            blke_ref[nb + jb] = e
            blkc_ref[nb + jb] = jnp.minimum(blk, c - jb * blk)
            return carry

        lax.fori_loop(0, nbe, per_block, 0)

        def pad_slot(r, carry):
            src_ref[nb * blk + r] = 0
            return carry

        lax.fori_loop(c, (c + (GATHER_GROUP - 1)) // GATHER_GROUP * GATHER_GROUP, pad_slot, 0)
        return nb + nbe

    n_used = lax.fori_loop(0, N_EXPERTS, per_expert, jnp.int32(0))
    nused_ref[0] = n_used
    last_e = blke_ref[n_used - 1]

    def unused(i, carry):
        blke_ref[i] = last_e
        blkc_ref[i] = 0
        return carry

    lax.fori_loop(n_used, n_blk, unused, 0)

    n_tok = eid_ref.shape[0] // 2

    for k in range(eid_ref.shape[0] // n_tok):
        def place(tok, carry):
            a = k * n_tok + tok
            dst = pst_ref[eid_ref[a]] + rank_ref[a]
            dest_ref[a] = dst
            src_ref[dst] = tok
            return carry

        lax.fori_loop(0, n_tok, place, 0, unroll=16)


def _plan(eid_flat, rank_flat, counts, blk):
    a = eid_flat.shape[0]
    n_blk = a // blk + N_EXPERTS
    smem = lambda: pl.BlockSpec(memory_space=pltpu.SMEM)
    i32 = jnp.int32
    return pl.pallas_call(
        functools.partial(_plan_kernel, blk=blk, n_blk=n_blk),
        out_shape=(jax.ShapeDtypeStruct((a,), i32), jax.ShapeDtypeStruct((n_blk * blk,), i32),
                   jax.ShapeDtypeStruct((n_blk,), i32), jax.ShapeDtypeStruct((n_blk,), i32),
                   jax.ShapeDtypeStruct((1,), i32)),
        in_specs=[smem(), smem(), smem()],
        out_specs=(smem(), smem(), smem(), smem(), smem()),
        scratch_shapes=[pltpu.SMEM((N_EXPERTS,), i32)],
        name="plan",
    )(eid_flat, rank_flat, counts)


def _gather_rows(idx_of_row, n_rows, src_hbm, dst, sem):
    def body(g, carry):
        for rr in range(GATHER_GROUP):
            r = g * GATHER_GROUP + rr
            pltpu.make_async_copy(src_hbm.at[pl.ds(idx_of_row(r), 1), :], dst.at[pl.ds(r, 1), :], sem).start()
        return carry
    lax.fori_loop(0, (n_rows + (GATHER_GROUP - 1)) // GATHER_GROUP, body, 0)


def _wait_rows(n_rows, src_hbm, dst, sem):
    def body(g, carry):
        pltpu.make_async_copy(src_hbm.at[pl.ds(0, GATHER_GROUP), :], dst.at[pl.ds(0, GATHER_GROUP), :], sem).wait()
        return carry
    lax.fori_loop(0, (n_rows + (GATHER_GROUP - 1)) // GATHER_GROUP, body, 0)


def _expert_kernel(blke_ref, blkc_ref, nused_ref, src_ref, h_hbm, w1_hbm, w3_hbm, w2_hbm, ys_hbm,
                   xbuf, obuf, w1b, w3b, w2b, gsem, wsem, osem, *, blk, kc):
    i = pl.program_id(0)
    last = pl.num_programs(0) - 1
    n_used = nused_ref[0]
    e = blke_ref[i]
    e_next = blke_ref[jnp.minimum(i + 1, last)]
    first_of_expert = (i == 0) | (blke_ref[jnp.maximum(i - 1, 0)] != e)
    fetch_next = (i + 1 < n_used) & (e_next != e)
    wbufs = ((w1_hbm, w1b), (w3_hbm, w3b), (w2_hbm, w2b))

    def weight_copy(k, ee):
        return pltpu.make_async_copy(wbufs[k][0].at[ee], wbufs[k][1], wsem.at[k])

    def issue_rows(b, slot):
        _gather_rows(lambda r: src_ref[b * blk + r], blkc_ref[b], h_hbm, xbuf.at[slot], gsem.at[slot])

    def weights_ready(k):
        @pl.when(first_of_expert)
        def _():
            weight_copy(k, e).wait()

    def weights_done(k):
        @pl.when(fetch_next)
        def _():
            weight_copy(k, e_next).start()

    def out_copy(b, slot, ch):
        return pltpu.make_async_copy(obuf.at[slot, pl.ds(ch * OUT_CHUNK, OUT_CHUNK), :],
                                     ys_hbm.at[pl.ds(b * blk + ch * OUT_CHUNK, OUT_CHUNK), :], osem.at[slot])

    def for_out_chunks(b, fn):
        def body(ch, carry):
            fn(ch)
            return carry
        lax.fori_loop(0, (blkc_ref[b] + (OUT_CHUNK - 1)) // OUT_CHUNK, body, 0)

    slot = lax.rem(i, 2)

    @pl.when(i == 0)
    def _():
        for k in range(3):
            weight_copy(k, e).start()
        xbuf[...] = jnp.zeros_like(xbuf)
        issue_rows(0, 0)

    @pl.when(i < n_used)
    def _():
        _wait_rows(blkc_ref[i], h_hbm, xbuf.at[slot], gsem.at[slot])

        @pl.when(i + 1 < n_used)
        def _():
            issue_rows(i + 1, 1 - slot)

        d = xbuf.shape[2]

        def up_proj(wb):
            acc = None
            for k0 in range(0, d, kc):
                part = jnp.dot(xbuf[slot, :, k0:k0 + kc].astype(BF16), wb[k0:k0 + kc, :].astype(BF16),
                               preferred_element_type=F32)
                acc = part if acc is None else acc + part
            return acc

        weights_ready(0)
        a = up_proj(w1b)
        weights_done(0)
        weights_ready(1)
        b = up_proj(w3b)
        weights_done(1)
        hid = ((a * jax.nn.sigmoid(a)) * b).astype(BF16)
        weights_ready(2)

        @pl.when(i >= 2)
        def _():
            for_out_chunks(i - 2, lambda ch: out_copy(i - 2, slot, 0).wait())

        for t in range(0, d, Y_TILE):
            obuf[slot, :, t:t + Y_TILE] = jnp.dot(hid, w2b[:, t:t + Y_TILE].astype(BF16),
                                                  preferred_element_type=F32)
        weights_done(2)
        for_out_chunks(i, lambda ch: out_copy(i, slot, ch).start())

        @pl.when(i == n_used - 1)
        def _():
            @pl.when(i >= 1)
            def _():
                for_out_chunks(i - 1, lambda ch: out_copy(i - 1, 1 - slot, 0).wait())
            for_out_chunks(i, lambda ch: out_copy(i, slot, 0).wait())


def _experts(h2, w1, w3, w2, blk_e, blk_cnt, n_used, row_src, blk):
    s, d = h2.shape
    de = w1.shape[-1]
    kc = min(1024, d)
    assert d % kc == 0 and d % Y_TILE == 0 and blk % GATHER_GROUP == 0 and blk % OUT_CHUNK == 0
    n_blk = blk_e.shape[0]
    anyspace = lambda: pl.BlockSpec(memory_space=pl.ANY)
    return pl.pallas_call(
        functools.partial(_expert_kernel, blk=blk, kc=kc),
        out_shape=jax.ShapeDtypeStruct((n_blk * blk, d), F32),
        grid_spec=pltpu.PrefetchScalarGridSpec(
            num_scalar_prefetch=4,
            grid=(n_blk,),
            in_specs=[anyspace(), anyspace(), anyspace(), anyspace()],
            out_specs=anyspace(),
            scratch_shapes=[pltpu.VMEM((2, blk, d), F32), pltpu.VMEM((2, blk, d), F32),
                            pltpu.VMEM((d, de), F32), pltpu.VMEM((d, de), F32), pltpu.VMEM((de, d), F32),
                            pltpu.SemaphoreType.DMA((2,)), pltpu.SemaphoreType.DMA((3,)),
                            pltpu.SemaphoreType.DMA((2,))]),
        compiler_params=_cparams(("arbitrary",)),
        name="experts",
    )(blk_e, blk_cnt, n_used, row_src, h2, w1, w3, w2)


def _combine_kernel(dest_ref, ys_hbm, x_ref, wt_ref, gt_ref, o_ref, ybuf, sem):
    i = pl.program_id(0)
    nb = pl.num_programs(0)
    tm = x_ref.shape[0]
    n_tok = tm * nb

    def issue(blk_i, slot):
        for k in range(2):
            _gather_rows(lambda r: dest_ref[k * n_tok + blk_i * tm + r], tm, ys_hbm, ybuf.at[slot, k],
                         sem.at[slot])

    slot = lax.rem(i, 2)

    @pl.when(i == 0)
    def _():
        issue(0, 0)

    @pl.when(i + 1 < nb)
    def _():
        issue(i + 1, 1 - slot)

    for k in range(2):
        _wait_rows(tm, ys_hbm, ybuf.at[slot, k], sem.at[slot])
    w = wt_ref[...]
    y = ybuf[slot, 0] * w[:, 0:1] + ybuf[slot, 1] * w[:, 1:2]
    o_ref[...] = x_ref[...] + gt_ref[...] * y


def _combine(dest, ys, x2, wts, gt2):
    s, d = x2.shape
    tm = min(256, s)
    return pl.pallas_call(
        _combine_kernel,
        out_shape=jax.ShapeDtypeStruct((s, d), F32),
        grid_spec=pltpu.PrefetchScalarGridSpec(
            num_scalar_prefetch=1,
            grid=(s // tm,),
            in_specs=[pl.BlockSpec(memory_space=pl.ANY),
                      pl.BlockSpec((tm, d), lambda i, ds_: (i, 0)),
                      pl.BlockSpec((tm, LANES), lambda i, ds_: (i, 0)),
                      pl.BlockSpec((1, d), lambda i, ds_: (0, 0))],
            out_specs=pl.BlockSpec((tm, d), lambda i, ds_: (i, 0)),
            scratch_shapes=[pltpu.VMEM((2, 2, tm, d), F32), pltpu.SemaphoreType.DMA((2,))]),
        compiler_params=_cparams(("arbitrary",)),
        name="combine",
    )(dest, ys, x2, wts, gt2)


def _layer(x2d, c, w_ada, b_ada, g_mix, g_ffn, w_in, conv_w, q_norm_g, k_norm_g, g_branch, w_out,
           w_group, w_expert, w1, w3, w2):
    s, d = x2d.shape
    conv_ch = conv_w.shape[1]
    attn_w = d - conv_ch
    n_heads = attn_w // HEAD_DIM
    na = s // RES
    assert s % (RES * BAND) == 0
    row = lambda v: v.reshape(1, -1)

    mod_a = _ada(c, w_ada, b_ada, 2 * d)
    sh1, sc1 = mod_a[:, :d], mod_a[:, d:]

    h1 = _norm1(x2d, row(g_mix), sc1, sh1).reshape(s, d)
    z_conv, w_out_bf = _inproj_conv(h1, w_in, 3 * conv_ch, w_out)
    z_attn, mod_b = _inproj(h1, w_in, row(q_norm_g), row(k_norm_g), 3 * conv_ch, attn_w, c, w_ada, b_ada, 2 * d)
    gt1, sh2, sc2, gt2 = [mod_b[:, i * d:(i + 1) * d] for i in range(4)]
    y_conv = _conv(z_conv.reshape(RES, na, 3 * conv_ch), conv_w, row(g_branch[:conv_ch]), conv_ch)
    slopes = jnp.exp2(-ALIBI_MAX_EXP * jnp.arange(1, n_heads + 1, dtype=F32) / n_heads)
    y_attn = _attention(z_attn.reshape(RES, na, 3 * attn_w), slopes, attn_w)
    x2 = _outproj(y_conv, y_attn, row(g_branch[conv_ch:]), w_out_bf, x2d, gt1)

    w_r = jnp.concatenate([w_group, w_expert,
                           jnp.zeros((d, LANES - N_GROUPS - N_EXPERTS), F32)], axis=1)
    h2, eid, wts = _router(x2, row(g_ffn), sc2, sh2, w_r)
    rank, counts = _rank(eid)
    eid_flat = eid[:, :2].T.reshape(-1)
    rank_flat = rank[:, :2].T.reshape(-1)
    cnt = counts[0, :N_EXPERTS].astype(jnp.int32)
    dest, row_src, blk_e, blk_cnt, n_used = _plan(eid_flat, rank_flat, cnt, MOE_BLOCK)
    ys = _experts(h2, w1, w3, w2, blk_e, blk_cnt, n_used, row_src, MOE_BLOCK)
    return _combine(dest, ys, x2, wts, gt2)


def kernel(x, c, w_ada, b_ada, g_mix, g_ffn, w_in, conv_w, q_norm_g, k_norm_g, g_branch, w_out,
           w_group, w_expert, w1, w3, w2):
    b, s, d = x.shape
    assert b == 1 and c.shape[0] == 1
    for l in range(w_ada.shape[0]):
        x = _layer(x.reshape(s, d), c[0], w_ada[l], b_ada[l], g_mix[l], g_ffn[l], w_in[l], conv_w[l],
                   q_norm_g[l], k_norm_g[l], g_branch[l], w_out[l], w_group[l], w_expert[l],
                   w1[l], w3[l], w2[l]).reshape(b, s, d)
    return x
```

```python
import functools
import math

import jax
import jax.numpy as jnp
from jax import lax
from jax.experimental import pallas as pl
from jax.experimental.pallas import tpu as pltpu

EPS = 1e-6
HEAD_DIM = 128
BAND = 128
PATTERNS = ((128, 1), (512, 4), (2048, 16))
RES = 16
ALIBI_MAX_EXP = 8.0
N_GROUPS = 8
EXPERTS_PER_GROUP = 8
N_EXPERTS = N_GROUPS * EXPERTS_PER_GROUP
MOE_BLOCK = 256
GATHER_GROUP = 8
Y_TILE = 512
OUT_CHUNK = 64
ADA_SLAB = 128
PERM_ROWS = 16
CAST_ROWS = 128
LANES = 128
SUBLANES = 8
VMEM_LIMIT = 56 * 1024 * 1024
F32 = jnp.float32
BF16 = jnp.bfloat16
NEG_INF = float("-inf")
LOG2E = 1.4426950408889634


def _cparams(sem):
    return pltpu.CompilerParams(dimension_semantics=sem, vmem_limit_bytes=VMEM_LIMIT)


def _ada_kernel(c_ref, w_ref, b_ref, o_ref, s_ref, *, rows):
    @pl.when(pl.program_id(0) == 0)
    def _():
        c = c_ref[...]
        s_ref[...] = c * jax.nn.sigmoid(c)

    d, tn = w_ref.shape

    def body(i, acc):
        r0 = pl.multiple_of(i * rows, rows)
        p = w_ref[pl.ds(r0, rows), :] * s_ref[pl.ds(r0, rows), :]
        return acc + jnp.sum(p.reshape(rows // SUBLANES, SUBLANES, tn), axis=0)

    acc = lax.fori_loop(0, d // rows, body, jnp.zeros((SUBLANES, tn), F32))
    o_ref[...] = jnp.sum(acc, axis=0, keepdims=True) + b_ref[...]


def _ada_slab(w_ref, s_ref, b_ref):
    d, tn = w_ref.shape
    rows = min(512, d)
    acc = jnp.zeros((SUBLANES, tn), F32)
    for r0 in range(0, d, rows):
        p = w_ref[r0:r0 + rows, :] * s_ref[r0:r0 + rows, :]
        acc = acc + jnp.sum(p.reshape(rows // SUBLANES, SUBLANES, tn), axis=0)
    return jnp.sum(acc, axis=0, keepdims=True) + b_ref[...]


def _ada(c, w_ada, b_ada, n):
    d = w_ada.shape[0]
    tn = min(512, n)
    rows = min(512, d)
    return pl.pallas_call(
        functools.partial(_ada_kernel, rows=rows),
        out_shape=jax.ShapeDtypeStruct((1, n), F32),
        grid=(n // tn,),
        in_specs=[pl.BlockSpec((d, 1), lambda j: (0, 0)),
                  pl.BlockSpec((d, tn), lambda j: (0, j)),
                  pl.BlockSpec((1, tn), lambda j: (0, j))],
        out_specs=pl.BlockSpec((1, tn), lambda j: (0, j)),
        scratch_shapes=[pltpu.VMEM((d, 1), F32)],
        compiler_params=_cparams(("arbitrary",)),
        name="ada",
    )(c.reshape(d, 1), w_ada, b_ada.reshape(1, -1))


def _rms_mod(x, g, sc, sh):
    ms = jnp.mean(x * x, axis=-1, keepdims=True)
    return (x * lax.rsqrt(ms + EPS) * g) * (1.0 + sc) + sh


def _row_permutation(ta, to_residue_major):
    n = RES * ta
    assert ta & (ta - 1) == 0
    sh = ta.bit_length() - 1
    r = lax.broadcasted_iota(jnp.int32, (n, n), 0)
    c = lax.broadcasted_iota(jnp.int32, (n, n), 1)
    res_major, natural = (r, c) if to_residue_major else (c, r)
    return (natural == RES * (res_major & (ta - 1)) + (res_major >> sh)).astype(BF16)


def _norm1_kernel(x_ref, g_ref, sc_ref, sh_ref, o_ref):
    ta = o_ref.shape[1]
    h = _rms_mod(x_ref[...], g_ref[...], sc_ref[...], sh_ref[...]).astype(BF16)
    hp = jnp.dot(_row_permutation(ta, True), h, preferred_element_type=F32)
    o_ref[...] = hp.astype(BF16).reshape(o_ref.shape)


def _norm1(x2d, g_mix, sc1, sh1):
    s, d = x2d.shape
    na = s // RES
    ta = min(32, na)
    vec = lambda: pl.BlockSpec((1, d), lambda i: (0, 0))
    return pl.pallas_call(
        _norm1_kernel,
        out_shape=jax.ShapeDtypeStruct((RES, na, d), BF16),
        grid=(na // ta,),
        in_specs=[pl.BlockSpec((RES * ta, d), lambda i: (i, 0)), vec(), vec(), vec()],
        out_specs=pl.BlockSpec((RES, ta, d), lambda i: (0, i, 0)),
        compiler_params=_cparams(("arbitrary",)),
        name="norm1",
    )(x2d, g_mix, sc1, sh1)


def _inproj_kernel(h_ref, w_ref, qg_ref, kg_ref, c_ref, wa_ref, ba_ref, z_ref, mod_ref, wb_ref, s_ref,
                   *, q0, k0, v0):
    j = pl.program_id(0)

    @pl.when((j == 0) & (pl.program_id(1) == 0))
    def _():
        c = c_ref[...]
        s_ref[...] = c * jax.nn.sigmoid(c)

    @pl.when(pl.program_id(1) == 0)
    def _():
        wb_ref[...] = w_ref[...].astype(BF16)

    mod_ref[...] = _ada_slab(wa_ref, s_ref, ba_ref)
    acc = jnp.dot(h_ref[...], wb_ref[...], preferred_element_type=F32)
    is_q = (j >= q0) & (j < k0)
    is_k = (j >= k0) & (j < v0)

    g = jnp.where(is_q, qg_ref[...] * (HEAD_DIM ** -0.5 * LOG2E), kg_ref[...])
    for hh in range(acc.shape[1] // HEAD_DIM):
        a = acc[:, hh * HEAD_DIM:(hh + 1) * HEAD_DIM]
        ms = jnp.mean(a * a, axis=-1, keepdims=True)
        scale = jnp.where(is_q | is_k, lax.rsqrt(ms + EPS) * g, 1.0)
        z_ref[:, hh * HEAD_DIM:(hh + 1) * HEAD_DIM] = a * scale


def _inproj_conv_kernel(h_ref, w_ref, wo_ref, z_ref, wob_ref, wb_ref):
    @pl.when(pl.program_id(1) == 0)
    def _():
        wb_ref[...] = w_ref[...].astype(BF16)

    wob_ref[...] = wo_ref[...].astype(BF16)
    z_ref[...] = jnp.dot(h_ref[...], wb_ref[...], preferred_element_type=F32).astype(BF16)


def _inproj_conv(h1, w_in, n_cols, w_out):
    s, d = h1.shape
    tm = min(1024, s)
    tn = math.gcd(512, n_cols)
    ni = s // tm
    wr, wc = w_out.shape
    ws = min(CAST_ROWS, wr)
    while wr // ws > (n_cols // tn) * ni:
        ws *= 2
    n_ws = wr // ws
    assert wr % ws == 0
    wslab = lambda: pl.BlockSpec((ws, wc), lambda j, i: (jnp.minimum(j * ni + i, n_ws - 1), 0))
    return pl.pallas_call(
        _inproj_conv_kernel,
        out_shape=(jax.ShapeDtypeStruct((s, n_cols), BF16), jax.ShapeDtypeStruct((wr, wc), BF16)),
        grid=(n_cols // tn, ni),
        in_specs=[pl.BlockSpec((tm, d), lambda j, i: (i, 0)),
                  pl.BlockSpec((d, tn), lambda j, i: (0, j)), wslab()],
        out_specs=(pl.BlockSpec((tm, tn), lambda j, i: (i, j)), wslab()),
        scratch_shapes=[pltpu.VMEM((d, tn), BF16)],
        compiler_params=_cparams(("arbitrary", "arbitrary")),
        name="inproj_conv",
    )(h1, w_in, w_out)


def _inproj(h1, w_in, q_g, k_g, col0, attn_w, c, w_ada, b_ada, mod_done):
    s, d = h1.shape
    n = w_in.shape[1] - col0
    tm = min(1024, s)
    tn = math.gcd(512, attn_w, col0)
    assert n == 3 * attn_w and tn % HEAD_DIM == 0
    j0 = col0 // tn
    q0 = 0
    k0 = q0 + attn_w // tn
    v0 = k0 + attn_w // tn
    ni = s // tm
    n_mod = w_ada.shape[1] - mod_done
    n_slab = n_mod // ADA_SLAB
    assert n_mod % ADA_SLAB == 0 and mod_done % ADA_SLAB == 0 and n_slab <= (n // tn) * ni
    slab = lambda j, i: jnp.minimum(j * ni + i, n_slab - 1)
    hvec = lambda: pl.BlockSpec((1, HEAD_DIM), lambda j, i: (0, 0))
    return pl.pallas_call(
        functools.partial(_inproj_kernel, q0=q0, k0=k0, v0=v0),
        out_shape=(jax.ShapeDtypeStruct((s, n), F32), jax.ShapeDtypeStruct((1, n_mod), F32)),
        grid=(n // tn, ni),
        in_specs=[pl.BlockSpec((tm, d), lambda j, i: (i, 0)),
                  pl.BlockSpec((d, tn), lambda j, i: (0, j0 + j)), hvec(), hvec(),
                  pl.BlockSpec((d, 1), lambda j, i: (0, 0)),
                  pl.BlockSpec((d, ADA_SLAB), lambda j, i: (0, mod_done // ADA_SLAB + slab(j, i))),
                  pl.BlockSpec((1, ADA_SLAB), lambda j, i: (0, mod_done // ADA_SLAB + slab(j, i)))],
        out_specs=(pl.BlockSpec((tm, tn), lambda j, i: (i, j)),
                   pl.BlockSpec((1, ADA_SLAB), lambda j, i: (0, slab(j, i)))),
        scratch_shapes=[pltpu.VMEM((d, tn), BF16), pltpu.VMEM((d, 1), F32)],
        compiler_params=_cparams(("arbitrary", "arbitrary")),
        name="inproj",
    )(h1, w_in, q_g, k_g, c.reshape(d, 1), w_ada, b_ada.reshape(1, -1))


def _conv_kernel(b_ref, c_ref, v_ref, cw_ref, g_ref, o_ref, ue_ref):
    i = pl.program_id(0)
    ta = b_ref.shape[1]
    w = cw_ref[...]
    assert w.shape[0] == 3
    for k in range(2):
        @pl.when(i == 0)
        def _():
            ue_ref[k, 0:SUBLANES, :] = jnp.zeros((SUBLANES, ue_ref.shape[2]), F32)

        @pl.when(i > 0)
        def _():
            ue_ref[k, 0:SUBLANES, :] = ue_ref[k, ta:ta + SUBLANES, :]

        r = RES - 2 + k
        ue_ref[k, SUBLANES:SUBLANES + ta, :] = c_ref[r].astype(F32) * v_ref[r].astype(F32)

    def u_at(bb):
        if bb >= 0:
            return c_ref[bb].astype(F32) * v_ref[bb].astype(F32)
        return ue_ref[2 + bb, SUBLANES - 1:SUBLANES - 1 + ta, :]

    for b in range(RES):
        y = b_ref[b].astype(F32) * (w[0:1, :] * u_at(b - 2) + w[1:2, :] * u_at(b - 1) + w[2:3, :] * u_at(b))
        ms = jnp.mean(y * y, axis=-1, keepdims=True)
        o_ref[b] = (y * lax.rsqrt(ms + EPS) * g_ref[...]).astype(BF16)


def _conv(z3, conv_w, g_conv, conv_ch):
    na = z3.shape[1]
    ta = min(64, na)
    blk = lambda jj: pl.BlockSpec((RES, ta, conv_ch), lambda i: (0, i, jj))
    return pl.pallas_call(
        _conv_kernel,
        out_shape=jax.ShapeDtypeStruct((RES, na, conv_ch), BF16),
        grid=(na // ta,),
        in_specs=[blk(0), blk(1), blk(2),
                  pl.BlockSpec(conv_w.shape, lambda i: (0, 0)),
                  pl.BlockSpec((1, conv_ch), lambda i: (0, 0))],
        out_specs=pl.BlockSpec((RES, ta, conv_ch), lambda i: (0, i, 0)),
        scratch_shapes=[pltpu.VMEM((2, ta + SUBLANES, conv_ch), F32)],
        compiler_params=_cparams(("arbitrary",)),
        name="conv",
    )(z3, z3, z3, conv_w, g_conv)


def _attn_kernel(slopes_ref, q_ref, kp_ref, kc_ref, vp_ref, vc_ref, o_ref, bias_ref, m_ref, l_ref, acc_ref):
    h = pl.program_id(0)
    cidx = pl.program_id(1)
    slope = slopes_ref[h] * LOG2E
    first_pen = jnp.where(cidx > 0, 0.0, NEG_INF).astype(F32)
    dn_t = (((1,), (1,)), ((), ()))
    ones_kv = jnp.ones((2 * BAND, HEAD_DIM), BF16)

    j = lax.broadcasted_iota(jnp.int32, (BAND, 2 * BAND), 0)
    u = lax.broadcasted_iota(jnp.int32, (BAND, 2 * BAND), 1)
    is_prev = u < BAND
    uu = jnp.where(is_prev, u, u - BAND)
    for pi, (window, dil) in enumerate(PATTERNS):
        assert window // dil == BAND and RES % dil == 0
        ns, sr = RES // dil, BAND * dil // RES
        assert sr & (sr - 1) == 0
        sh = sr.bit_length() - 1
        mq = (j & (sr - 1)) * ns + (j >> sh)
        mk = (uu & (sr - 1)) * ns + (uu >> sh)
        steps = mq - mk + jnp.where(is_prev, BAND, 0)
        valid = (is_prev & (steps <= BAND)) | (jnp.logical_not(is_prev) & (steps >= 0))
        bias = jnp.where(valid, -slope * (steps * dil).astype(F32), NEG_INF)
        bias_ref[pi, 0] = bias
        bias_ref[pi, 1] = bias + jnp.where(is_prev, first_pen, 0.0)

    for pi, (window, dil) in enumerate(PATTERNS):
        ns, sr = RES // dil, BAND * dil // RES
        n_chunks = BAND // sr
        for r in range(dil):
            k_prev = v_prev = None
            for n in range(n_chunks):
                def gather(ref, nn):
                    return jnp.concatenate([ref[dil * c + r, nn * sr:(nn + 1) * sr, :] for c in range(ns)], axis=0)

                q = gather(q_ref, n).astype(BF16)
                if n == 0:
                    k_prev = gather(kp_ref, n_chunks - 1).astype(BF16)
                    v_prev = gather(vp_ref, n_chunks - 1).astype(BF16)
                k_cur, v_cur = gather(kc_ref, n).astype(BF16), gather(vc_ref, n).astype(BF16)
                kcat = jnp.concatenate([k_prev, k_cur], axis=0)
                vcat = jnp.concatenate([v_prev, v_cur], axis=0)
                k_prev, v_prev = k_cur, v_cur
                s = lax.dot_general(q, kcat, dn_t, preferred_element_type=F32) + bias_ref[pi, 1 if n == 0 else 0]
                m = jnp.max(s, axis=-1, keepdims=True)
                p = jnp.exp2(s - m).astype(BF16)
                ol = jnp.dot(p, jnp.concatenate([vcat, ones_kv], axis=1), preferred_element_type=F32)
                o, l_all = ol[:, :HEAD_DIM], ol[:, HEAD_DIM:]
                for c in range(ns):
                    rows = slice(c * sr, (c + 1) * sr)
                    at = (dil * c + r, slice(n * sr, (n + 1) * sr), slice(None))
                    m_b = jnp.broadcast_to(m[rows], (sr, HEAD_DIM))
                    l_b = l_all[rows]
                    if pi == 0:
                        m_ref[at] = m_b
                        l_ref[at] = l_b
                        acc_ref[at] = o[rows]
                    else:
                        m_old = m_ref[at]
                        m_new = jnp.maximum(m_old, m_b)
                        a_old = jnp.exp2(m_old - m_new)
                        a_new = jnp.exp2(m_b - m_new)
                        l_new = a_old * l_ref[at] + a_new * l_b
                        o_new = a_old * acc_ref[at] + a_new * o[rows]
                        if pi == len(PATTERNS) - 1:
                            acc_ref[at] = o_new / l_new
                        else:
                            m_ref[at] = m_new
                            l_ref[at] = l_new
                            acc_ref[at] = o_new
    o_ref[...] = acc_ref[...].astype(BF16)


def _attention(z3, slopes, attn_w):
    na = z3.shape[1]
    n_heads = attn_w // HEAD_DIM
    assert na % BAND == 0
    qb = 0
    kb = qb + n_heads
    vb = kb + n_heads
    blk = (RES, BAND, HEAD_DIM)
    cur = lambda base: pl.BlockSpec(blk, lambda h, c, sl: (0, c, base + h))
    prev = lambda base: pl.BlockSpec(blk, lambda h, c, sl: (0, jnp.maximum(c - 1, 0), base + h))
    return pl.pallas_call(
        _attn_kernel,
        out_shape=jax.ShapeDtypeStruct((RES, na, attn_w), BF16),
        grid_spec=pltpu.PrefetchScalarGridSpec(
            num_scalar_prefetch=1,
            grid=(n_heads, na // BAND),
            in_specs=[cur(qb), prev(kb), cur(kb), prev(vb), cur(vb)],
            out_specs=pl.BlockSpec(blk, lambda h, c, sl: (0, c, h)),
            scratch_shapes=[pltpu.VMEM((len(PATTERNS), 2, BAND, 2 * BAND), F32),
                            pltpu.VMEM(blk, F32), pltpu.VMEM(blk, F32), pltpu.VMEM(blk, F32)]),
        compiler_params=_cparams(("arbitrary", "arbitrary")),
        name="attn",
    )(slopes, z3, z3, z3, z3, z3)


def _outproj_kernel(yc_ref, ya_ref, ga_ref, w_ref, x_ref, gt_ref, o_ref, y_ref):
    _, ta, cc = yc_ref.shape

    @pl.when(pl.program_id(1) == 0)
    def _():
        tp = min(PERM_ROWS, ta)
        n = RES * tp
        perm = _row_permutation(tp, False)
        for t0 in range(0, ta, tp):
            ya = ya_ref[:, t0:t0 + tp, :].reshape(n, ya_ref.shape[2]).astype(F32)
            ms = jnp.mean(ya * ya, axis=-1, keepdims=True)
            yn = (ya * lax.rsqrt(ms + EPS) * ga_ref[...]).astype(BF16)
            yc = yc_ref[:, t0:t0 + tp, :].reshape(n, cc)
            rows = slice(RES * t0, RES * t0 + n)
            y_ref[rows, 0:cc] = jnp.dot(perm, yc, preferred_element_type=F32).astype(BF16)
            y_ref[rows, cc:] = jnp.dot(perm, yn, preferred_element_type=F32).astype(BF16)

    acc = jnp.dot(y_ref[...], w_ref[...], preferred_element_type=F32)
    o_ref[...] = x_ref[...] + gt_ref[...] * acc


def _outproj(y_conv3, y_attn3, g_attn, w_out_bf, x2d, gt1):
    s, d = x2d.shape
    na = s // RES
    cc, aw = y_conv3.shape[2], y_attn3.shape[2]
    ta = min(64, na)
    tn = min(512, d)
    return pl.pallas_call(
        _outproj_kernel,
        out_shape=jax.ShapeDtypeStruct((s, d), F32),
        grid=(na // ta, d // tn),
        in_specs=[pl.BlockSpec((RES, ta, cc), lambda i, j: (0, i, 0)),
                  pl.BlockSpec((RES, ta, aw), lambda i, j: (0, i, 0)),
                  pl.BlockSpec((1, aw), lambda i, j: (0, 0)),
                  pl.BlockSpec((cc + aw, tn), lambda i, j: (0, j)),
                  pl.BlockSpec((RES * ta, tn), lambda i, j: (i, j)),
                  pl.BlockSpec((1, tn), lambda i, j: (0, j))],
        out_specs=pl.BlockSpec((RES * ta, tn), lambda i, j: (i, j)),
        scratch_shapes=[pltpu.VMEM((RES * ta, cc + aw), BF16)],
        compiler_params=_cparams(("arbitrary", "arbitrary")),
        name="outproj",
    )(y_conv3, y_attn3, g_attn, w_out_bf, x2d, gt1)


def _router_kernel(x_ref, g_ref, sc_ref, sh_ref, wr_ref, h_ref, eid_ref, wt_ref):
    h = _rms_mod(x_ref[...], g_ref[...], sc_ref[...], sh_ref[...])
    h_ref[...] = h
    w = wr_ref[...]
    h_hi, w_hi = h.astype(BF16), w.astype(BF16)
    h_lo = (h - h_hi.astype(F32)).astype(BF16)
    w_lo = (w - w_hi.astype(F32)).astype(BF16)
    logits = (jnp.dot(h_hi, w_hi, preferred_element_type=F32) + jnp.dot(h_hi, w_lo, preferred_element_type=F32)
              + jnp.dot(h_lo, w_hi, preferred_element_type=F32))
    tm = logits.shape[0]
    lane = lax.broadcasted_iota(jnp.int32, (tm, LANES), 1).astype(F32)
    big = float(LANES)
    is_g = lane < N_GROUPS
    gl = jnp.where(is_g, logits, NEG_INF)
    gmax = jnp.max(gl, axis=-1, keepdims=True)
    gsel = jnp.min(jnp.where(gl == gmax, lane, big), axis=-1, keepdims=True)
    g_w = 1.0 / jnp.sum(jnp.where(is_g, jnp.exp(gl - gmax), 0.0), axis=-1, keepdims=True)
    lo = N_GROUPS + gsel * EXPERTS_PER_GROUP
    in_grp = (lane >= lo) & (lane < lo + EXPERTS_PER_GROUP)
    el = jnp.where(in_grp, logits, NEG_INF)
    e1 = jnp.max(el, axis=-1, keepdims=True)
    i1 = jnp.min(jnp.where(el == e1, lane, big), axis=-1, keepdims=True)
    el2 = jnp.where(lane == i1, NEG_INF, el)
    e2 = jnp.max(el2, axis=-1, keepdims=True)
    i2 = jnp.min(jnp.where(el2 == e2, lane, big), axis=-1, keepdims=True)
    t = jnp.exp(e2 - e1)
    w1 = g_w / (1.0 + t)
    w2 = g_w * t / (1.0 + t)
    eid = jnp.where(lane == 0.0, i1 - N_GROUPS, jnp.where(lane == 1.0, i2 - N_GROUPS, 0.0))
    eid_ref[...] = eid.astype(jnp.int32)
    wt_ref[...] = jnp.where(lane == 0.0, w1, jnp.where(lane == 1.0, w2, 0.0))


def _router(x2, g_ffn, sc2, sh2, w_r):
    s, d = x2.shape
    tm = min(512, s)
    vec = lambda: pl.BlockSpec((1, d), lambda i: (0, 0))
    lanes_out = lambda: pl.BlockSpec((tm, LANES), lambda i: (i, 0))
    return pl.pallas_call(
        _router_kernel,
        out_shape=(jax.ShapeDtypeStruct((s, d), F32),
                   jax.ShapeDtypeStruct((s, LANES), jnp.int32),
                   jax.ShapeDtypeStruct((s, LANES), F32)),
        grid=(s // tm,),
        in_specs=[pl.BlockSpec((tm, d), lambda i: (i, 0)), vec(), vec(), vec(),
                  pl.BlockSpec((d, LANES), lambda i: (0, 0))],
        out_specs=(pl.BlockSpec((tm, d), lambda i: (i, 0)), lanes_out(), lanes_out()),
        compiler_params=_cparams(("arbitrary",)),
        name="router",
    )(x2, g_ffn, sc2, sh2, w_r)


def _rank_kernel(eid_ref, rank_ref, cnt_ref, run_ref):
    @pl.when(pl.program_id(0) == 0)
    def _():
        run_ref[...] = jnp.zeros_like(run_ref)

    tb = eid_ref.shape[0]
    lane = lax.broadcasted_iota(jnp.int32, (tb, LANES), 1)
    row = lax.broadcasted_iota(jnp.int32, (tb, tb), 0)
    col = lax.broadcasted_iota(jnp.int32, (tb, tb), 1)
    earlier = (col < row).astype(BF16)
    e = eid_ref[...]
    run = run_ref[...]
    ranks = []
    for k in range(2):
        onehot = e[:, k:k + 1] == lane
        oh = onehot.astype(F32)
        before = jnp.dot(earlier, oh.astype(BF16), preferred_element_type=F32) + run
        ranks.append(jnp.sum(jnp.where(onehot, before, 0.0), axis=-1, keepdims=True))
        run = run + jnp.sum(oh, axis=0, keepdims=True)
    rank_ref[...] = jnp.where(lane == 0, ranks[0], jnp.where(lane == 1, ranks[1], 0.0)).astype(jnp.int32)
    run_ref[...] = run
    cnt_ref[...] = run


def _rank(eid):
    s = eid.shape[0]
    tb = min(512, s)
    return pl.pallas_call(
        _rank_kernel,
        out_shape=(jax.ShapeDtypeStruct((s, LANES), jnp.int32), jax.ShapeDtypeStruct((1, LANES), F32)),
        grid=(s // tb,),
        in_specs=[pl.BlockSpec((tb, LANES), lambda i: (i, 0))],
        out_specs=(pl.BlockSpec((tb, LANES), lambda i: (i, 0)), pl.BlockSpec((1, LANES), lambda i: (0, 0))),
        scratch_shapes=[pltpu.VMEM((1, LANES), F32)],
        compiler_params=_cparams(("arbitrary",)),
        name="rank",
    )(eid)


def _plan_kernel(eid_ref, rank_ref, cnt_ref, dest_ref, src_ref, blke_ref, blkc_ref, nused_ref, pst_ref,
                 *, blk, n_blk):
    def per_expert(e, nb):
        c = cnt_ref[e]
        pst_ref[e] = nb * blk
        nbe = (c + (blk - 1)) // blk

        def per_block(jb, carry):
            blke_ref[nb + jb] = e
            blkc_ref[nb + jb] = jnp.minimum(blk, c - jb * blk)
            return carry

        lax.fori_loop(0, nbe, per_block, 0)

        def pad_slot(r, carry):
            src_ref[nb * blk + r] = 0
            return carry

        lax.fori_loop(c, (c + (GATHER_GROUP - 1)) // GATHER_GROUP * GATHER_GROUP, pad_slot, 0)
        return nb + nbe

    n_used = lax.fori_loop(0, N_EXPERTS, per_expert, jnp.int32(0))
    nused_ref[0] = n_used
    last_e = blke_ref[n_used - 1]

    def unused(i, carry):
        blke_ref[i] = last_e
        blkc_ref[i] = 0
        return carry

    lax.fori_loop(n_used, n_blk, unused, 0)

    n_tok = eid_ref.shape[0] // 2

    for k in range(eid_ref.shape[0] // n_tok):
        def place(tok, carry):
            a = k * n_tok + tok
            dst = pst_ref[eid_ref[a]] + rank_ref[a]
            dest_ref[a] = dst
            src_ref[dst] = tok
            return carry

        lax.fori_loop(0, n_tok, place, 0, unroll=16)


def _plan(eid_flat, rank_flat, counts, blk):
    a = eid_flat.shape[0]
    n_blk = a // blk + N_EXPERTS
    smem = lambda: pl.BlockSpec(memory_space=pltpu.SMEM)
    i32 = jnp.int32
    return pl.pallas_call(
        functools.partial(_plan_kernel, blk=blk, n_blk=n_blk),
        out_shape=(jax.ShapeDtypeStruct((a,), i32), jax.ShapeDtypeStruct((n_blk * blk,), i32),
                   jax.ShapeDtypeStruct((n_blk,), i32), jax.ShapeDtypeStruct((n_blk,), i32),
                   jax.ShapeDtypeStruct((1,), i32)),
        in_specs=[smem(), smem(), smem()],
        out_specs=(smem(), smem(), smem(), smem(), smem()),
        scratch_shapes=[pltpu.SMEM((N_EXPERTS,), i32)],
        name="plan",
    )(eid_flat, rank_flat, counts)


def _gather_rows(idx_of_row, n_rows, src_hbm, dst, sem):
    def body(g, carry):
        for rr in range(GATHER_GROUP):
            r = g * GATHER_GROUP + rr
            pltpu.make_async_copy(src_hbm.at[pl.ds(idx_of_row(r), 1), :], dst.at[pl.ds(r, 1), :], sem).start()
        return carry
    lax.fori_loop(0, (n_rows + (GATHER_GROUP - 1)) // GATHER_GROUP, body, 0)


def _wait_rows(n_rows, src_hbm, dst, sem):
    def body(g, carry):
        pltpu.make_async_copy(src_hbm.at[pl.ds(0, GATHER_GROUP), :], dst.at[pl.ds(0, GATHER_GROUP), :], sem).wait()
        return carry
    lax.fori_loop(0, (n_rows + (GATHER_GROUP - 1)) // GATHER_GROUP, body, 0)


def _expert_kernel(blke_ref, blkc_ref, nused_ref, src_ref, h_hbm, w1_hbm, w3_hbm, w2_hbm, ys_hbm,
                   xbuf, obuf, w1b, w3b, w2b, gsem, wsem, osem, *, blk, kc):
    i = pl.program_id(0)
    last = pl.num_programs(0) - 1
    n_used = nused_ref[0]
    e = blke_ref[i]
    e_next = blke_ref[jnp.minimum(i + 1, last)]
    first_of_expert = (i == 0) | (blke_ref[jnp.maximum(i - 1, 0)] != e)
    fetch_next = (i + 1 < n_used) & (e_next != e)
    wbufs = ((w1_hbm, w1b), (w3_hbm, w3b), (w2_hbm, w2b))

    def weight_copy(k, ee):
        return pltpu.make_async_copy(wbufs[k][0].at[ee], wbufs[k][1], wsem.at[k])

    def issue_rows(b, slot):
        _gather_rows(lambda r: src_ref[b * blk + r], blkc_ref[b], h_hbm, xbuf.at[slot], gsem.at[slot])

    def weights_ready(k):
        @pl.when(first_of_expert)
        def _():
            weight_copy(k, e).wait()

    def weights_done(k):
        @pl.when(fetch_next)
        def _():
            weight_copy(k, e_next).start()

    def out_copy(b, slot, ch):
        return pltpu.make_async_copy(obuf.at[slot, pl.ds(ch * OUT_CHUNK, OUT_CHUNK), :],
                                     ys_hbm.at[pl.ds(b * blk + ch * OUT_CHUNK, OUT_CHUNK), :], osem.at[slot])

    def for_out_chunks(b, fn):
        def body(ch, carry):
            fn(ch)
            return carry
        lax.fori_loop(0, (blkc_ref[b] + (OUT_CHUNK - 1)) // OUT_CHUNK, body, 0)

    slot = lax.rem(i, 2)

    @pl.when(i == 0)
    def _():
        for k in range(3):
            weight_copy(k, e).start()
        xbuf[...] = jnp.zeros_like(xbuf)
        issue_rows(0, 0)

    @pl.when(i < n_used)
    def _():
        _wait_rows(blkc_ref[i], h_hbm, xbuf.at[slot], gsem.at[slot])

        @pl.when(i + 1 < n_used)
        def _():
            issue_rows(i + 1, 1 - slot)

        d = xbuf.shape[2]

        def up_proj(wb):
            acc = None
            for k0 in range(0, d, kc):
                part = jnp.dot(xbuf[slot, :, k0:k0 + kc].astype(BF16), wb[k0:k0 + kc, :].astype(BF16),
                               preferred_element_type=F32)
                acc = part if acc is None else acc + part
            return acc

        weights_ready(0)
        a = up_proj(w1b)
        weights_done(0)
        weights_ready(1)
        b = up_proj(w3b)
        weights_done(1)
        hid = ((a * jax.nn.sigmoid(a)) * b).astype(BF16)
        weights_ready(2)

        @pl.when(i >= 2)
        def _():
            for_out_chunks(i - 2, lambda ch: out_copy(i - 2, slot, 0).wait())

        for t in range(0, d, Y_TILE):
            obuf[slot, :, t:t + Y_TILE] = jnp.dot(hid, w2b[:, t:t + Y_TILE].astype(BF16),
                                                  preferred_element_type=F32)
        weights_done(2)
        for_out_chunks(i, lambda ch: out_copy(i, slot, ch).start())

        @pl.when(i == n_used - 1)
        def _():
            @pl.when(i >= 1)
            def _():
                for_out_chunks(i - 1, lambda ch: out_copy(i - 1, 1 - slot, 0).wait())
            for_out_chunks(i, lambda ch: out_copy(i, slot, 0).wait())


def _experts(h2, w1, w3, w2, blk_e, blk_cnt, n_used, row_src, blk):
    s, d = h2.shape
    de = w1.shape[-1]
    kc = min(1024, d)
    assert d % kc == 0 and d % Y_TILE == 0 and blk % GATHER_GROUP == 0 and blk % OUT_CHUNK == 0
    n_blk = blk_e.shape[0]
    anyspace = lambda: pl.BlockSpec(memory_space=pl.ANY)
    return pl.pallas_call(
        functools.partial(_expert_kernel, blk=blk, kc=kc),
        out_shape=jax.ShapeDtypeStruct((n_blk * blk, d), F32),
        grid_spec=pltpu.PrefetchScalarGridSpec(
            num_scalar_prefetch=4,
            grid=(n_blk,),
            in_specs=[anyspace(), anyspace(), anyspace(), anyspace()],
            out_specs=anyspace(),
            scratch_shapes=[pltpu.VMEM((2, blk, d), F32), pltpu.VMEM((2, blk, d), F32),
                            pltpu.VMEM((d, de), F32), pltpu.VMEM((d, de), F32), pltpu.VMEM((de, d), F32),
                            pltpu.SemaphoreType.DMA((2,)), pltpu.SemaphoreType.DMA((3,)),
                            pltpu.SemaphoreType.DMA((2,))]),
        compiler_params=_cparams(("arbitrary",)),
        name="experts",
    )(blk_e, blk_cnt, n_used, row_src, h2, w1, w3, w2)


def _combine_kernel(dest_ref, ys_hbm, x_ref, wt_ref, gt_ref, o_ref, ybuf, sem):
    i = pl.program_id(0)
    nb = pl.num_programs(0)
    tm = x_ref.shape[0]
    n_tok = tm * nb

    def issue(blk_i, slot):
        for k in range(2):
            _gather_rows(lambda r: dest_ref[k * n_tok + blk_i * tm + r], tm, ys_hbm, ybuf.at[slot, k],
                         sem.at[slot])

    slot = lax.rem(i, 2)

    @pl.when(i == 0)
    def _():
        issue(0, 0)

    @pl.when(i + 1 < nb)
    def _():
        issue(i + 1, 1 - slot)

    for k in range(2):
        _wait_rows(tm, ys_hbm, ybuf.at[slot, k], sem.at[slot])
    w = wt_ref[...]
    y = ybuf[slot, 0] * w[:, 0:1] + ybuf[slot, 1] * w[:, 1:2]
    o_ref[...] = x_ref[...] + gt_ref[...] * y


def _combine(dest, ys, x2, wts, gt2):
    s, d = x2.shape
    tm = min(256, s)
    return pl.pallas_call(
        _combine_kernel,
        out_shape=jax.ShapeDtypeStruct((s, d), F32),
        grid_spec=pltpu.PrefetchScalarGridSpec(
            num_scalar_prefetch=1,
            grid=(s // tm,),
            in_specs=[pl.BlockSpec(memory_space=pl.ANY),
                      pl.BlockSpec((tm, d), lambda i, ds_: (i, 0)),
                      pl.BlockSpec((tm, LANES), lambda i, ds_: (i, 0)),
                      pl.BlockSpec((1, d), lambda i, ds_: (0, 0))],
            out_specs=pl.BlockSpec((tm, d), lambda i, ds_: (i, 0)),
            scratch_shapes=[pltpu.VMEM((2, 2, tm, d), F32), pltpu.SemaphoreType.DMA((2,))]),
        compiler_params=_cparams(("arbitrary",)),
        name="combine",
    )(dest, ys, x2, wts, gt2)


def _layer(x2d, c, w_ada, b_ada, g_mix, g_ffn, w_in, conv_w, q_norm_g, k_norm_g, g_branch, w_out,
           w_group, w_expert, w1, w3, w2):
    s, d = x2d.shape
    conv_ch = conv_w.shape[1]
    attn_w = d - conv_ch
    n_heads = attn_w // HEAD_DIM
    na = s // RES
    assert s % (RES * BAND) == 0
    row = lambda v: v.reshape(1, -1)

    mod_a = _ada(c, w_ada, b_ada, 2 * d)
    sh1, sc1 = mod_a[:, :d], mod_a[:, d:]

    h1 = _norm1(x2d, row(g_mix), sc1, sh1).reshape(s, d)
    z_conv, w_out_bf = _inproj_conv(h1, w_in, 3 * conv_ch, w_out)
    z_attn, mod_b = _inproj(h1, w_in, row(q_norm_g), row(k_norm_g), 3 * conv_ch, attn_w, c, w_ada, b_ada, 2 * d)
    gt1, sh2, sc2, gt2 = [mod_b[:, i * d:(i + 1) * d] for i in range(4)]
    y_conv = _conv(z_conv.reshape(RES, na, 3 * conv_ch), conv_w, row(g_branch[:conv_ch]), conv_ch)
    slopes = jnp.exp2(-ALIBI_MAX_EXP * jnp.arange(1, n_heads + 1, dtype=F32) / n_heads)
    y_attn = _attention(z_attn.reshape(RES, na, 3 * attn_w), slopes, attn_w)
    x2 = _outproj(y_conv, y_attn, row(g_branch[conv_ch:]), w_out_bf, x2d, gt1)

    w_r = jnp.concatenate([w_group, w_expert,
                           jnp.zeros((d, LANES - N_GROUPS - N_EXPERTS), F32)], axis=1)
    h2, eid, wts = _router(x2, row(g_ffn), sc2, sh2, w_r)
    rank, counts = _rank(eid)
    eid_flat = eid[:, :2].T.reshape(-1)
    rank_flat = rank[:, :2].T.reshape(-1)
    cnt = counts[0, :N_EXPERTS].astype(jnp.int32)
    dest, row_src, blk_e, blk_cnt, n_used = _plan(eid_flat, rank_flat, cnt, MOE_BLOCK)
    ys = _experts(h2, w1, w3, w2, blk_e, blk_cnt, n_used, row_src, MOE_BLOCK)
    return _combine(dest, ys, x2, wts, gt2)


def kernel(x, c, w_ada, b_ada, g_mix, g_ffn, w_in, conv_w, q_norm_g, k_norm_g, g_branch, w_out,
           w_group, w_expert, w1, w3, w2):
    b, s, d = x.shape
    assert b == 1 and c.shape[0] == 1
    for l in range(w_ada.shape[0]):
        x = _layer(x.reshape(s, d), c[0], w_ada[l], b_ada[l], g_mix[l], g_ffn[l], w_in[l], conv_w[l],
                   q_norm_g[l], k_norm_g[l], g_branch[l], w_out[l], w_group[l], w_expert[l],
                   w1[l], w3[l], w2[l]).reshape(b, s, d)
    return x
```

```python
import functools
import math

import jax
import jax.numpy as jnp
from jax import lax
from jax.experimental import pallas as pl
from jax.experimental.pallas import tpu as pltpu

EPS = 1e-6
HEAD_DIM = 128
BAND = 128
PATTERNS = ((128, 1), (512, 4), (2048, 16))
RES = 16
ALIBI_MAX_EXP = 8.0
N_GROUPS = 8
EXPERTS_PER_GROUP = 8
N_EXPERTS = N_GROUPS * EXPERTS_PER_GROUP
MOE_BLOCK = 256
GATHER_GROUP = 8
Y_TILE = 512
OUT_CHUNK = 64
ADA_SLAB = 128
PERM_ROWS = 16
CAST_ROWS = 128
LANES = 128
SUBLANES = 8
VMEM_LIMIT = 56 * 1024 * 1024
F32 = jnp.float32
BF16 = jnp.bfloat16
NEG_INF = float("-inf")
LOG2E = 1.4426950408889634


def _cparams(sem):
    return pltpu.CompilerParams(dimension_semantics=sem, vmem_limit_bytes=VMEM_LIMIT)


def _ada_kernel(c_ref, w_ref, b_ref, o_ref, s_ref, *, rows):
    @pl.when(pl.program_id(0) == 0)
    def _():
        c = c_ref[...]
        s_ref[...] = c * jax.nn.sigmoid(c)

    d, tn = w_ref.shape

    def body(i, acc):
        r0 = pl.multiple_of(i * rows, rows)
        p = w_ref[pl.ds(r0, rows), :] * s_ref[pl.ds(r0, rows), :]
        return acc + jnp.sum(p.reshape(rows // SUBLANES, SUBLANES, tn), axis=0)

    acc = lax.fori_loop(0, d // rows, body, jnp.zeros((SUBLANES, tn), F32))
    o_ref[...] = jnp.sum(acc, axis=0, keepdims=True) + b_ref[...]


def _ada_slab(w_ref, s_ref, b_ref):
    d, tn = w_ref.shape
    rows = min(512, d)
    acc = jnp.zeros((SUBLANES, tn), F32)
    for r0 in range(0, d, rows):
        p = w_ref[r0:r0 + rows, :] * s_ref[r0:r0 + rows, :]
        acc = acc + jnp.sum(p.reshape(rows // SUBLANES, SUBLANES, tn), axis=0)
    return jnp.sum(acc, axis=0, keepdims=True) + b_ref[...]


def _ada(c, w_ada, b_ada, n):
    d = w_ada.shape[0]
    tn = min(512, n)
    rows = min(512, d)
    return pl.pallas_call(
        functools.partial(_ada_kernel, rows=rows),
        out_shape=jax.ShapeDtypeStruct((1, n), F32),
        grid=(n // tn,),
        in_specs=[pl.BlockSpec((d, 1), lambda j: (0, 0)),
                  pl.BlockSpec((d, tn), lambda j: (0, j)),
                  pl.BlockSpec((1, tn), lambda j: (0, j))],
        out_specs=pl.BlockSpec((1, tn), lambda j: (0, j)),
        scratch_shapes=[pltpu.VMEM((d, 1), F32)],
        compiler_params=_cparams(("arbitrary",)),
        name="ada",
    )(c.reshape(d, 1), w_ada, b_ada.reshape(1, -1))


def _rms_mod(x, g, sc, sh):
    ms = jnp.mean(x * x, axis=-1, keepdims=True)
    return (x * lax.rsqrt(ms + EPS) * g) * (1.0 + sc) + sh


def _row_permutation(ta, to_residue_major):
    n = RES * ta
    assert ta & (ta - 1) == 0
    sh = ta.bit_length() - 1
    r = lax.broadcasted_iota(jnp.int32, (n, n), 0)
    c = lax.broadcasted_iota(jnp.int32, (n, n), 1)
    res_major, natural = (r, c) if to_residue_major else (c, r)
    return (natural == RES * (res_major & (ta - 1)) + (res_major >> sh)).astype(BF16)


def _norm1_kernel(x_ref, g_ref, sc_ref, sh_ref, o_ref):
    ta = o_ref.shape[1]
    h = _rms_mod(x_ref[...], g_ref[...], sc_ref[...], sh_ref[...]).astype(BF16)
    hp = jnp.dot(_row_permutation(ta, True), h, preferred_element_type=F32)
    o_ref[...] = hp.astype(BF16).reshape(o_ref.shape)


def _norm1(x2d, g_mix, sc1, sh1):
    s, d = x2d.shape
    na = s // RES
    ta = min(32, na)
    vec = lambda: pl.BlockSpec((1, d), lambda i: (0, 0))
    return pl.pallas_call(
        _norm1_kernel,
        out_shape=jax.ShapeDtypeStruct((RES, na, d), BF16),
        grid=(na // ta,),
        in_specs=[pl.BlockSpec((RES * ta, d), lambda i: (i, 0)), vec(), vec(), vec()],
        out_specs=pl.BlockSpec((RES, ta, d), lambda i: (0, i, 0)),
        compiler_params=_cparams(("arbitrary",)),
        name="norm1",
    )(x2d, g_mix, sc1, sh1)


def _inproj_kernel(h_ref, w_ref, qg_ref, kg_ref, c_ref, wa_ref, ba_ref, z_ref, mod_ref, wb_ref, s_ref,
                   *, q0, k0, v0):
    j = pl.program_id(0)

    @pl.when((j == 0) & (pl.program_id(1) == 0))
    def _():
        c = c_ref[...]
        s_ref[...] = c * jax.nn.sigmoid(c)

    @pl.when(pl.program_id(1) == 0)
    def _():
        wb_ref[...] = w_ref[...].astype(BF16)

    mod_ref[...] = _ada_slab(wa_ref, s_ref, ba_ref)
    acc = jnp.dot(h_ref[...], wb_ref[...], preferred_element_type=F32)
    is_q = (j >= q0) & (j < k0)
    is_k = (j >= k0) & (j < v0)

    g = jnp.where(is_q, qg_ref[...] * (HEAD_DIM ** -0.5 * LOG2E), kg_ref[...])
    for hh in range(acc.shape[1] // HEAD_DIM):
        a = acc[:, hh * HEAD_DIM:(hh + 1) * HEAD_DIM]
        ms = jnp.mean(a * a, axis=-1, keepdims=True)
        scale = jnp.where(is_q | is_k, lax.rsqrt(ms + EPS) * g, 1.0)
        z_ref[:, hh * HEAD_DIM:(hh + 1) * HEAD_DIM] = a * scale


def _inproj_conv_kernel(h_ref, w_ref, wo_ref, z_ref, wob_ref, wb_ref):
    @pl.when(pl.program_id(1) == 0)
    def _():
        wb_ref[...] = w_ref[...].astype(BF16)

    wob_ref[...] = wo_ref[...].astype(BF16)
    z_ref[...] = jnp.dot(h_ref[...], wb_ref[...], preferred_element_type=F32).astype(BF16)


def _inproj_conv(h1, w_in, n_cols, w_out):
    s, d = h1.shape
    tm = min(1024, s)
    tn = math.gcd(512, n_cols)
    ni = s // tm
    wr, wc = w_out.shape
    ws = min(CAST_ROWS, wr)
    while wr // ws > (n_cols // tn) * ni:
        ws *= 2
    n_ws = wr // ws
    assert wr % ws == 0
    wslab = lambda: pl.BlockSpec((ws, wc), lambda j, i: (jnp.minimum(j * ni + i, n_ws - 1), 0))
    return pl.pallas_call(
        _inproj_conv_kernel,
        out_shape=(jax.ShapeDtypeStruct((s, n_cols), BF16), jax.ShapeDtypeStruct((wr, wc), BF16)),
        grid=(n_cols // tn, ni),
        in_specs=[pl.BlockSpec((tm, d), lambda j, i: (i, 0)),
                  pl.BlockSpec((d, tn), lambda j, i: (0, j)), wslab()],
        out_specs=(pl.BlockSpec((tm, tn), lambda j, i: (i, j)), wslab()),
        scratch_shapes=[pltpu.VMEM((d, tn), BF16)],
        compiler_params=_cparams(("arbitrary", "arbitrary")),
        name="inproj_conv",
    )(h1, w_in, w_out)


def _inproj(h1, w_in, q_g, k_g, col0, attn_w, c, w_ada, b_ada, mod_done):
    s, d = h1.shape
    n = w_in.shape[1] - col0
    tm = min(1024, s)
    tn = math.gcd(512, attn_w, col0)
    assert n == 3 * attn_w and tn % HEAD_DIM == 0
    j0 = col0 // tn
    q0 = 0
    k0 = q0 + attn_w // tn
    v0 = k0 + attn_w // tn
    ni = s // tm
    n_mod = w_ada.shape[1] - mod_done
    n_slab = n_mod // ADA_SLAB
    assert n_mod % ADA_SLAB == 0 and mod_done % ADA_SLAB == 0 and n_slab <= (n // tn) * ni
    slab = lambda j, i: jnp.minimum(j * ni + i, n_slab - 1)
    hvec = lambda: pl.BlockSpec((1, HEAD_DIM), lambda j, i: (0, 0))
    return pl.pallas_call(
        functools.partial(_inproj_kernel, q0=q0, k0=k0, v0=v0),
        out_shape=(jax.ShapeDtypeStruct((s, n), F32), jax.ShapeDtypeStruct((1, n_mod), F32)),
        grid=(n // tn, ni),
        in_specs=[pl.BlockSpec((tm, d), lambda j, i: (i, 0)),
                  pl.BlockSpec((d, tn), lambda j, i: (0, j0 + j)), hvec(), hvec(),
                  pl.BlockSpec((d, 1), lambda j, i: (0, 0)),
                  pl.BlockSpec((d, ADA_SLAB), lambda j, i: (0, mod_done // ADA_SLAB + slab(j, i))),
                  pl.BlockSpec((1, ADA_SLAB), lambda j, i: (0, mod_done // ADA_SLAB + slab(j, i)))],
        out_specs=(pl.BlockSpec((tm, tn), lambda j, i: (i, j)),
                   pl.BlockSpec((1, ADA_SLAB), lambda j, i: (0, slab(j, i)))),
        scratch_shapes=[pltpu.VMEM((d, tn), BF16), pltpu.VMEM((d, 1), F32)],
        compiler_params=_cparams(("arbitrary", "arbitrary")),
        name="inproj",
    )(h1, w_in, q_g, k_g, c.reshape(d, 1), w_ada, b_ada.reshape(1, -1))


def _conv_kernel(b_ref, c_ref, v_ref, cw_ref, g_ref, o_ref, ue_ref):
    i = pl.program_id(0)
    ta = b_ref.shape[1]
    w = cw_ref[...]
    assert w.shape[0] == 3
    for k in range(2):
        @pl.when(i == 0)
        def _():
            ue_ref[k, 0:SUBLANES, :] = jnp.zeros((SUBLANES, ue_ref.shape[2]), F32)

        @pl.when(i > 0)
        def _():
            ue_ref[k, 0:SUBLANES, :] = ue_ref[k, ta:ta + SUBLANES, :]

        r = RES - 2 + k
        ue_ref[k, SUBLANES:SUBLANES + ta, :] = c_ref[r].astype(F32) * v_ref[r].astype(F32)

    def u_at(bb):
        if bb >= 0:
            return c_ref[bb].astype(F32) * v_ref[bb].astype(F32)
        return ue_ref[2 + bb, SUBLANES - 1:SUBLANES - 1 + ta, :]

    for b in range(RES):
        y = b_ref[b].astype(F32) * (w[0:1, :] * u_at(b - 2) + w[1:2, :] * u_at(b - 1) + w[2:3, :] * u_at(b))
        ms = jnp.mean(y * y, axis=-1, keepdims=True)
        o_ref[b] = (y * lax.rsqrt(ms + EPS) * g_ref[...]).astype(BF16)


def _conv(z3, conv_w, g_conv, conv_ch):
    na = z3.shape[1]
    ta = min(64, na)
    blk = lambda jj: pl.BlockSpec((RES, ta, conv_ch), lambda i: (0, i, jj))
    return pl.pallas_call(
        _conv_kernel,
        out_shape=jax.ShapeDtypeStruct((RES, na, conv_ch), BF16),
        grid=(na // ta,),
        in_specs=[blk(0), blk(1), blk(2),
                  pl.BlockSpec(conv_w.shape, lambda i: (0, 0)),
                  pl.BlockSpec((1, conv_ch), lambda i: (0, 0))],
        out_specs=pl.BlockSpec((RES, ta, conv_ch), lambda i: (0, i, 0)),
        scratch_shapes=[pltpu.VMEM((2, ta + SUBLANES, conv_ch), F32)],
        compiler_params=_cparams(("arbitrary",)),
        name="conv",
    )(z3, z3, z3, conv_w, g_conv)


def _attn_kernel(slopes_ref, q_ref, kp_ref, kc_ref, vp_ref, vc_ref, o_ref, bias_ref, m_ref, l_ref, acc_ref):
    h = pl.program_id(0)
    cidx = pl.program_id(1)
    slope = slopes_ref[h] * LOG2E
    first_pen = jnp.where(cidx > 0, 0.0, NEG_INF).astype(F32)
    dn_t = (((1,), (1,)), ((), ()))
    ones_kv = jnp.ones((2 * BAND, HEAD_DIM), BF16)

    j = lax.broadcasted_iota(jnp.int32, (BAND, 2 * BAND), 0)
    u = lax.broadcasted_iota(jnp.int32, (BAND, 2 * BAND), 1)
    is_prev = u < BAND
    uu = jnp.where(is_prev, u, u - BAND)
    for pi, (window, dil) in enumerate(PATTERNS):
        assert window // dil == BAND and RES % dil == 0
        ns, sr = RES // dil, BAND * dil // RES
        assert sr & (sr - 1) == 0
        sh = sr.bit_length() - 1
        mq = (j & (sr - 1)) * ns + (j >> sh)
        mk = (uu & (sr - 1)) * ns + (uu >> sh)
        steps = mq - mk + jnp.where(is_prev, BAND, 0)
        valid = (is_prev & (steps <= BAND)) | (jnp.logical_not(is_prev) & (steps >= 0))
        bias = jnp.where(valid, -slope * (steps * dil).astype(F32), NEG_INF)
        bias_ref[pi, 0] = bias
        bias_ref[pi, 1] = bias + jnp.where(is_prev, first_pen, 0.0)

    for pi, (window, dil) in enumerate(PATTERNS):
        ns, sr = RES // dil, BAND * dil // RES
        n_chunks = BAND // sr
        for r in range(dil):
            k_prev = v_prev = None
            for n in range(n_chunks):
                def gather(ref, nn):
                    return jnp.concatenate([ref[dil * c + r, nn * sr:(nn + 1) * sr, :] for c in range(ns)], axis=0)

                q = gather(q_ref, n).astype(BF16)
                if n == 0:
                    k_prev = gather(kp_ref, n_chunks - 1).astype(BF16)
                    v_prev = gather(vp_ref, n_chunks - 1).astype(BF16)
                k_cur, v_cur = gather(kc_ref, n).astype(BF16), gather(vc_ref, n).astype(BF16)
                kcat = jnp.concatenate([k_prev, k_cur], axis=0)
                vcat = jnp.concatenate([v_prev, v_cur], axis=0)
                k_prev, v_prev = k_cur, v_cur
                s = lax.dot_general(q, kcat, dn_t, preferred_element_type=F32) + bias_ref[pi, 1 if n == 0 else 0]
                m = jnp.max(s, axis=-1, keepdims=True)
                p = jnp.exp2(s - m).astype(BF16)
                ol = jnp.dot(p, jnp.concatenate([vcat, ones_kv], axis=1), preferred_element_type=F32)
                o, l_all = ol[:, :HEAD_DIM], ol[:, HEAD_DIM:]
                for c in range(ns):
                    rows = slice(c * sr, (c + 1) * sr)
                    at = (dil * c + r, slice(n * sr, (n + 1) * sr), slice(None))
                    m_b = jnp.broadcast_to(m[rows], (sr, HEAD_DIM))
                    l_b = l_all[rows]
                    if pi == 0:
                        m_ref[at] = m_b
                        l_ref[at] = l_b
                        acc_ref[at] = o[rows]
                    else:
                        m_old = m_ref[at]
                        m_new = jnp.maximum(m_old, m_b)
                        a_old = jnp.exp2(m_old - m_new)
                        a_new = jnp.exp2(m_b - m_new)
                        l_new = a_old * l_ref[at] + a_new * l_b
                        o_new = a_old * acc_ref[at] + a_new * o[rows]
                        if pi == len(PATTERNS) - 1:
                            acc_ref[at] = o_new / l_new
                        else:
                            m_ref[at] = m_new
                            l_ref[at] = l_new
                            acc_ref[at] = o_new
    o_ref[...] = acc_ref[...].astype(BF16)


def _attention(z3, slopes, attn_w):
    na = z3.shape[1]
    n_heads = attn_w // HEAD_DIM
    assert na % BAND == 0
    qb = 0
    kb = qb + n_heads
    vb = kb + n_heads
    blk = (RES, BAND, HEAD_DIM)
    cur = lambda base: pl.BlockSpec(blk, lambda h, c, sl: (0, c, base + h))
    prev = lambda base: pl.BlockSpec(blk, lambda h, c, sl: (0, jnp.maximum(c - 1, 0), base + h))
    return pl.pallas_call(
        _attn_kernel,
        out_shape=jax.ShapeDtypeStruct((RES, na, attn_w), BF16),
        grid_spec=pltpu.PrefetchScalarGridSpec(
            num_scalar_prefetch=1,
            grid=(n_heads, na // BAND),
            in_specs=[cur(qb), prev(kb), cur(kb), prev(vb), cur(vb)],
            out_specs=pl.BlockSpec(blk, lambda h, c, sl: (0, c, h)),
            scratch_shapes=[pltpu.VMEM((len(PATTERNS), 2, BAND, 2 * BAND), F32),
                            pltpu.VMEM(blk, F32), pltpu.VMEM(blk, F32), pltpu.VMEM(blk, F32)]),
        compiler_params=_cparams(("arbitrary", "arbitrary")),
        name="attn",
    )(slopes, z3, z3, z3, z3, z3)


def _outproj_kernel(yc_ref, ya_ref, ga_ref, w_ref, x_ref, gt_ref, o_ref, y_ref):
    _, ta, cc = yc_ref.shape

    @pl.when(pl.program_id(1) == 0)
    def _():
        tp = min(PERM_ROWS, ta)
        n = RES * tp
        perm = _row_permutation(tp, False)
        for t0 in range(0, ta, tp):
            ya = ya_ref[:, t0:t0 + tp, :].reshape(n, ya_ref.shape[2]).astype(F32)
            ms = jnp.mean(ya * ya, axis=-1, keepdims=True)
            yn = (ya * lax.rsqrt(ms + EPS) * ga_ref[...]).astype(BF16)
            yc = yc_ref[:, t0:t0 + tp, :].reshape(n, cc)
            rows = slice(RES * t0, RES * t0 + n)
            y_ref[rows, 0:cc] = jnp.dot(perm, yc, preferred_element_type=F32).astype(BF16)
            y_ref[rows, cc:] = jnp.dot(perm, yn, preferred_element_type=F32).astype(BF16)

    acc = jnp.dot(y_ref[...], w_ref[...], preferred_element_type=F32)
    o_ref[...] = x_ref[...] + gt_ref[...] * acc


def _outproj(y_conv3, y_attn3, g_attn, w_out_bf, x2d, gt1):
    s, d = x2d.shape
    na = s // RES
    cc, aw = y_conv3.shape[2], y_attn3.shape[2]
    ta = min(64, na)
    tn = min(512, d)
    return pl.pallas_call(
        _outproj_kernel,
        out_shape=jax.ShapeDtypeStruct((s, d), F32),
        grid=(na // ta, d // tn),
        in_specs=[pl.BlockSpec((RES, ta, cc), lambda i, j: (0, i, 0)),
                  pl.BlockSpec((RES, ta, aw), lambda i, j: (0, i, 0)),
                  pl.BlockSpec((1, aw), lambda i, j: (0, 0)),
                  pl.BlockSpec((cc + aw, tn), lambda i, j: (0, j)),
                  pl.BlockSpec((RES * ta, tn), lambda i, j: (i, j)),
                  pl.BlockSpec((1, tn), lambda i, j: (0, j))],
        out_specs=pl.BlockSpec((RES * ta, tn), lambda i, j: (i, j)),
        scratch_shapes=[pltpu.VMEM((RES * ta, cc + aw), BF16)],
        compiler_params=_cparams(("arbitrary", "arbitrary")),
        name="outproj",
    )(y_conv3, y_attn3, g_attn, w_out_bf, x2d, gt1)


def _router_kernel(x_ref, g_ref, sc_ref, sh_ref, wr_ref, h_ref, eid_ref, wt_ref):
    h = _rms_mod(x_ref[...], g_ref[...], sc_ref[...], sh_ref[...])
    h_ref[...] = h
    w = wr_ref[...]
    h_hi, w_hi = h.astype(BF16), w.astype(BF16)
    h_lo = (h - h_hi.astype(F32)).astype(BF16)
    w_lo = (w - w_hi.astype(F32)).astype(BF16)
    logits = (jnp.dot(h_hi, w_hi, preferred_element_type=F32) + jnp.dot(h_hi, w_lo, preferred_element_type=F32)
              + jnp.dot(h_lo, w_hi, preferred_element_type=F32))
    tm = logits.shape[0]
    lane = lax.broadcasted_iota(jnp.int32, (tm, LANES), 1).astype(F32)
    big = float(LANES)
    is_g = lane < N_GROUPS
    gl = jnp.where(is_g, logits, NEG_INF)
    gmax = jnp.max(gl, axis=-1, keepdims=True)
    gsel = jnp.min(jnp.where(gl == gmax, lane, big), axis=-1, keepdims=True)
    g_w = 1.0 / jnp.sum(jnp.where(is_g, jnp.exp(gl - gmax), 0.0), axis=-1, keepdims=True)
    lo = N_GROUPS + gsel * EXPERTS_PER_GROUP
    in_grp = (lane >= lo) & (lane < lo + EXPERTS_PER_GROUP)
    el = jnp.where(in_grp, logits, NEG_INF)
    e1 = jnp.max(el, axis=-1, keepdims=True)
    i1 = jnp.min(jnp.where(el == e1, lane, big), axis=-1, keepdims=True)
    el2 = jnp.where(lane == i1, NEG_INF, el)
    e2 = jnp.max(el2, axis=-1, keepdims=True)
    i2 = jnp.min(jnp.where(el2 == e2, lane, big), axis=-1, keepdims=True)
    t = jnp.exp(e2 - e1)
    w1 = g_w / (1.0 + t)
    w2 = g_w * t / (1.0 + t)
    eid = jnp.where(lane == 0.0, i1 - N_GROUPS, jnp.where(lane == 1.0, i2 - N_GROUPS, 0.0))
    eid_ref[...] = eid.astype(jnp.int32)
    wt_ref[...] = jnp.where(lane == 0.0, w1, jnp.where(lane == 1.0, w2, 0.0))


def _router(x2, g_ffn, sc2, sh2, w_r):
    s, d = x2.shape
    tm = min(512, s)
    vec = lambda: pl.BlockSpec((1, d), lambda i: (0, 0))
    lanes_out = lambda: pl.BlockSpec((tm, LANES), lambda i: (i, 0))
    return pl.pallas_call(
        _router_kernel,
        out_shape=(jax.ShapeDtypeStruct((s, d), F32),
                   jax.ShapeDtypeStruct((s, LANES), jnp.int32),
                   jax.ShapeDtypeStruct((s, LANES), F32)),
        grid=(s // tm,),
        in_specs=[pl.BlockSpec((tm, d), lambda i: (i, 0)), vec(), vec(), vec(),
                  pl.BlockSpec((d, LANES), lambda i: (0, 0))],
        out_specs=(pl.BlockSpec((tm, d), lambda i: (i, 0)), lanes_out(), lanes_out()),
        compiler_params=_cparams(("arbitrary",)),
        name="router",
    )(x2, g_ffn, sc2, sh2, w_r)


def _rank_kernel(eid_ref, rank_ref, cnt_ref, run_ref):
    @pl.when(pl.program_id(0) == 0)
    def _():
        run_ref[...] = jnp.zeros_like(run_ref)

    tb = eid_ref.shape[0]
    lane = lax.broadcasted_iota(jnp.int32, (tb, LANES), 1)
    row = lax.broadcasted_iota(jnp.int32, (tb, tb), 0)
    col = lax.broadcasted_iota(jnp.int32, (tb, tb), 1)
    earlier = (col < row).astype(BF16)
    e = eid_ref[...]
    run = run_ref[...]
    ranks = []
    for k in range(2):
        onehot = e[:, k:k + 1] == lane
        oh = onehot.astype(F32)
        before = jnp.dot(earlier, oh.astype(BF16), preferred_element_type=F32) + run
        ranks.append(jnp.sum(jnp.where(onehot, before, 0.0), axis=-1, keepdims=True))
        run = run + jnp.sum(oh, axis=0, keepdims=True)
    rank_ref[...] = jnp.where(lane == 0, ranks[0], jnp.where(lane == 1, ranks[1], 0.0)).astype(jnp.int32)
    run_ref[...] = run
    cnt_ref[...] = run


def _rank(eid):
    s = eid.shape[0]
    tb = min(512, s)
    return pl.pallas_call(
        _rank_kernel,
        out_shape=(jax.ShapeDtypeStruct((s, LANES), jnp.int32), jax.ShapeDtypeStruct((1, LANES), F32)),
        grid=(s // tb,),
        in_specs=[pl.BlockSpec((tb, LANES), lambda i: (i, 0))],
        out_specs=(pl.BlockSpec((tb, LANES), lambda i: (i, 0)), pl.BlockSpec((1, LANES), lambda i: (0, 0))),
        scratch_shapes=[pltpu.VMEM((1, LANES), F32)],
        compiler_params=_cparams(("arbitrary",)),
        name="rank",
    )(eid)


def _plan_kernel(eid_ref, rank_ref, cnt_ref, dest_ref, src_ref, blke_ref, blkc_ref, nused_ref, pst_ref,
                 *, blk, n_blk):
    def per_expert(e, nb):
        c = cnt_ref[e]
        pst_ref[e] = nb * blk
        nbe = (c + (blk - 1)) // blk

        def per_block(jb, carry):
            blke_ref[nb + jb] = e
            blkc_ref[nb + jb] = jnp.minimum(blk, c - jb * blk)
            return carry

        lax.fori_loop(0, nbe, per_block, 0)

        def pad_slot(r, carry):
            src_ref[nb * blk + r] = 0
            return carry

        lax.fori_loop(c, (c + (GATHER_GROUP - 1)) // GATHER_GROUP * GATHER_GROUP, pad_slot, 0)
        return nb + nbe

    n_used = lax.fori_loop(0, N_EXPERTS, per_expert, jnp.int32(0))
    nused_ref[0] = n_used
    last_e = blke_ref[n_used - 1]

    def unused(i, carry):
        blke_ref[i] = last_e
        blkc_ref[i] = 0
        return carry

    lax.fori_loop(n_used, n_blk, unused, 0)

    n_tok = eid_ref.shape[0] // 2

    for k in range(eid_ref.shape[0] // n_tok):
        def place(tok, carry):
            a = k * n_tok + tok
            dst = pst_ref[eid_ref[a]] + rank_ref[a]
            dest_ref[a] = dst
            src_ref[dst] = tok
            return carry

        lax.fori_loop(0, n_tok, place, 0, unroll=16)


def _plan(eid_flat, rank_flat, counts, blk):
    a = eid_flat.shape[0]
    n_blk = a // blk + N_EXPERTS
    smem = lambda: pl.BlockSpec(memory_space=pltpu.SMEM)
    i32 = jnp.int32
    return pl.pallas_call(
        functools.partial(_plan_kernel, blk=blk, n_blk=n_blk),
        out_shape=(jax.ShapeDtypeStruct((a,), i32), jax.ShapeDtypeStruct((n_blk * blk,), i32),
                   jax.ShapeDtypeStruct((n_blk,), i32), jax.ShapeDtypeStruct((n_blk,), i32),
                   jax.ShapeDtypeStruct((1,), i32)),
        in_specs=[smem(), smem(), smem()],
        out_specs=(smem(), smem(), smem(), smem(), smem()),
        scratch_shapes=[pltpu.SMEM((N_EXPERTS,), i32)],
        name="plan",
    )(eid_flat, rank_flat, counts)


def _gather_rows(idx_of_row, n_rows, src_hbm, dst, sem, alternate=False):
    def body(g, carry):
        for rr in range(GATHER_GROUP):
            r = g * GATHER_GROUP + rr
            pltpu.make_async_copy(src_hbm.at[pl.ds(idx_of_row(r), 1), :], dst.at[pl.ds(r, 1), :], sem).start(
                priority=rr % 2 if alternate else 0)
        return carry
    lax.fori_loop(0, (n_rows + (GATHER_GROUP - 1)) // GATHER_GROUP, body, 0)


def _wait_rows(n_rows, src_hbm, dst, sem):
    def body(g, carry):
        pltpu.make_async_copy(src_hbm.at[pl.ds(0, GATHER_GROUP), :], dst.at[pl.ds(0, GATHER_GROUP), :], sem).wait()
        return carry
    lax.fori_loop(0, (n_rows + (GATHER_GROUP - 1)) // GATHER_GROUP, body, 0)


def _expert_kernel(blke_ref, blkc_ref, nused_ref, src_ref, h_hbm, w1_hbm, w3_hbm, w2_hbm, ys_hbm,
                   xbuf, obuf, w1b, w3b, w2b, gsem, wsem, osem, *, blk, kc):
    i = pl.program_id(0)
    last = pl.num_programs(0) - 1
    n_used = nused_ref[0]
    e = blke_ref[i]
    e_next = blke_ref[jnp.minimum(i + 1, last)]
    first_of_expert = (i == 0) | (blke_ref[jnp.maximum(i - 1, 0)] != e)
    fetch_next = (i + 1 < n_used) & (e_next != e)
    wbufs = ((w1_hbm, w1b), (w3_hbm, w3b), (w2_hbm, w2b))

    def weight_copy(k, ee):
        return pltpu.make_async_copy(wbufs[k][0].at[ee], wbufs[k][1], wsem.at[k])

    def issue_rows(b, slot):
        _gather_rows(lambda r: src_ref[b * blk + r], blkc_ref[b], h_hbm, xbuf.at[slot], gsem.at[slot])

    def weights_ready(k):
        @pl.when(first_of_expert)
        def _():
            weight_copy(k, e).wait()

    def weights_done(k):
        @pl.when(fetch_next)
        def _():
            weight_copy(k, e_next).start(priority=1)

    def out_copy(b, slot, ch):
        return pltpu.make_async_copy(obuf.at[slot, pl.ds(ch * OUT_CHUNK, OUT_CHUNK), :],
                                     ys_hbm.at[pl.ds(b * blk + ch * OUT_CHUNK, OUT_CHUNK), :], osem.at[slot])

    def for_out_chunks(b, fn):
        def body(ch, carry):
            fn(ch)
            return carry
        lax.fori_loop(0, (blkc_ref[b] + (OUT_CHUNK - 1)) // OUT_CHUNK, body, 0)

    slot = lax.rem(i, 2)

    @pl.when(i == 0)
    def _():
        for k in range(3):
            weight_copy(k, e).start(priority=1)
        xbuf[...] = jnp.zeros_like(xbuf)
        issue_rows(0, 0)

    @pl.when(i < n_used)
    def _():
        _wait_rows(blkc_ref[i], h_hbm, xbuf.at[slot], gsem.at[slot])

        @pl.when(i + 1 < n_used)
        def _():
            issue_rows(i + 1, 1 - slot)

        d = xbuf.shape[2]

        def up_proj(wb):
            acc = None
            for k0 in range(0, d, kc):
                part = jnp.dot(xbuf[slot, :, k0:k0 + kc].astype(BF16), wb[k0:k0 + kc, :].astype(BF16),
                               preferred_element_type=F32)
                acc = part if acc is None else acc + part
            return acc

        weights_ready(0)
        a = up_proj(w1b)
        weights_done(0)
        weights_ready(1)
        b = up_proj(w3b)
        weights_done(1)
        hid = ((a * jax.nn.sigmoid(a)) * b).astype(BF16)
        weights_ready(2)

        @pl.when(i >= 2)
        def _():
            for_out_chunks(i - 2, lambda ch: out_copy(i - 2, slot, 0).wait())

        for t in range(0, d, Y_TILE):
            obuf[slot, :, t:t + Y_TILE] = jnp.dot(hid, w2b[:, t:t + Y_TILE].astype(BF16),
                                                  preferred_element_type=F32)
        weights_done(2)
        for_out_chunks(i, lambda ch: out_copy(i, slot, ch).start())

        @pl.when(i == n_used - 1)
        def _():
            @pl.when(i >= 1)
            def _():
                for_out_chunks(i - 1, lambda ch: out_copy(i - 1, 1 - slot, 0).wait())
            for_out_chunks(i, lambda ch: out_copy(i, slot, 0).wait())


def _experts(h2, w1, w3, w2, blk_e, blk_cnt, n_used, row_src, blk):
    s, d = h2.shape
    de = w1.shape[-1]
    kc = min(1024, d)
    assert d % kc == 0 and d % Y_TILE == 0 and blk % GATHER_GROUP == 0 and blk % OUT_CHUNK == 0
    n_blk = blk_e.shape[0]
    anyspace = lambda: pl.BlockSpec(memory_space=pl.ANY)
    return pl.pallas_call(
        functools.partial(_expert_kernel, blk=blk, kc=kc),
        out_shape=jax.ShapeDtypeStruct((n_blk * blk, d), F32),
        grid_spec=pltpu.PrefetchScalarGridSpec(
            num_scalar_prefetch=4,
            grid=(n_blk,),
            in_specs=[anyspace(), anyspace(), anyspace(), anyspace()],
            out_specs=anyspace(),
            scratch_shapes=[pltpu.VMEM((2, blk, d), F32), pltpu.VMEM((2, blk, d), F32),
                            pltpu.VMEM((d, de), F32), pltpu.VMEM((d, de), F32), pltpu.VMEM((de, d), F32),
                            pltpu.SemaphoreType.DMA((2,)), pltpu.SemaphoreType.DMA((3,)),
                            pltpu.SemaphoreType.DMA((2,))]),
        compiler_params=_cparams(("arbitrary",)),
        name="experts",
    )(blk_e, blk_cnt, n_used, row_src, h2, w1, w3, w2)


def _combine_kernel(dest_ref, ys_hbm, x_ref, wt_ref, gt_ref, o_ref, ybuf, sem):
    i = pl.program_id(0)
    nb = pl.num_programs(0)
    tm = x_ref.shape[0]
    n_tok = tm * nb

    def issue(blk_i, slot):
        for k in range(2):
            _gather_rows(lambda r: dest_ref[k * n_tok + blk_i * tm + r], tm, ys_hbm, ybuf.at[slot, k],
                         sem.at[slot], alternate=True)

    slot = lax.rem(i, 2)

    @pl.when(i == 0)
    def _():
        issue(0, 0)

    @pl.when(i + 1 < nb)
    def _():
        issue(i + 1, 1 - slot)

    for k in range(2):
        _wait_rows(tm, ys_hbm, ybuf.at[slot, k], sem.at[slot])
    w = wt_ref[...]
    y = ybuf[slot, 0] * w[:, 0:1] + ybuf[slot, 1] * w[:, 1:2]
    o_ref[...] = x_ref[...] + gt_ref[...] * y


def _combine(dest, ys, x2, wts, gt2):
    s, d = x2.shape
    tm = min(256, s)
    return pl.pallas_call(
        _combine_kernel,
        out_shape=jax.ShapeDtypeStruct((s, d), F32),
        grid_spec=pltpu.PrefetchScalarGridSpec(
            num_scalar_prefetch=1,
            grid=(s // tm,),
            in_specs=[pl.BlockSpec(memory_space=pl.ANY),
                      pl.BlockSpec((tm, d), lambda i, ds_: (i, 0)),
                      pl.BlockSpec((tm, LANES), lambda i, ds_: (i, 0)),
                      pl.BlockSpec((1, d), lambda i, ds_: (0, 0))],
            out_specs=pl.BlockSpec((tm, d), lambda i, ds_: (i, 0)),
            scratch_shapes=[pltpu.VMEM((2, 2, tm, d), F32), pltpu.SemaphoreType.DMA((2,))]),
        compiler_params=_cparams(("arbitrary",)),
        name="combine",
    )(dest, ys, x2, wts, gt2)


def _layer(x2d, c, w_ada, b_ada, g_mix, g_ffn, w_in, conv_w, q_norm_g, k_norm_g, g_branch, w_out,
           w_group, w_expert, w1, w3, w2):
    s, d = x2d.shape
    conv_ch = conv_w.shape[1]
    attn_w = d - conv_ch
    n_heads = attn_w // HEAD_DIM
    na = s // RES
    assert s % (RES * BAND) == 0
    row = lambda v: v.reshape(1, -1)

    mod_a = _ada(c, w_ada, b_ada, 2 * d)
    sh1, sc1 = mod_a[:, :d], mod_a[:, d:]

    h1 = _norm1(x2d, row(g_mix), sc1, sh1).reshape(s, d)
    z_conv, w_out_bf = _inproj_conv(h1, w_in, 3 * conv_ch, w_out)
    z_attn, mod_b = _inproj(h1, w_in, row(q_norm_g), row(k_norm_g), 3 * conv_ch, attn_w, c, w_ada, b_ada, 2 * d)
    gt1, sh2, sc2, gt2 = [mod_b[:, i * d:(i + 1) * d] for i in range(4)]
    y_conv = _conv(z_conv.reshape(RES, na, 3 * conv_ch), conv_w, row(g_branch[:conv_ch]), conv_ch)
    slopes = jnp.exp2(-ALIBI_MAX_EXP * jnp.arange(1, n_heads + 1, dtype=F32) / n_heads)
    y_attn = _attention(z_attn.reshape(RES, na, 3 * attn_w), slopes, attn_w)
    x2 = _outproj(y_conv, y_attn, row(g_branch[conv_ch:]), w_out_bf, x2d, gt1)

    w_r = jnp.concatenate([w_group, w_expert,
                           jnp.zeros((d, LANES - N_GROUPS - N_EXPERTS), F32)], axis=1)
    h2, eid, wts = _router(x2, row(g_ffn), sc2, sh2, w_r)
    rank, counts = _rank(eid)
    eid_flat = eid[:, :2].T.reshape(-1)
    rank_flat = rank[:, :2].T.reshape(-1)
    cnt = counts[0, :N_EXPERTS].astype(jnp.int32)
    dest, row_src, blk_e, blk_cnt, n_used = _plan(eid_flat, rank_flat, cnt, MOE_BLOCK)
    ys = _experts(h2, w1, w3, w2, blk_e, blk_cnt, n_used, row_src, MOE_BLOCK)
    return _combine(dest, ys, x2, wts, gt2)


def kernel(x, c, w_ada, b_ada, g_mix, g_ffn, w_in, conv_w, q_norm_g, k_norm_g, g_branch, w_out,
           w_group, w_expert, w1, w3, w2):
    b, s, d = x.shape
    assert b == 1 and c.shape[0] == 1
    for l in range(w_ada.shape[0]):
        x = _layer(x.reshape(s, d), c[0], w_ada[l], b_ada[l], g_mix[l], g_ffn[l], w_in[l], conv_w[l],
                   q_norm_g[l], k_norm_g[l], g_branch[l], w_out[l], w_group[l], w_expert[l],
                   w1[l], w3[l], w2[l]).reshape(b, s, d)
    return x
```
